```python
import math
import jax, jax.numpy as jnp
from jax import lax
import numpy as np

D_MODEL = 2048
BATCH = 2
SEQ = 4096
DEPTH = 2

MEM_LEN = 256
RWKV_HEADS = 16
RWKV_HEAD_DIM = 64
RWKV_WIDTH = RWKV_HEADS * RWKV_HEAD_DIM
ATT_Q_HEADS = 16
ATT_KV_HEADS = 4
ATT_HEAD_DIM = 64
ATT_WIDTH = ATT_Q_HEADS * ATT_HEAD_DIM
ATT_KV_WIDTH = ATT_KV_HEADS * ATT_HEAD_DIM
MIX_WIDTH = RWKV_WIDTH + ATT_WIDTH
WINDOW = 128
BLOCK = 128
DECAY_RANK = 96
ICLR_RANK = 96
VRES_RANK = 64
GATE_RANK = 256
GN_EPS = 64e-5
RWKV_COLS = 3 * RWKV_WIDTH + 2 * DECAY_RANK + 2 * ICLR_RANK + GATE_RANK
IN_COLS = RWKV_COLS + ATT_WIDTH + 2 * ATT_KV_WIDTH
N_BUCKETS = 32
MAX_DISTANCE = 128
XATT_HEADS = 4
XATT_HEAD_DIM = 128
XATT_WIDTH = XATT_HEADS * XATT_HEAD_DIM
N_GROUPS = 4
EXPERTS_PER_GROUP = 8
TOP_K = 2
D_EXPERT = 512
EPS = 1e-6
NEG_INF = -1e30

kernel_name = "hymba_rwkv7_swa_hmoe_encoder"


def _split(x, sizes):
    idx, acc = [], 0
    for s in sizes[:-1]:
        acc += s
        idx.append(acc)
    return jnp.split(x, idx, axis=-1)


def rms_norm(x, g):
    xf = x.astype(jnp.float32)
    y = xf * lax.rsqrt(jnp.mean(xf * xf, axis=-1, keepdims=True) + EPS)
    return (y * g.astype(jnp.float32)).astype(x.dtype)


def centred_shift(u, mu_prev, mu_next):
    prev = jnp.pad(u, ((0, 0), (1, 0), (0, 0)))[:, :-1]
    nxt = jnp.pad(u, ((0, 0), (0, 1), (0, 0)))[:, 1:]
    return u + mu_prev * (prev - u) + mu_next * (nxt - u)


def _rwkv7_step(S, inp):
    r, w, k, v, a, b = inp
    Sa = jnp.einsum('zbhvk,zbhk->zbhv', S, a)
    S = S * w[..., None, :] + Sa[..., :, None] * b[..., None, :] + v[..., :, None] * k[..., None, :]
    return S, jnp.einsum('zbhvk,zbhk->zbhv', S, r)


def rwkv7_bidirectional(u, v_first, decay_w0, decay_w2, iclr_a0, iclr_a2, gate_w2, vres,
                        k_k, k_a, r_k, ln_g, ln_b):
    B, T, _ = u.shape
    C, H, N = RWKV_WIDTH, RWKV_HEADS, RWKV_HEAD_DIM
    f32 = jnp.float32
    r, k, v, wd, ad, gd = _split(u, (C, C, C, 2 * DECAY_RANK, 2 * ICLR_RANK, GATE_RANK))
    wd = wd.reshape(B, T, 2, DECAY_RANK)
    ad = ad.reshape(B, T, 2, ICLR_RANK)
    w_pre = (decay_w0 + jnp.einsum('btzr,zrc->btzc', jnp.tanh(wd), decay_w2)).astype(f32)
    decay = jnp.exp(-jnp.exp(-jax.nn.softplus(-w_pre) - 0.5))
    a = jax.nn.sigmoid((iclr_a0 + jnp.einsum('btzr,zrc->btzc', ad, iclr_a2)).astype(f32))
    g = jax.nn.sigmoid(gd) @ gate_w2
    if vres is None:
        v_first = v
    else:
        v0, v1, v2 = vres
        v = v + (v_first - v) * jax.nn.sigmoid(v0 + (v @ v1) @ v2)
    kk = (k * k_k).astype(f32).reshape(B, T, H, N)
    kk = kk / jnp.maximum(jnp.sqrt(jnp.sum(kk * kk, axis=-1, keepdims=True)), 1e-12)
    kd = k.astype(f32)[:, :, None, :] * (1.0 + (a - 1.0) * k_a.astype(f32))

    def per_dir(t):
        t = t.reshape(B, T, 2, H, N)
        t = jnp.stack([t[:, :, 0], jnp.flip(t[:, :, 1], axis=1)])
        return jnp.moveaxis(t, 2, 0)

    def shared(t):
        t = t.astype(f32).reshape(B, T, H, N)
        return jnp.moveaxis(jnp.stack([t, jnp.flip(t, axis=1)]), 2, 0)

    kk_s = shared(kk)
    xs = (shared(r), per_dir(decay), per_dir(kd), shared(v), -kk_s, kk_s * per_dir(a))
    S0 = jnp.zeros((2, B, H, N, N), f32)
    _, ys = lax.scan(_rwkv7_step, S0, xs)
    y = jnp.moveaxis(ys[:, 0] + jnp.flip(ys[:, 1], axis=0), 0, 1)
    mu = jnp.mean(y, axis=-1, keepdims=True)
    var = jnp.mean((y - mu) ** 2, axis=-1, keepdims=True)
    yn = ((y - mu) * lax.rsqrt(var + GN_EPS)).reshape(B, T, C) * ln_g + ln_b
    rh = r.astype(f32).reshape(B, T, H, N)
    kb = (kd[:, :, 0] + kd[:, :, 1]).reshape(B, T, H, N)
    bonus = jnp.sum(rh * kb * r_k.astype(f32).reshape(H, N), axis=-1, keepdims=True) * v.astype(f32).reshape(B, T, H, N)
    out = (yn + bonus.reshape(B, T, C)) * g
    return out.astype(u.dtype), v_first


def t5_bucket(rel):
    nb = N_BUCKETS // 2
    max_exact = nb // 2
    ret = jnp.where(rel > 0, nb, 0)
    n = jnp.abs(rel)
    large = max_exact + (jnp.log(jnp.maximum(n, max_exact).astype(jnp.float32) / max_exact)
                         / math.log(MAX_DISTANCE / max_exact) * (nb - max_exact)).astype(jnp.int32)
    large = jnp.minimum(large, nb - 1)
    return ret + jnp.where(n < max_exact, n, large)


def window_attention(q, k, v, sink, rel_bias_table):
    B, T = q.shape[:2]
    nb = T // BLOCK
    G = ATT_Q_HEADS // ATT_KV_HEADS
    f32 = jnp.float32
    qb = q.astype(f32).reshape(B, nb, BLOCK, ATT_KV_HEADS, G, ATT_HEAD_DIM)

    def band(t):
        tp = jnp.pad(t.astype(f32), ((0, 0), (BLOCK, BLOCK), (0, 0), (0, 0)))
        return jnp.concatenate([tp[:, o:o + T].reshape(B, nb, BLOCK, ATT_KV_HEADS, ATT_HEAD_DIM)
                                for o in (0, BLOCK, 2 * BLOCK)], axis=2)

    kb, vb = band(k), band(v)
    s = jnp.einsum('bnikgd,bnjkd->bnkgij', qb, kb) * (ATT_HEAD_DIM ** -0.5)
    i = jnp.arange(BLOCK)[:, None]
    j = jnp.arange(3 * BLOCK)[None, :]
    rel = j - BLOCK - i
    bias = rel_bias_table[t5_bucket(rel)].astype(f32)
    bias = jnp.transpose(bias, (2, 0, 1)).reshape(ATT_KV_HEADS, G, BLOCK, 3 * BLOCK)
    key_pos = jnp.arange(nb)[:, None, None] * BLOCK - BLOCK + j[None]
    mask = (jnp.abs(rel)[None] <= WINDOW) & (key_pos >= 0) & (key_pos < T)
    s = jnp.where(mask[None, :, None, None], s + bias, NEG_INF)
    sk = sink.astype(f32).reshape(ATT_KV_HEADS, G, 1)
    m = jnp.maximum(jnp.max(s, axis=-1), sk)
    p = jnp.exp(s - m[..., None])
    denom = jnp.sum(p, axis=-1) + jnp.exp(sk - m)
    o = jnp.einsum('bnkgij,bnjkd->bnikgd', p, vb)
    o = o / jnp.transpose(denom, (0, 1, 4, 2, 3))[..., None]
    return o.reshape(B, T, ATT_WIDTH).astype(q.dtype)


def memory_cross_attention(h, mem_n, wq, wk, wv, wo):
    B, T, _ = h.shape
    M = mem_n.shape[1]
    q = (h @ wq).reshape(B, T, XATT_HEADS, XATT_HEAD_DIM).astype(jnp.float32)
    k = (mem_n @ wk).reshape(B, M, XATT_HEADS, XATT_HEAD_DIM).astype(jnp.float32)
    v = (mem_n @ wv).reshape(B, M, XATT_HEADS, XATT_HEAD_DIM)
    s = jnp.einsum('bthd,bmhd->bhtm', q, k) * (XATT_HEAD_DIM ** -0.5)
    p = jax.nn.softmax(s, axis=-1).astype(v.dtype)
    o = jnp.einsum('bhtm,bmhd->bthd', p, v).reshape(B, T, XATT_WIDTH)
    return o @ wo


def hierarchical_moe(h, w_coarse, b_coarse, w_fine, b_fine, w_gate, w_up, w_down):
    B, T, D = h.shape
    xf = h.reshape(B * T, D)
    cl = (xf @ w_coarse).astype(jnp.float32) + b_coarse
    cp = jax.nn.softmax(cl, axis=-1)
    p_g, g_idx = lax.top_k(cp, 1)
    g_oh = jax.nn.one_hot(g_idx[:, 0], N_GROUPS, dtype=jnp.float32)
    fl = (xf @ w_fine).astype(jnp.float32).reshape(-1, N_GROUPS, EXPERTS_PER_GROUP) + b_fine
    fl_sel = jnp.sum(fl * g_oh[:, :, None], axis=1)
    top_v, top_i = lax.top_k(fl_sel, TOP_K)
    top_w = jax.nn.softmax(top_v, axis=-1) * p_g
    fine_gate = jnp.sum(jax.nn.one_hot(top_i, EXPERTS_PER_GROUP, dtype=jnp.float32) * top_w[..., None], axis=1)
    gate = g_oh[:, :, None] * fine_gate[:, None, :]
    y = jnp.zeros_like(xf)
    for grp in range(N_GROUPS):
        hg = jax.nn.silu(jnp.einsum('nd,edf->nef', xf, w_gate[grp])) * jnp.einsum('nd,edf->nef', xf, w_up[grp])
        hg = hg * gate[:, grp, :, None].astype(hg.dtype)
        y = y + jnp.einsum('nef,efd->nd', hg, w_down[grp])
    return y.reshape(B, T, D)


def setup_inputs(seed: int = 0) -> dict:
    key = jax.random.key(seed)
    ks = iter(jax.random.split(key, 48))
    f32 = jnp.float32

    def nrm(shape, scale):
        return scale * jax.random.normal(next(ks), shape, f32)

    def unif(shape, lo, hi):
        return jax.random.uniform(next(ks), shape, f32, lo, hi)

    C, L = RWKV_WIDTH, DEPTH
    return {
        "x": nrm((BATCH, SEQ, D_MODEL), 1.0),
        "mem": nrm((BATCH, MEM_LEN, D_MODEL), 1.0),
        "w_in": nrm((L, D_MODEL, IN_COLS), D_MODEL ** -0.5),
        "shift_prev": unif((L, RWKV_COLS), 0.0, 0.5),
        "shift_next": unif((L, RWKV_COLS), 0.0, 0.5),
        "decay_w0": unif((L, 2, C), -4.0, 1.0),
        "decay_w2": nrm((L, 2, DECAY_RANK, C), 0.5 * DECAY_RANK ** -0.5),
        "iclr_a0": nrm((L, 2, C), 0.5),
        "iclr_a2": nrm((L, 2, ICLR_RANK, C), 0.5 * ICLR_RANK ** -0.5),
        "gate_w2": nrm((L, GATE_RANK, C), GATE_RANK ** -0.5),
        "vres_v0": nrm((L - 1, C), 0.5),
        "vres_w1": nrm((L - 1, C, VRES_RANK), C ** -0.5),
        "vres_w2": nrm((L - 1, VRES_RANK, C), 0.5 * VRES_RANK ** -0.5),
        "k_k": 0.85 + nrm((L, C), 0.1),
        "k_a": 1.0 + nrm((L, C), 0.1),
        "r_k": nrm((L, C), 0.1),
        "ln_x_gain": 1.0 + nrm((L, C), 0.1),
        "ln_x_bias": nrm((L, C), 0.02),
        "att_sink": nrm((L, ATT_Q_HEADS), 1.0),
        "att_out_gain": 1.0 + nrm((L, ATT_WIDTH), 0.1),
        "rel_bias_table": nrm((N_BUCKETS, ATT_Q_HEADS), 0.5),
        "w_out": nrm((L, MIX_WIDTH, D_MODEL), MIX_WIDTH ** -0.5),
        "norm_mix": 1.0 + nrm((L, D_MODEL), 0.05),
        "norm_xatt": 1.0 + nrm((L, D_MODEL), 0.05),
        "mem_norm": 1.0 + nrm((L, D_MODEL), 0.05),
        "xatt_wq": nrm((L, D_MODEL, XATT_WIDTH), D_MODEL ** -0.5),
        "xatt_wk": nrm((L, D_MODEL, XATT_WIDTH), D_MODEL ** -0.5),
        "xatt_wv": nrm((L, D_MODEL, XATT_WIDTH), D_MODEL ** -0.5),
        "xatt_wo": nrm((L, XATT_WIDTH, D_MODEL), XATT_WIDTH ** -0.5),
        "norm_moe": 1.0 + nrm((L, D_MODEL), 0.05),
        "router_coarse_w": nrm((L, D_MODEL, N_GROUPS), D_MODEL ** -0.5),
        "router_coarse_b": nrm((L, N_GROUPS), 0.01),
        "router_fine_w": nrm((L, D_MODEL, N_GROUPS * EXPERTS_PER_GROUP), D_MODEL ** -0.5),
        "router_fine_b": nrm((L, N_GROUPS, EXPERTS_PER_GROUP), 0.01),
        "expert_w_gate": nrm((L, N_GROUPS, EXPERTS_PER_GROUP, D_MODEL, D_EXPERT), D_MODEL ** -0.5),
        "expert_w_up": nrm((L, N_GROUPS, EXPERTS_PER_GROUP, D_MODEL, D_EXPERT), D_MODEL ** -0.5),
        "expert_w_down": nrm((L, N_GROUPS, EXPERTS_PER_GROUP, D_EXPERT, D_MODEL), D_EXPERT ** -0.5),
        "final_norm": 1.0 + nrm((D_MODEL,), 0.05),
    }


def reference(x, mem, w_in, shift_prev, shift_next, decay_w0, decay_w2, iclr_a0, iclr_a2, gate_w2,
              vres_v0, vres_w1, vres_w2, k_k, k_a, r_k, ln_x_gain, ln_x_bias, att_sink, att_out_gain,
              rel_bias_table, w_out, norm_mix, norm_xatt, mem_norm, xatt_wq, xatt_wk, xatt_wv, xatt_wo,
              norm_moe, router_coarse_w, router_coarse_b, router_fine_w, router_fine_b,
              expert_w_gate, expert_w_up, expert_w_down, final_norm):
    B, T, _ = x.shape
    h = x
    v_first = None
    for l in range(DEPTH):
        hn = rms_norm(h, norm_mix[l])
        proj = hn @ w_in[l]
        u_r, q, k, v = _split(proj, (RWKV_COLS, ATT_WIDTH, ATT_KV_WIDTH, ATT_KV_WIDTH))
        u_r = centred_shift(u_r, shift_prev[l], shift_next[l])
        vres = None if l == 0 else (vres_v0[l - 1], vres_w1[l - 1], vres_w2[l - 1])
        y_r, v_first = rwkv7_bidirectional(u_r, v_first, decay_w0[l], decay_w2[l], iclr_a0[l], iclr_a2[l],
                                           gate_w2[l], vres, k_k[l], k_a[l], r_k[l], ln_x_gain[l], ln_x_bias[l])
        y_a = window_attention(q.reshape(B, T, ATT_Q_HEADS, ATT_HEAD_DIM),
                               k.reshape(B, T, ATT_KV_HEADS, ATT_HEAD_DIM),
                               v.reshape(B, T, ATT_KV_HEADS, ATT_HEAD_DIM),
                               att_sink[l], rel_bias_table)
        y_a = rms_norm(y_a, att_out_gain[l])
        h = h + jnp.concatenate([y_r, y_a], axis=-1) @ w_out[l]
        hn = rms_norm(h, norm_xatt[l])
        mem_n = rms_norm(mem, mem_norm[l])
        h = h + memory_cross_attention(hn, mem_n, xatt_wq[l], xatt_wk[l], xatt_wv[l], xatt_wo[l])
        hn = rms_norm(h, norm_moe[l])
        h = h + hierarchical_moe(hn, router_coarse_w[l], router_coarse_b[l], router_fine_w[l], router_fine_b[l],
                                 expert_w_gate[l], expert_w_up[l], expert_w_down[l])
    return rms_norm(h, final_norm)
```

```python
import functools
import math

import jax
import jax.numpy as jnp
from jax import lax
from jax.experimental import pallas as pl
from jax.experimental.pallas import tpu as pltpu

F32 = jnp.float32
BF16 = jnp.bfloat16
HI = lax.Precision.HIGHEST

D_MODEL = 2048
RWKV_HEADS = 16
HEAD_DIM = 64
C_RWKV = RWKV_HEADS * HEAD_DIM
ATT_Q_HEADS = 16
ATT_KV_HEADS = 4
ATT_WIDTH = ATT_Q_HEADS * HEAD_DIM
ATT_KV_WIDTH = ATT_KV_HEADS * HEAD_DIM
WINDOW = 128
BLOCK = 128
DECAY_RANK = 96
ICLR_RANK = 96
VRES_RANK = 64
GATE_RANK = 256
GN_EPS = 64e-5
N_BUCKETS = 32
MAX_DISTANCE = 128
XATT_HEADS = 4
XATT_HEAD_DIM = 128
XATT_WIDTH = XATT_HEADS * XATT_HEAD_DIM
N_GROUPS = 4
EXPERTS_PER_GROUP = 8
N_EXPERTS = N_GROUPS * EXPERTS_PER_GROUP
D_EXPERT = 512
EPS = 1e-6
NEG_INF = -1e30

LANES = 128
RANK_PAD = 128
R0, K0, V0 = 0, C_RWKV, 2 * C_RWKV
WD0 = 3 * C_RWKV
WD1 = WD0 + RANK_PAD
AD0 = WD1 + RANK_PAD
AD1 = AD0 + RANK_PAD
GD0 = AD1 + RANK_PAD
RW_COLS = GD0 + GATE_RANK
AT_COLS = ATT_WIDTH + 2 * ATT_KV_WIDTH

CHUNK = 64
MOE_TILE = 256
VMEM_LIMIT = 56 * 1024 * 1024


def _cparams(sem, vmem=VMEM_LIMIT):
    return pltpu.CompilerParams(dimension_semantics=sem, vmem_limit_bytes=vmem)


def _sigmoid(x):
    return 1.0 / (1.0 + jnp.exp(-x))


def _dot(a, b, prec=None):
    return jnp.dot(a, b, preferred_element_type=F32, precision=prec)


def _dot_nt(a, b, prec=None):
    return lax.dot_general(a, b, (((1,), (1,)), ((), ())), preferred_element_type=F32, precision=prec)


def _dot_tn(a, b, prec=None):
    return lax.dot_general(a, b, (((0,), (0,)), ((), ())), preferred_element_type=F32, precision=prec)


def _rms(x, g):
    ms = jnp.mean(x * x, axis=-1, keepdims=True)
    return x * lax.rsqrt(ms + EPS) * g


def _norm_mm_kernel(x_ref, g_ref, w_ref, o_ref, xn_ref):
    @pl.when(pl.program_id(1) == 0)
    def _():
        xn_ref[...] = _rms(x_ref[...], g_ref[...]).astype(BF16)

    o_ref[...] = _dot(xn_ref[...], w_ref[...]).astype(o_ref.dtype)


def norm_matmul(x, g, w, tm, tn, out_dtype=F32):
    n, d = x.shape
    nc = w.shape[1]
    return pl.pallas_call(
        _norm_mm_kernel,
        grid=(n // tm, nc // tn),
        in_specs=[
            pl.BlockSpec((tm, d), lambda i, j: (i, 0)),
            pl.BlockSpec((1, d), lambda i, j: (0, 0)),
            pl.BlockSpec((d, tn), lambda i, j: (0, j)),
        ],
        out_specs=pl.BlockSpec((tm, tn), lambda i, j: (i, j)),
        out_shape=jax.ShapeDtypeStruct((n, nc), out_dtype),
        scratch_shapes=[pltpu.VMEM((tm, d), BF16)],
        compiler_params=_cparams(("parallel", "arbitrary")),
        name="norm_matmul",
    )(x, g.reshape(1, d), w)


def _prep_kernel(seq_tiles, has_vres, *refs):
    if has_vres:
        (u_ref, hp_ref, hn_ref, mup_ref, mun_ref, w0_ref, w2_ref, a0_ref, a2_ref, gw_ref, kk_ref, ka_ref,
         vf_ref, v0_ref, v1_ref, v2_ref,
         r_o, v_o, kk_o, a0_o, a1_o, lw0_o, lw1_o, kd0_o, kd1_o, g_o) = refs
    else:
        (u_ref, hp_ref, hn_ref, mup_ref, mun_ref, w0_ref, w2_ref, a0_ref, a2_ref, gw_ref, kk_ref, ka_ref,
         r_o, v_o, kk_o, a0_o, a1_o, lw0_o, lw1_o, kd0_o, kd1_o, g_o) = refs
    tm = u_ref.shape[0]
    i = pl.program_id(0)
    it = i % seq_tiles
    has_prev = jnp.where(it != 0, 1.0, 0.0)
    has_next = jnp.where(it != seq_tiles - 1, 1.0, 0.0)
    row = lax.broadcasted_iota(jnp.int32, (tm, 1), 0)

    def shifted(c0, c1):
        u = u_ref[:, c0:c1]
        pr = hp_ref[7:8, c0:c1] * has_prev
        nx = hn_ref[0:1, c0:c1] * has_next
        prev = jnp.where(row == 0, pr, pltpu.roll(u, 1, 0))
        nxt = jnp.where(row == tm - 1, nx, pltpu.roll(u, tm - 1, 0))
        return u + mup_ref[:, c0:c1] * (prev - u) + mun_ref[:, c0:c1] * (nxt - u)

    r = shifted(R0, R0 + C_RWKV)
    k = shifted(K0, K0 + C_RWKV)
    v = shifted(V0, V0 + C_RWKV)
    r_o[...] = r
    if has_vres:
        low = _dot(v.astype(BF16), v1_ref[...])
        mix = _sigmoid(v0_ref[...] + _dot(low.astype(BF16), v2_ref[...]))
        v = v + (vf_ref[...] - v) * mix
    v_o[...] = v
    kk_o[...] = k * kk_ref[...]
    ka = ka_ref[...]
    lw_scale = -math.exp(-0.5)
    for z, (wc, ac, lw_o, a_o, kd_o) in enumerate(
            ((WD0, AD0, lw0_o, a0_o, kd0_o), (WD1, AD1, lw1_o, a1_o, kd1_o))):
        wd = shifted(wc, wc + RANK_PAD)
        w_pre = w0_ref[z:z + 1, :] + _dot(jnp.tanh(wd).astype(BF16), w2_ref[z])
        lw_o[...] = lw_scale * _sigmoid(w_pre)
        ad = shifted(ac, ac + RANK_PAD)
        a = _sigmoid(a0_ref[z:z + 1, :] + _dot(ad.astype(BF16), a2_ref[z]))
        a_o[...] = a
        kd_o[...] = k * (1.0 + (a - 1.0) * ka)
    gd = shifted(GD0, GD0 + GATE_RANK)
    g_o[...] = _dot(_sigmoid(gd).astype(BF16), gw_ref[...])


def rwkv_prep(u, seq_len, mup, mun, w0, w2p, a0, a2p, gw, k_k, k_a, vres, v_first, tm=256):
    n = u.shape[0]
    seq_tiles = seq_len // tm
    hb = tm // 8
    nblk8 = n // 8
    has_vres = vres is not None
    c = C_RWKV

    def full(shape):
        nd = len(shape)
        return pl.BlockSpec(shape, lambda i: (0,) * nd)

    in_specs = [
        pl.BlockSpec((tm, RW_COLS), lambda i: (i, 0)),
        pl.BlockSpec((8, RW_COLS), lambda i: (jnp.maximum(i * hb - 1, 0), 0)),
        pl.BlockSpec((8, RW_COLS), lambda i: (jnp.minimum((i + 1) * hb, nblk8 - 1), 0)),
        full((1, RW_COLS)), full((1, RW_COLS)),
        full((2, c)), full((2, RANK_PAD, c)), full((2, c)), full((2, RANK_PAD, c)),
        full((GATE_RANK, c)), full((1, c)), full((1, c)),
    ]
    args = [u, u, u, mup, mun, w0, w2p, a0, a2p, gw, k_k.reshape(1, c), k_a.reshape(1, c)]
    if has_vres:
        v0, v1p, v2p = vres
        in_specs += [pl.BlockSpec((tm, c), lambda i: (i, 0)), full((1, c)), full((c, RANK_PAD)), full((RANK_PAD, c))]
        args += [v_first, v0.reshape(1, c), v1p, v2p]
    out_spec = pl.BlockSpec((tm, c), lambda i: (i, 0))
    out_sds = jax.ShapeDtypeStruct((n, c), F32)
    return pl.pallas_call(
        functools.partial(_prep_kernel, seq_tiles, has_vres),
        grid=(n // tm,),
        in_specs=in_specs,
        out_specs=[out_spec] * 10,
        out_shape=[out_sds] * 10,
        compiler_params=_cparams(("parallel",)),
        name="rwkv_prep",
    )(*args)


def _chunk_one_dir(rev, r, v, kkraw, a, lw, kd, s_ref, consts):
    L = r.shape[0]
    m1, bd2, tri, strict, incl = consts
    ss = _dot(kkraw * kkraw, bd2, HI)
    kk = kkraw / jnp.maximum(jnp.sqrt(ss), 1e-12)
    an = -kk
    bn = kk * a
    c = _dot(tri, lw, HI)
    c_last = c[0:1, :] if rev else c[L - 1:L, :]
    e_pos = jnp.exp(c)
    e_neg = jnp.exp(-c)
    e_rem = jnp.exp(c_last - c)
    a_t = an * jnp.exp(c - lw)
    b_t = bn * e_neg
    k_t = kd * e_neg
    r_t = r * e_pos
    b_h = bn * e_rem
    k_h = kd * e_rem

    def stack(x):
        x = x.astype(BF16)
        z = jnp.zeros_like(x)
        return jnp.concatenate([jnp.where(m1, x, z), jnp.where(m1, z, x)], axis=0)

    a_s, b_s, k_s, r_s, v_s, bh_s, kh_s = (stack(x) for x in (a_t, b_t, k_t, r_t, v, b_h, k_h))
    s0 = s_ref[...]
    s0b = s0.astype(BF16)
    m_ab = jnp.where(strict, _dot_nt(a_s, b_s), 0.0)
    m_ak = jnp.where(strict, _dot_nt(a_s, k_s), 0.0)
    m_rb = jnp.where(incl, _dot_nt(r_s, b_s), 0.0)
    m_rk = jnp.where(incl, _dot_nt(r_s, k_s), 0.0)
    x = _dot_nt(a_s, s0b) + _dot(m_ak.astype(BF16), v_s)
    p = m_ab
    n_steps = int(math.log2(L))
    for step in range(n_steps):
        pb = p.astype(BF16)
        x = x + _dot(pb, x.astype(BF16))
        if step < n_steps - 1:
            p = _dot(pb, pb)
    ub = x.astype(BF16)
    y2 = _dot_nt(r_s, s0b) + _dot(m_rb.astype(BF16), ub) + _dot(m_rk.astype(BF16), v_s)
    s_ref[...] = s0 * jnp.exp(c_last) + _dot_tn(ub, bh_s) + _dot_tn(v_s, kh_s)
    return y2[0:L, :] + y2[L:2 * L, :]


def _chunk_kernel(rf, vf, kkf, af, lwf, kdf, rb, vb, kkb, ab, lwb, kdb, yf_o, yb_o, s_ref):
    L = rf.shape[0]

    @pl.when(pl.program_id(2) == 0)
    def _():
        s_ref[...] = jnp.zeros_like(s_ref)

    lane = lax.broadcasted_iota(jnp.int32, (1, LANES), 1)
    m1 = lane < HEAD_DIM
    ri = lax.broadcasted_iota(jnp.int32, (LANES, LANES), 0)
    ci = lax.broadcasted_iota(jnp.int32, (LANES, LANES), 1)
    bd2 = ((ri >> 6) == (ci >> 6)).astype(F32)
    t_i = lax.broadcasted_iota(jnp.int32, (L, L), 0)
    s_i = lax.broadcasted_iota(jnp.int32, (L, L), 1)
    r2 = lax.broadcasted_iota(jnp.int32, (2 * L, 2 * L), 0)
    c2 = lax.broadcasted_iota(jnp.int32, (2 * L, 2 * L), 1)
    same = (r2 >= L) == (c2 >= L)
    tt = r2 & (L - 1)
    st = c2 & (L - 1)
    for rev, ins, y_o in ((False, (rf, vf, kkf, af, lwf, kdf), yf_o), (True, (rb, vb, kkb, ab, lwb, kdb), yb_o)):
        if rev:
            tri = (s_i >= t_i).astype(F32)
            strict = same & (st > tt)
            incl = same & (st >= tt)
        else:
            tri = (s_i <= t_i).astype(F32)
            strict = same & (st < tt)
            incl = same & (st <= tt)
        vals = [x[...] for x in ins]
        y_o[...] = _chunk_one_dir(rev, *vals, s_ref.at[1 if rev else 0], (m1, bd2, tri, strict, incl))


def rwkv_chunk(r, v, kk, a0, a1, lw0, lw1, kd0, kd1, batch, seq_len):
    c = C_RWKV
    L = CHUNK
    nc = seq_len // L
    hp = c // LANES

    def r3(x):
        return x.reshape(batch, seq_len, c)

    fwd = pl.BlockSpec((None, L, LANES), lambda b, h, t: (b, t, h))
    bwd = pl.BlockSpec((None, L, LANES), lambda b, h, t: (b, nc - 1 - t, h))
    sds = jax.ShapeDtypeStruct((batch, seq_len, c), F32)
    yf, yb = pl.pallas_call(
        _chunk_kernel,
        grid=(batch, hp, nc),
        in_specs=[fwd] * 6 + [bwd] * 6,
        out_specs=[fwd, bwd],
        out_shape=[sds, sds],
        scratch_shapes=[pltpu.VMEM((2, LANES, LANES), F32)],
        compiler_params=_cparams(("parallel", "parallel", "arbitrary")),
        name="rwkv_chunk",
    )(r3(r), r3(v), r3(kk), r3(a0), r3(lw0), r3(kd0), r3(r), r3(v), r3(kk), r3(a1), r3(lw1), r3(kd1))
    return yf.reshape(-1, c), yb.reshape(-1, c)


def _post_kernel(yf, yb, r, v, kd0, kd1, g, rk, lng, lnb, o_ref):
    ri = lax.broadcasted_iota(jnp.int32, (LANES, LANES), 0)
    ci = lax.broadcasted_iota(jnp.int32, (LANES, LANES), 1)
    bd2 = ((ri >> 6) == (ci >> 6)).astype(F32)
    y = yf[...] + yb[...]
    mu = _dot(y, bd2, HI) * (1.0 / HEAD_DIM)
    d = y - mu
    var = _dot(d * d, bd2, HI) * (1.0 / HEAD_DIM)
    yn = d * lax.rsqrt(var + GN_EPS) * lng[...] + lnb[...]
    bonus = _dot(r[...] * (kd0[...] + kd1[...]) * rk[...], bd2, HI) * v[...]
    o_ref[...] = ((yn + bonus) * g[...]).astype(o_ref.dtype)


def rwkv_post(yf, yb, r, v, kd0, kd1, g, r_k, ln_g, ln_b, tm=512):
    n, c = yf.shape
    blk = pl.BlockSpec((tm, LANES), lambda i, h: (i, h))
    vec = pl.BlockSpec((1, LANES), lambda i, h: (0, h))
    return pl.pallas_call(
        _post_kernel,
        grid=(n // tm, c // LANES),
        in_specs=[blk] * 7 + [vec] * 3,
        out_specs=blk,
        out_shape=jax.ShapeDtypeStruct((n, c), BF16),
        compiler_params=_cparams(("parallel", "parallel")),
        name="rwkv_post",
    )(yf, yb, r, v, kd0, kd1, g, r_k.reshape(1, c), ln_g.reshape(1, c), ln_b.reshape(1, c))


def _attn_kernel(q_ref, kp, kc, kn, vp, vc, vn, bias_ref, sink_ref, gain_ref, o_ref):
    n = pl.program_id(1)
    nb = pl.num_programs(1)
    i = lax.broadcasted_iota(jnp.int32, (BLOCK, 3 * BLOCK), 0)
    j = lax.broadcasted_iota(jnp.int32, (BLOCK, 3 * BLOCK), 1)
    rel = j - BLOCK - i
    valid = (jnp.abs(rel) <= WINDOW) & ((j >= BLOCK) | (n > 0)) & ((j < 2 * BLOCK) | (n < nb - 1))
    kb = jnp.concatenate([kp[...], kc[...], kn[...]], axis=0).astype(BF16)
    vb = jnp.concatenate([vp[...], vc[...], vn[...]], axis=0).astype(BF16)
    q = q_ref[...].astype(BF16)
    scale = HEAD_DIM ** -0.5
    group = ATT_Q_HEADS // ATT_KV_HEADS
    outs = []
    for kh in range(ATT_KV_HEADS):
        k_h = kb[:, kh * HEAD_DIM:(kh + 1) * HEAD_DIM]
        v_h = vb[:, kh * HEAD_DIM:(kh + 1) * HEAD_DIM]
        for gi in range(group):
            h = kh * group + gi
            q_h = q[:, h * HEAD_DIM:(h + 1) * HEAD_DIM]
            s = _dot_nt(q_h, k_h) * scale
            s = jnp.where(valid, s + bias_ref[h], NEG_INF)
            sk = sink_ref[h]
            m = jnp.maximum(jnp.max(s, axis=-1, keepdims=True), sk)
            p = jnp.exp(s - m)
            denom = jnp.sum(p, axis=-1, keepdims=True) + jnp.exp(sk - m)
            outs.append(_dot(p.astype(BF16), v_h) / denom)
    o = jnp.concatenate(outs, axis=-1)
    o_ref[...] = _rms(o, gain_ref[...]).astype(o_ref.dtype)


def window_attention(qkv, bias, sink, gain, batch, seq_len):
    n = qkv.shape[0]
    nb = seq_len // BLOCK
    kcol = ATT_WIDTH // ATT_KV_WIDTH
    vcol = kcol + 1

    def rows(off):
        def f(b, t):
            return jnp.clip(t + off, 0, nb - 1) + b * nb
        return f

    def kv_spec(col, off):
        f = rows(off)
        return pl.BlockSpec((BLOCK, ATT_KV_WIDTH), lambda b, t: (f(b, t), col))

    return pl.pallas_call(
        _attn_kernel,
        grid=(batch, nb),
        in_specs=[
            pl.BlockSpec((BLOCK, ATT_WIDTH), lambda b, t: (b * nb + t, 0)),
            kv_spec(kcol, -1), kv_spec(kcol, 0), kv_spec(kcol, 1),
            kv_spec(vcol, -1), kv_spec(vcol, 0), kv_spec(vcol, 1),
            pl.BlockSpec((ATT_Q_HEADS, BLOCK, 3 * BLOCK), lambda b, t: (0, 0, 0)),
            pl.BlockSpec(memory_space=pltpu.SMEM),
            pl.BlockSpec((1, ATT_WIDTH), lambda b, t: (0, 0)),
        ],
        out_specs=pl.BlockSpec((BLOCK, ATT_WIDTH), lambda b, t: (b * nb + t, 0)),
        out_shape=jax.ShapeDtypeStruct((n, ATT_WIDTH), BF16),
        compiler_params=_cparams(("parallel", "parallel")),
        name="window_attention",
    )(qkv, qkv, qkv, qkv, qkv, qkv, qkv, bias, sink, gain.reshape(1, ATT_WIDTH))


def _t5_bias(rel_bias_table):
    i = jnp.arange(BLOCK)[:, None]
    j = jnp.arange(3 * BLOCK)[None, :]
    rel = j - BLOCK - i
    nbk = N_BUCKETS // 2
    max_exact = nbk // 2
    ret = jnp.where(rel > 0, nbk, 0)
    nabs = jnp.abs(rel)
    large = max_exact + (jnp.log(jnp.maximum(nabs, max_exact).astype(jnp.float32) / max_exact)
                         / math.log(MAX_DISTANCE / max_exact) * (nbk - max_exact)).astype(jnp.int32)
    large = jnp.minimum(large, nbk - 1)
    bucket = ret + jnp.where(nabs < max_exact, nabs, large)
    return jnp.transpose(rel_bias_table[bucket].astype(F32), (2, 0, 1))


def _outproj_kernel(yr, ya, w1, w2, h_ref, o_ref):
    o_ref[...] = h_ref[...] + _dot(yr[...], w1[...]) + _dot(ya[...], w2[...])


def out_proj(y_r, y_a, w1, w2, h, tm=512, tn=1024):
    n, d = h.shape
    c = y_r.shape[1]
    return pl.pallas_call(
        _outproj_kernel,
        grid=(n // tm, d // tn),
        in_specs=[
            pl.BlockSpec((tm, c), lambda i, j: (i, 0)),
            pl.BlockSpec((tm, c), lambda i, j: (i, 0)),
            pl.BlockSpec((c, tn), lambda i, j: (0, j)),
            pl.BlockSpec((c, tn), lambda i, j: (0, j)),
            pl.BlockSpec((tm, tn), lambda i, j: (i, j)),
        ],
        out_specs=pl.BlockSpec((tm, tn), lambda i, j: (i, j)),
        out_shape=jax.ShapeDtypeStruct((n, d), F32),
        compiler_params=_cparams(("parallel", "parallel")),
        name="out_proj",
    )(y_r, y_a, w1, w2, h)


def _xatt_kernel(h_ref, g_ref, wq, kv_ref, wo, o_ref):
    h = h_ref[...]
    hn = _rms(h, g_ref[...]).astype(BF16)
    q = _dot(hn, wq[...])
    kv = kv_ref[...]
    scale = XATT_HEAD_DIM ** -0.5
    outs = []
    for hd in range(XATT_HEADS):
        sl = slice(hd * XATT_HEAD_DIM, (hd + 1) * XATT_HEAD_DIM)
        k_h = kv[:, sl].astype(BF16)
        v_h = kv[:, XATT_WIDTH + hd * XATT_HEAD_DIM: XATT_WIDTH + (hd + 1) * XATT_HEAD_DIM].astype(BF16)
        s = _dot_nt(q[:, sl].astype(BF16), k_h) * scale
        m = jnp.max(s, axis=-1, keepdims=True)
        p = jnp.exp(s - m)
        p = p / jnp.sum(p, axis=-1, keepdims=True)
        outs.append(_dot(p.astype(BF16), v_h))
    o = jnp.concatenate(outs, axis=-1).astype(BF16)
    o_ref[...] = h + _dot(o, wo[...])


def cross_attention(h, g, wq, kv, wo, batch, seq_len, tm=256):
    n, d = h.shape
    m = kv.shape[0] // batch
    tiles = seq_len // tm
    return pl.pallas_call(
        _xatt_kernel,
        grid=(n // tm,),
        in_specs=[
            pl.BlockSpec((tm, d), lambda i: (i, 0)),
            pl.BlockSpec((1, d), lambda i: (0, 0)),
            pl.BlockSpec((d, XATT_WIDTH), lambda i: (0, 0)),
            pl.BlockSpec((m, 2 * XATT_WIDTH), lambda i: (i // tiles, 0)),
            pl.BlockSpec((XATT_WIDTH, d), lambda i: (0, 0)),
        ],
        out_specs=pl.BlockSpec((tm, d), lambda i: (i, 0)),
        out_shape=jax.ShapeDtypeStruct((n, d), F32),
        compiler_params=_cparams(("parallel",)),
        name="cross_attention",
    )(h, g.reshape(1, d), wq, kv, wo)


def _router_kernel(h_ref, g_ref, w_ref, b_ref, o_ref):
    hn = _rms(h_ref[...], g_ref[...])
    logits = _dot(hn, w_ref[...], HI) + b_ref[...]
    tm = logits.shape[0]
    lane = lax.broadcasted_iota(jnp.int32, (tm, LANES), 1)
    is_coarse = (lane >= N_EXPERTS) & (lane < N_EXPERTS + N_GROUPS)
    cl = jnp.where(is_coarse, logits, NEG_INF)
    cmax = jnp.max(cl, axis=-1, keepdims=True)
    csum = jnp.sum(jnp.where(is_coarse, jnp.exp(cl - cmax), 0.0), axis=-1, keepdims=True)
    p_g = 1.0 / csum
    lane_f = lane.astype(F32)
    grp_f = (lane >> 3).astype(F32)
    big = float(LANES)
    g_lane = jnp.min(jnp.where(is_coarse & (cl == cmax), lane_f, big), axis=-1, keepdims=True)
    g_idx = g_lane - float(N_EXPERTS)
    in_grp = (lane < N_EXPERTS) & (grp_f == g_idx)
    fl = jnp.where(in_grp, logits, NEG_INF)
    m1 = jnp.max(fl, axis=-1, keepdims=True)
    i1 = jnp.min(jnp.where(in_grp & (fl == m1), lane_f, big), axis=-1, keepdims=True)
    fl2 = jnp.where(lane_f == i1, NEG_INF, fl)
    m2 = jnp.max(fl2, axis=-1, keepdims=True)
    i2 = jnp.min(jnp.where(in_grp & (lane_f != i1) & (fl2 == m2), lane_f, big), axis=-1, keepdims=True)
    e2 = jnp.exp(m2 - m1)
    w1 = p_g / (1.0 + e2)
    w2 = p_g * e2 / (1.0 + e2)
    out = jnp.where(lane == 0, i1,
                    jnp.where(lane == 1, i2,
                              jnp.where(lane == 2, w1, jnp.where(lane == 3, w2, 0.0))))
    o_ref[...] = out


def moe_router(h, g, w_r, b_r, tm=512):
    n, d = h.shape
    return pl.pallas_call(
        _router_kernel,
        grid=(n // tm,),
        in_specs=[
            pl.BlockSpec((tm, d), lambda i: (i, 0)),
            pl.BlockSpec((1, d), lambda i: (0, 0)),
            pl.BlockSpec((d, LANES), lambda i: (0, 0)),
            pl.BlockSpec((1, LANES), lambda i: (0, 0)),
        ],
        out_specs=pl.BlockSpec((tm, LANES), lambda i: (i, 0)),
        out_shape=jax.ShapeDtypeStruct((n, LANES), F32),
        compiler_params=_cparams(("parallel",)),
        name="moe_router",
    )(h, g.reshape(1, d), w_r, b_r)


def _row_copy(src_hbm, dst_vmem, sem, src_row, dst_row):
    return pltpu.make_async_copy(src_hbm.at[pl.ds(src_row, 1), :], dst_vmem.at[pl.ds(dst_row, 1), :], sem)


def _expert_kernel(te_ref, nused_ref, tok_ref, h_hbm, g_ref, wg, wu, wd, o_ref, xbuf, sem):
    t = pl.program_id(0)
    tm = xbuf.shape[0]

    @pl.when(t < nused_ref[0])
    def _():
        base = t * tm

        def start(r, carry):
            _row_copy(h_hbm, xbuf, sem, tok_ref[base + r], r).start()
            return carry

        lax.fori_loop(0, tm, start, 0)

        def wait(r, carry):
            _row_copy(h_hbm, xbuf, sem, 0, r).wait()
            return carry

        lax.fori_loop(0, tm, wait, 0)
        xn = _rms(xbuf[...], g_ref[...]).astype(BF16)
        hg = _dot(xn, wg[...])
        hu = _dot(xn, wu[...])
        act = (hg * _sigmoid(hg) * hu).astype(BF16)
        o_ref[...] = _dot(act, wd[...])

    @pl.when(t >= nused_ref[0])
    def _():
        o_ref[...] = jnp.zeros_like(o_ref)


def moe_experts(h, g, w_gate, w_up, w_down, tile_expert, n_used, row_tok, n_tiles):
    n, d = h.shape
    tm = MOE_TILE
    grid_spec = pltpu.PrefetchScalarGridSpec(
        num_scalar_prefetch=3,
        grid=(n_tiles,),
        in_specs=[
            pl.BlockSpec(memory_space=pl.ANY),
            pl.BlockSpec((1, d), lambda t, te, nu, tok: (0, 0)),
            pl.BlockSpec((None, d, D_EXPERT), lambda t, te, nu, tok: (te[t], 0, 0)),
            pl.BlockSpec((None, d, D_EXPERT), lambda t, te, nu, tok: (te[t], 0, 0)),
            pl.BlockSpec((None, D_EXPERT, d), lambda t, te, nu, tok: (te[t], 0, 0)),
        ],
        out_specs=pl.BlockSpec((tm, d), lambda t, te, nu, tok: (t, 0)),
        scratch_shapes=[pltpu.VMEM((tm, d), F32), pltpu.SemaphoreType.DMA(())],
    )
    return pl.pallas_call(
        _expert_kernel,
        grid_spec=grid_spec,
        out_shape=jax.ShapeDtypeStruct((n_tiles * tm, d), F32),
        compiler_params=_cparams(("arbitrary",)),
        name="moe_experts",
    )(tile_expert, n_used, row_tok, h, g.reshape(1, d), w_gate, w_up, w_down)


def _combine_kernel(final, pos_ref, y_hbm, h_ref, slab_ref, fg_ref, o_ref, buf, sem):
    i = pl.program_id(0)
    tm = h_ref.shape[0]
    base = i * tm

    def start(r, carry):
        _row_copy(y_hbm, buf.at[0], sem, pos_ref[2 * (base + r)], r).start()
        _row_copy(y_hbm, buf.at[1], sem, pos_ref[2 * (base + r) + 1], r).start()
        return carry

    lax.fori_loop(0, tm, start, 0)

    def wait(r, carry):
        _row_copy(y_hbm, buf.at[0], sem, 0, r).wait()
        _row_copy(y_hbm, buf.at[1], sem, 0, r).wait()
        return carry

    lax.fori_loop(0, tm, wait, 0)
    slab = slab_ref[...]
    out = h_ref[...] + slab[:, 2:3] * buf[0] + slab[:, 3:4] * buf[1]
    if final:
        out = _rms(out, fg_ref[...])
    o_ref[...] = out


def moe_combine(y_sorted, pos, h, slab, final_gain, final, tm=128):
    n, d = h.shape
    grid_spec = pltpu.PrefetchScalarGridSpec(
        num_scalar_prefetch=1,
        grid=(n // tm,),
        in_specs=[
            pl.BlockSpec(memory_space=pl.ANY),
            pl.BlockSpec((tm, d), lambda i, p: (i, 0)),
            pl.BlockSpec((tm, LANES), lambda i, p: (i, 0)),
            pl.BlockSpec((1, d), lambda i, p: (0, 0)),
        ],
        out_specs=pl.BlockSpec((tm, d), lambda i, p: (i, 0)),
        scratch_shapes=[pltpu.VMEM((2, tm, d), F32), pltpu.SemaphoreType.DMA(())],
    )
    return pl.pallas_call(
        functools.partial(_combine_kernel, final),
        grid_spec=grid_spec,
        out_shape=jax.ShapeDtypeStruct((n, d), F32),
        compiler_params=_cparams(("arbitrary",)),
        name="moe_combine",
    )(pos, y_sorted, h, slab, final_gain.reshape(1, d))


def _routing_tables(slab, n_tiles):
    n = slab.shape[0]
    tm = MOE_TILE
    e = slab[:, 0:2].astype(jnp.int32).reshape(-1)
    onehot = (e[:, None] == jnp.arange(N_EXPERTS, dtype=jnp.int32)[None, :]).astype(jnp.int32)
    csum = jnp.cumsum(onehot, axis=0)
    counts = csum[-1]
    rank = jnp.take_along_axis(csum, e[:, None], axis=1)[:, 0] - 1
    tiles_per = (counts + tm - 1) // tm
    tile_end = jnp.cumsum(tiles_per)
    tile_start = tile_end - tiles_per
    pos = tile_start[e] * tm + rank
    row_tok = jnp.zeros((n_tiles * tm,), jnp.int32).at[pos].set(jnp.arange(2 * n, dtype=jnp.int32) // 2)
    n_used = tile_end[-1]
    t_idx = jnp.arange(n_tiles, dtype=jnp.int32)
    tile_expert = jnp.searchsorted(tile_end, jnp.minimum(t_idx, n_used - 1), side="right").astype(jnp.int32)
    tile_expert = jnp.minimum(tile_expert, N_EXPERTS - 1)
    return tile_expert, n_used.reshape(1).astype(jnp.int32), row_tok, pos.astype(jnp.int32)


def _pad_rows(w, rows):
    return jnp.pad(w, ((0, rows - w.shape[0]), (0, 0)))


def _pad_cols(w, cols):
    return jnp.pad(w, ((0, 0), (0, cols - w.shape[1])))


def _split_in_weights(w):
    c = C_RWKV
    o = 3 * c
    wd0 = w[..., o:o + DECAY_RANK]
    wd1 = w[..., o + DECAY_RANK:o + 2 * DECAY_RANK]
    o += 2 * DECAY_RANK
    ad0 = w[..., o:o + ICLR_RANK]
    ad1 = w[..., o + ICLR_RANK:o + 2 * ICLR_RANK]
    o += 2 * ICLR_RANK
    gd = w[..., o:o + GATE_RANK]
    o += GATE_RANK
    att = w[..., o:]

    def padl(x):
        pad = [(0, 0)] * (x.ndim - 1) + [(0, RANK_PAD - x.shape[-1])]
        return jnp.pad(x, pad)

    rw = jnp.concatenate([w[..., :3 * c], padl(wd0), padl(wd1), padl(ad0), padl(ad1), gd], axis=-1)
    return rw, att


def kernel(x, mem, w_in, shift_prev, shift_next, decay_w0, decay_w2, iclr_a0, iclr_a2, gate_w2, vres_v0, vres_w1, vres_w2, k_k, k_a, r_k, ln_x_gain, ln_x_bias, att_sink, att_out_gain, rel_bias_table, w_out, norm_mix, norm_xatt, mem_norm, xatt_wq, xatt_wk, xatt_wv, xatt_wo, norm_moe, router_coarse_w, router_coarse_b, router_fine_w, router_fine_b, expert_w_gate, expert_w_up, expert_w_down, final_norm):
    batch, seq_len, d = x.shape
    depth = w_in.shape[0]
    n = batch * seq_len
    mem_len = mem.shape[1]
    h = x.reshape(n, d)
    memf = mem.reshape(batch * mem_len, d)
    bias = _t5_bias(rel_bias_table)
    n_tiles = (2 * n) // MOE_TILE + N_EXPERTS
    v_first = None
    for l in range(depth):
        w_rw, w_at = _split_in_weights(w_in[l])
        mup, _ = _split_in_weights(shift_prev[l][None, :])
        mun, _ = _split_in_weights(shift_next[l][None, :])
        w2p = jnp.pad(decay_w2[l], ((0, 0), (0, RANK_PAD - DECAY_RANK), (0, 0))).astype(BF16)
        a2p = jnp.pad(iclr_a2[l], ((0, 0), (0, RANK_PAD - ICLR_RANK), (0, 0))).astype(BF16)
        if l == 0:
            vres = None
        else:
            vres = (vres_v0[l - 1], _pad_cols(vres_w1[l - 1], RANK_PAD).astype(BF16),
                    _pad_rows(vres_w2[l - 1], RANK_PAD).astype(BF16))
        u = norm_matmul(h, norm_mix[l], w_rw.astype(BF16), 512, 768)
        qkv = norm_matmul(h, norm_mix[l], w_at.astype(BF16), 512, 768)
        r, v, kk, a0, a1, lw0, lw1, kd0, kd1, g = rwkv_prep(
            u, seq_len, mup, mun, decay_w0[l], w2p, iclr_a0[l], a2p, gate_w2[l].astype(BF16),
            k_k[l], k_a[l], vres, v_first)
        if l == 0:
            v_first = v
        yf, yb = rwkv_chunk(r, v, kk, a0, a1, lw0, lw1, kd0, kd1, batch, seq_len)
        y_r = rwkv_post(yf, yb, r, v, kd0, kd1, g, r_k[l], ln_x_gain[l], ln_x_bias[l])
        y_a = window_attention(qkv, bias, att_sink[l], att_out_gain[l], batch, seq_len)
        wo = w_out[l].astype(BF16)
        h = out_proj(y_r, y_a, wo[:C_RWKV], wo[C_RWKV:], h)
        wkv = jnp.concatenate([xatt_wk[l], xatt_wv[l]], axis=1).astype(BF16)
        kv = norm_matmul(memf, mem_norm[l], wkv, memf.shape[0], 2 * XATT_WIDTH)
        h = cross_attention(h, norm_xatt[l], xatt_wq[l].astype(BF16), kv, xatt_wo[l].astype(BF16), batch, seq_len)
        w_r = _pad_cols(jnp.concatenate([router_fine_w[l], router_coarse_w[l]], axis=1), LANES)
        b_r = _pad_cols(jnp.concatenate([router_fine_b[l].reshape(1, -1), router_coarse_b[l].reshape(1, -1)], axis=1), LANES)
        slab = moe_router(h, norm_moe[l], w_r, b_r)
        tile_expert, n_used, row_tok, pos = _routing_tables(slab, n_tiles)
        wg = expert_w_gate[l].reshape(N_EXPERTS, d, D_EXPERT).astype(BF16)
        wu = expert_w_up[l].reshape(N_EXPERTS, d, D_EXPERT).astype(BF16)
        wdn = expert_w_down[l].reshape(N_EXPERTS, D_EXPERT, d).astype(BF16)
        y_sorted = moe_experts(h, norm_moe[l], wg, wu, wdn, tile_expert, n_used, row_tok, n_tiles)
        h = moe_combine(y_sorted, pos, h, slab, final_norm, l == depth - 1)
    return h.reshape(batch, seq_len, d)
```

```python
import functools
import math

import jax
import jax.numpy as jnp
from jax import lax
from jax.experimental import pallas as pl
from jax.experimental.pallas import tpu as pltpu

F32 = jnp.float32
BF16 = jnp.bfloat16
HI = lax.Precision.HIGHEST

D_MODEL = 2048
RWKV_HEADS = 16
HEAD_DIM = 64
C_RWKV = RWKV_HEADS * HEAD_DIM
ATT_Q_HEADS = 16
ATT_KV_HEADS = 4
ATT_WIDTH = ATT_Q_HEADS * HEAD_DIM
ATT_KV_WIDTH = ATT_KV_HEADS * HEAD_DIM
WINDOW = 128
BLOCK = 128
DECAY_RANK = 96
ICLR_RANK = 96
VRES_RANK = 64
GATE_RANK = 256
GN_EPS = 64e-5
N_BUCKETS = 32
MAX_DISTANCE = 128
XATT_HEADS = 4
XATT_HEAD_DIM = 128
XATT_WIDTH = XATT_HEADS * XATT_HEAD_DIM
N_GROUPS = 4
EXPERTS_PER_GROUP = 8
N_EXPERTS = N_GROUPS * EXPERTS_PER_GROUP
D_EXPERT = 512
EPS = 1e-6
NEG_INF = -1e30

LANES = 128
RANK_PAD = 128
R0, K0, V0 = 0, C_RWKV, 2 * C_RWKV
WD0 = 3 * C_RWKV
WD1 = WD0 + RANK_PAD
AD0 = WD1 + RANK_PAD
AD1 = AD0 + RANK_PAD
GD0 = AD1 + RANK_PAD
RW_COLS = GD0 + GATE_RANK
AT_COLS = ATT_WIDTH + 2 * ATT_KV_WIDTH

CHUNK = 64
MOE_TILE = 256
VMEM_LIMIT = 56 * 1024 * 1024


def _cparams(sem, vmem=VMEM_LIMIT):
    return pltpu.CompilerParams(dimension_semantics=sem, vmem_limit_bytes=vmem)


def _sigmoid(x):
    return 1.0 / (1.0 + jnp.exp(-x))


def _dot(a, b, prec=None):
    return jnp.dot(a, b, preferred_element_type=F32, precision=prec)


def _dot_nt(a, b, prec=None):
    return lax.dot_general(a, b, (((1,), (1,)), ((), ())), preferred_element_type=F32, precision=prec)


def _dot_tn(a, b, prec=None):
    return lax.dot_general(a, b, (((0,), (0,)), ((), ())), preferred_element_type=F32, precision=prec)


def _rms(x, g):
    ms = jnp.mean(x * x, axis=-1, keepdims=True)
    return x * lax.rsqrt(ms + EPS) * g


def _norm_mm_kernel(x_ref, g_ref, w_ref, o_ref, xn_ref):
    @pl.when(pl.program_id(1) == 0)
    def _():
        xn_ref[...] = _rms(x_ref[...], g_ref[...]).astype(BF16)

    o_ref[...] = _dot(xn_ref[...], w_ref[...]).astype(o_ref.dtype)


def norm_matmul(x, g, w, tm, tn, out_dtype=F32):
    n, d = x.shape
    nc = w.shape[1]
    return pl.pallas_call(
        _norm_mm_kernel,
        grid=(n // tm, nc // tn),
        in_specs=[
            pl.BlockSpec((tm, d), lambda i, j: (i, 0)),
            pl.BlockSpec((1, d), lambda i, j: (0, 0)),
            pl.BlockSpec((d, tn), lambda i, j: (0, j)),
        ],
        out_specs=pl.BlockSpec((tm, tn), lambda i, j: (i, j)),
        out_shape=jax.ShapeDtypeStruct((n, nc), out_dtype),
        scratch_shapes=[pltpu.VMEM((tm, d), BF16)],
        compiler_params=_cparams(("parallel", "arbitrary")),
        name="norm_matmul",
    )(x, g.reshape(1, d), w)


def _prep_kernel(seq_tiles, has_vres, *refs):
    if has_vres:
        (u_ref, hp_ref, hn_ref, mup_ref, mun_ref, w0_ref, w2_ref, a0_ref, a2_ref, gw_ref, kk_ref, ka_ref,
         vf_ref, v0_ref, v1_ref, v2_ref,
         r_o, v_o, kk_o, a0_o, a1_o, lw0_o, lw1_o, kd0_o, kd1_o, g_o) = refs
    else:
        (u_ref, hp_ref, hn_ref, mup_ref, mun_ref, w0_ref, w2_ref, a0_ref, a2_ref, gw_ref, kk_ref, ka_ref,
         r_o, v_o, kk_o, a0_o, a1_o, lw0_o, lw1_o, kd0_o, kd1_o, g_o) = refs
    tm = u_ref.shape[0]
    i = pl.program_id(0)
    it = i % seq_tiles
    has_prev = jnp.where(it != 0, 1.0, 0.0)
    has_next = jnp.where(it != seq_tiles - 1, 1.0, 0.0)
    row = lax.broadcasted_iota(jnp.int32, (tm, 1), 0)

    def shifted(c0, c1):
        u = u_ref[:, c0:c1]
        pr = hp_ref[7:8, c0:c1] * has_prev
        nx = hn_ref[0:1, c0:c1] * has_next
        prev = jnp.where(row == 0, pr, pltpu.roll(u, 1, 0))
        nxt = jnp.where(row == tm - 1, nx, pltpu.roll(u, tm - 1, 0))
        return u + mup_ref[:, c0:c1] * (prev - u) + mun_ref[:, c0:c1] * (nxt - u)

    r = shifted(R0, R0 + C_RWKV)
    k = shifted(K0, K0 + C_RWKV)
    v = shifted(V0, V0 + C_RWKV)
    r_o[...] = r
    if has_vres:
        low = _dot(v.astype(BF16), v1_ref[...])
        mix = _sigmoid(v0_ref[...] + _dot(low.astype(BF16), v2_ref[...]))
        v = v + (vf_ref[...] - v) * mix
    v_o[...] = v
    kk_o[...] = k * kk_ref[...]
    ka = ka_ref[...]
    lw_scale = -math.exp(-0.5)
    for z, (wc, ac, lw_o, a_o, kd_o) in enumerate(
            ((WD0, AD0, lw0_o, a0_o, kd0_o), (WD1, AD1, lw1_o, a1_o, kd1_o))):
        wd = shifted(wc, wc + RANK_PAD)
        w_pre = w0_ref[z:z + 1, :] + _dot(jnp.tanh(wd).astype(BF16), w2_ref[z])
        lw_o[...] = lw_scale * _sigmoid(w_pre)
        ad = shifted(ac, ac + RANK_PAD)
        a = _sigmoid(a0_ref[z:z + 1, :] + _dot(ad.astype(BF16), a2_ref[z]))
        a_o[...] = a
        kd_o[...] = k * (1.0 + (a - 1.0) * ka)
    gd = shifted(GD0, GD0 + GATE_RANK)
    g_o[...] = _dot(_sigmoid(gd).astype(BF16), gw_ref[...])


def rwkv_prep(u, seq_len, mup, mun, w0, w2p, a0, a2p, gw, k_k, k_a, vres, v_first, tm=256):
    n = u.shape[0]
    seq_tiles = seq_len // tm
    hb = tm // 8
    nblk8 = n // 8
    has_vres = vres is not None
    c = C_RWKV

    def full(shape):
        nd = len(shape)
        return pl.BlockSpec(shape, lambda i: (0,) * nd)

    in_specs = [
        pl.BlockSpec((tm, RW_COLS), lambda i: (i, 0)),
        pl.BlockSpec((8, RW_COLS), lambda i: (jnp.maximum(i * hb - 1, 0), 0)),
        pl.BlockSpec((8, RW_COLS), lambda i: (jnp.minimum((i + 1) * hb, nblk8 - 1), 0)),
        full((1, RW_COLS)), full((1, RW_COLS)),
        full((2, c)), full((2, RANK_PAD, c)), full((2, c)), full((2, RANK_PAD, c)),
        full((GATE_RANK, c)), full((1, c)), full((1, c)),
    ]
    args = [u, u, u, mup, mun, w0, w2p, a0, a2p, gw, k_k.reshape(1, c), k_a.reshape(1, c)]
    if has_vres:
        v0, v1p, v2p = vres
        in_specs += [pl.BlockSpec((tm, c), lambda i: (i, 0)), full((1, c)), full((c, RANK_PAD)), full((RANK_PAD, c))]
        args += [v_first, v0.reshape(1, c), v1p, v2p]
    out_spec = pl.BlockSpec((tm, c), lambda i: (i, 0))
    out_sds = jax.ShapeDtypeStruct((n, c), F32)
    return pl.pallas_call(
        functools.partial(_prep_kernel, seq_tiles, has_vres),
        grid=(n // tm,),
        in_specs=in_specs,
        out_specs=[out_spec] * 10,
        out_shape=[out_sds] * 10,
        compiler_params=_cparams(("parallel",)),
        name="rwkv_prep",
    )(*args)


def _split_bf16(x):
    hi = x.astype(BF16)
    lo = (x - hi.astype(F32)).astype(BF16)
    return hi, lo


def _chunk_kernel(rf, vf, kkf, af, lwf, kdf, rb, vb, kkb, ab, lwb, kdb, yf_o, yb_o, s_ref):
    L, width = rf.shape
    L2 = 2 * L
    npairs = width // LANES

    @pl.when(pl.program_id(1) == 0)
    def _():
        s_ref[...] = jnp.zeros_like(s_ref)

    lane = lax.broadcasted_iota(jnp.int32, (1, LANES), 1)
    m1 = lane < HEAD_DIM
    ri = lax.broadcasted_iota(jnp.int32, (LANES, LANES), 0)
    ci = lax.broadcasted_iota(jnp.int32, (LANES, LANES), 1)
    bd2 = ((ri >> 6) == (ci >> 6)).astype(BF16)
    t_i = lax.broadcasted_iota(jnp.int32, (L, L2), 0)
    s_i = lax.broadcasted_iota(jnp.int32, (L, L2), 1) & (L - 1)
    r2 = lax.broadcasted_iota(jnp.int32, (L2, L2), 0)
    c2 = lax.broadcasted_iota(jnp.int32, (L2, L2), 1)
    same = (r2 >= L) == (c2 >= L)
    tt = r2 & (L - 1)
    st = c2 & (L - 1)

    def stack(x):
        z = jnp.zeros_like(x)
        return jnp.concatenate([jnp.where(m1, x, z), jnp.where(m1, z, x)], axis=0)

    chains = []
    for rev, ins in ((False, (rf, vf, kkf, af, lwf, kdf)), (True, (rb, vb, kkb, ab, lwb, kdb))):
        if rev:
            tri2 = (s_i >= t_i).astype(BF16)
            strict = same & (st > tt)
            incl = same & (st >= tt)
        else:
            tri2 = (s_i <= t_i).astype(BF16)
            strict = same & (st < tt)
            incl = same & (st <= tt)
        r, v, kkraw, a, lw, kd = (x[...] for x in ins)
        c = _dot(tri2, jnp.concatenate(_split_bf16(lw), axis=0))
        c_last = c[0:1, :] if rev else c[L - 1:L, :]
        decay_last = jnp.exp(c_last)
        e_pos = jnp.exp(c)
        e_neg = jnp.exp(-c)
        e_rem = jnp.exp(c_last - c)
        sq_hi, sq_lo = _split_bf16(kkraw * kkraw)
        ss = []
        for pr in range(npairs):
            sl = slice(pr * LANES, (pr + 1) * LANES)
            both = _dot(jnp.concatenate([sq_hi[:, sl], sq_lo[:, sl]], axis=0), bd2)
            ss.append(both[0:L] + both[L:])
        ss = jnp.concatenate(ss, axis=1)
        kk = kkraw / jnp.maximum(jnp.sqrt(ss), 1e-12)
        bn = kk * a
        a_t = (-kk * jnp.exp(c - lw)).astype(BF16)
        b_t = (bn * e_neg).astype(BF16)
        k_t = (kd * e_neg).astype(BF16)
        r_t = (r * e_pos).astype(BF16)
        b_h = (bn * e_rem).astype(BF16)
        k_h = (kd * e_rem).astype(BF16)
        vb16 = v.astype(BF16)
        for pr in range(npairs):
            sl = slice(pr * LANES, (pr + 1) * LANES)
            chains.append(dict(
                strict=strict, incl=incl, s_ref=s_ref.at[1 if rev else 0, pr], decay_last=decay_last[:, sl],
                ar=jnp.concatenate([stack(a_t[:, sl]), stack(r_t[:, sl])], axis=0),
                bk=jnp.concatenate([stack(b_t[:, sl]), stack(k_t[:, sl])], axis=0),
                bkh=jnp.concatenate([stack(b_h[:, sl]), stack(k_h[:, sl])], axis=0),
                v_s=stack(vb16[:, sl])))
    for ch in chains:
        ch["s0"] = ch["s_ref"][...]
        ch["sc"] = _dot_nt(ch["ar"], ch["bk"])
    for ch in chains:
        ch["from_state"] = _dot_nt(ch["ar"], ch["s0"].astype(BF16))
    for ch in chains:
        sc = ch["sc"]
        m_ak = jnp.where(ch["strict"], sc[0:L2, L2:], 0.0).astype(BF16)
        ch["m_r"] = jnp.concatenate([jnp.where(ch["incl"], sc[L2:, 0:L2], 0.0).astype(BF16),
                                     jnp.where(ch["incl"], sc[L2:, L2:], 0.0).astype(BF16)], axis=1)
        ch["p"] = jnp.where(ch["strict"], sc[0:L2, 0:L2], 0.0)
        ch["x"] = ch["from_state"][0:L2] + _dot(m_ak, ch["v_s"])
    n_steps = int(math.log2(L))
    for step in range(n_steps):
        for ch in chains:
            pb = ch["p"].astype(BF16)
            xb = ch["x"].astype(BF16)
            if step < n_steps - 1:
                res = _dot(pb, jnp.concatenate([xb, pb], axis=1))
                ch["x"] = ch["x"] + res[:, 0:LANES]
                ch["p"] = res[:, LANES:]
            else:
                ch["x"] = ch["x"] + _dot(pb, xb)
    for ch in chains:
        ch["uv"] = jnp.concatenate([ch["x"].astype(BF16), ch["v_s"]], axis=0)
        ch["y2"] = ch["from_state"][L2:] + _dot(ch["m_r"], ch["uv"])
    for ch in chains:
        ch["s_ref"][...] = ch["s0"] * ch["decay_last"] + _dot_tn(ch["uv"], ch["bkh"])
    for y_o, group in ((yf_o, chains[:npairs]), (yb_o, chains[npairs:])):
        y_o[...] = jnp.concatenate([ch["y2"][0:L, :] + ch["y2"][L:L2, :] for ch in group], axis=1)


def rwkv_chunk(r, v, kk, a0, a1, lw0, lw1, kd0, kd1, batch, seq_len):
    c = C_RWKV
    L = CHUNK
    nc = seq_len // L

    def r3(x):
        return x.reshape(batch, seq_len, c)

    fwd = pl.BlockSpec((None, L, c), lambda b, t: (b, t, 0))
    bwd = pl.BlockSpec((None, L, c), lambda b, t: (b, nc - 1 - t, 0))
    sds = jax.ShapeDtypeStruct((batch, seq_len, c), F32)
    yf, yb = pl.pallas_call(
        _chunk_kernel,
        grid=(batch, nc),
        in_specs=[fwd] * 6 + [bwd] * 6,
        out_specs=[fwd, bwd],
        out_shape=[sds, sds],
        scratch_shapes=[pltpu.VMEM((2, c // LANES, LANES, LANES), F32)],
        compiler_params=_cparams(("parallel", "arbitrary")),
        name="rwkv_chunk",
    )(r3(r), r3(v), r3(kk), r3(a0), r3(lw0), r3(kd0), r3(r), r3(v), r3(kk), r3(a1), r3(lw1), r3(kd1))
    return yf.reshape(-1, c), yb.reshape(-1, c)


def _post_kernel(yf, yb, r, v, kd0, kd1, g, rk, lng, lnb, o_ref):
    ri = lax.broadcasted_iota(jnp.int32, (LANES, LANES), 0)
    ci = lax.broadcasted_iota(jnp.int32, (LANES, LANES), 1)
    bd2 = ((ri >> 6) == (ci >> 6)).astype(F32)
    y = yf[...] + yb[...]
    mu = _dot(y, bd2, HI) * (1.0 / HEAD_DIM)
    d = y - mu
    var = _dot(d * d, bd2, HI) * (1.0 / HEAD_DIM)
    yn = d * lax.rsqrt(var + GN_EPS) * lng[...] + lnb[...]
    bonus = _dot(r[...] * (kd0[...] + kd1[...]) * rk[...], bd2, HI) * v[...]
    o_ref[...] = ((yn + bonus) * g[...]).astype(o_ref.dtype)


def rwkv_post(yf, yb, r, v, kd0, kd1, g, r_k, ln_g, ln_b, tm=512):
    n, c = yf.shape
    blk = pl.BlockSpec((tm, LANES), lambda i, h: (i, h))
    vec = pl.BlockSpec((1, LANES), lambda i, h: (0, h))
    return pl.pallas_call(
        _post_kernel,
        grid=(n // tm, c // LANES),
        in_specs=[blk] * 7 + [vec] * 3,
        out_specs=blk,
        out_shape=jax.ShapeDtypeStruct((n, c), BF16),
        compiler_params=_cparams(("parallel", "parallel")),
        name="rwkv_post",
    )(yf, yb, r, v, kd0, kd1, g, r_k.reshape(1, c), ln_g.reshape(1, c), ln_b.reshape(1, c))


def _attn_kernel(q_ref, kp, kc, kn, vp, vc, vn, bias_ref, sink_ref, gain_ref, o_ref):
    n = pl.program_id(1)
    nb = pl.num_programs(1)
    i = lax.broadcasted_iota(jnp.int32, (BLOCK, 3 * BLOCK), 0)
    j = lax.broadcasted_iota(jnp.int32, (BLOCK, 3 * BLOCK), 1)
    rel = j - BLOCK - i
    valid = (jnp.abs(rel) <= WINDOW) & ((j >= BLOCK) | (n > 0)) & ((j < 2 * BLOCK) | (n < nb - 1))
    kb = jnp.concatenate([kp[...], kc[...], kn[...]], axis=0).astype(BF16)
    vb = jnp.concatenate([vp[...], vc[...], vn[...]], axis=0).astype(BF16)
    q = (q_ref[...] * (HEAD_DIM ** -0.5)).astype(BF16)
    group = ATT_Q_HEADS // ATT_KV_HEADS
    valid_g = jnp.concatenate([valid] * group, axis=0)
    row_head = lax.broadcasted_iota(jnp.int32, (group * BLOCK, 1), 0) >> 7

    def scores(kh):
        qg = jnp.concatenate([q[:, (kh * group + gi) * HEAD_DIM:(kh * group + gi + 1) * HEAD_DIM]
                              for gi in range(group)], axis=0)
        return _dot_nt(qg, kb[:, kh * HEAD_DIM:(kh + 1) * HEAD_DIM])

    def finish(kh, s):
        bias = bias_ref[kh * group:(kh + 1) * group].reshape(group * BLOCK, 3 * BLOCK)
        s = jnp.where(valid_g, s + bias, NEG_INF)
        sk = jnp.zeros((group * BLOCK, 1), F32)
        for gi in range(group):
            sk = jnp.where(row_head == gi, sink_ref[kh * group + gi], sk)
        m = jnp.maximum(jnp.max(s, axis=-1, keepdims=True), sk)
        p = jnp.exp(s - m)
        denom = jnp.sum(p, axis=-1, keepdims=True) + jnp.exp(sk - m)
        og = _dot(p.astype(BF16), vb[:, kh * HEAD_DIM:(kh + 1) * HEAD_DIM]) / denom
        return [og[gi * BLOCK:(gi + 1) * BLOCK] for gi in range(group)]

    outs = []
    pending = scores(0)
    for kh in range(ATT_KV_HEADS):
        nxt = scores(kh + 1) if kh + 1 < ATT_KV_HEADS else None
        outs += finish(kh, pending)
        pending = nxt
    o = jnp.concatenate(outs, axis=-1)
    o_ref[...] = _rms(o, gain_ref[...]).astype(o_ref.dtype)


def window_attention(qkv, bias, sink, gain, batch, seq_len):
    n = qkv.shape[0]
    nb = seq_len // BLOCK
    kcol = ATT_WIDTH // ATT_KV_WIDTH
    vcol = kcol + 1

    def rows(off):
        def f(b, t):
            return jnp.clip(t + off, 0, nb - 1) + b * nb
        return f

    def kv_spec(col, off):
        f = rows(off)
        return pl.BlockSpec((BLOCK, ATT_KV_WIDTH), lambda b, t: (f(b, t), col))

    return pl.pallas_call(
        _attn_kernel,
        grid=(batch, nb),
        in_specs=[
            pl.BlockSpec((BLOCK, ATT_WIDTH), lambda b, t: (b * nb + t, 0)),
            kv_spec(kcol, -1), kv_spec(kcol, 0), kv_spec(kcol, 1),
            kv_spec(vcol, -1), kv_spec(vcol, 0), kv_spec(vcol, 1),
            pl.BlockSpec((ATT_Q_HEADS, BLOCK, 3 * BLOCK), lambda b, t: (0, 0, 0)),
            pl.BlockSpec(memory_space=pltpu.SMEM),
            pl.BlockSpec((1, ATT_WIDTH), lambda b, t: (0, 0)),
        ],
        out_specs=pl.BlockSpec((BLOCK, ATT_WIDTH), lambda b, t: (b * nb + t, 0)),
        out_shape=jax.ShapeDtypeStruct((n, ATT_WIDTH), BF16),
        compiler_params=_cparams(("parallel", "parallel")),
        name="window_attention",
    )(qkv, qkv, qkv, qkv, qkv, qkv, qkv, bias, sink, gain.reshape(1, ATT_WIDTH))


def _t5_bias(rel_bias_table):
    i = jnp.arange(BLOCK)[:, None]
    j = jnp.arange(3 * BLOCK)[None, :]
    rel = j - BLOCK - i
    nbk = N_BUCKETS // 2
    max_exact = nbk // 2
    ret = jnp.where(rel > 0, nbk, 0)
    nabs = jnp.abs(rel)
    large = max_exact + (jnp.log(jnp.maximum(nabs, max_exact).astype(jnp.float32) / max_exact)
                         / math.log(MAX_DISTANCE / max_exact) * (nbk - max_exact)).astype(jnp.int32)
    large = jnp.minimum(large, nbk - 1)
    bucket = ret + jnp.where(nabs < max_exact, nabs, large)
    onehot = (bucket[None, :, :] == jnp.arange(N_BUCKETS)[:, None, None]).astype(F32)
    return jnp.einsum("bh,bij->hij", rel_bias_table.astype(F32), onehot, precision=HI)


def _outproj_kernel(yr, ya, w1, w2, h_ref, o_ref):
    o_ref[...] = h_ref[...] + _dot(yr[...], w1[...]) + _dot(ya[...], w2[...])


def out_proj(y_r, y_a, w1, w2, h, tm=512, tn=1024):
    n, d = h.shape
    c = y_r.shape[1]
    return pl.pallas_call(
        _outproj_kernel,
        grid=(n // tm, d // tn),
        in_specs=[
            pl.BlockSpec((tm, c), lambda i, j: (i, 0)),
            pl.BlockSpec((tm, c), lambda i, j: (i, 0)),
            pl.BlockSpec((c, tn), lambda i, j: (0, j)),
            pl.BlockSpec((c, tn), lambda i, j: (0, j)),
            pl.BlockSpec((tm, tn), lambda i, j: (i, j)),
        ],
        out_specs=pl.BlockSpec((tm, tn), lambda i, j: (i, j)),
        out_shape=jax.ShapeDtypeStruct((n, d), F32),
        compiler_params=_cparams(("parallel", "parallel")),
        name="out_proj",
    )(y_r, y_a, w1, w2, h)


def _xatt_kernel(h_ref, g_ref, wq, kv_ref, wo, o_ref):
    h = h_ref[...]
    hn = _rms(h, g_ref[...]).astype(BF16)
    q = _dot(hn, wq[...])
    kv = kv_ref[...]
    scale = XATT_HEAD_DIM ** -0.5
    outs = []
    for hd in range(XATT_HEADS):
        sl = slice(hd * XATT_HEAD_DIM, (hd + 1) * XATT_HEAD_DIM)
        k_h = kv[:, sl].astype(BF16)
        v_h = kv[:, XATT_WIDTH + hd * XATT_HEAD_DIM: XATT_WIDTH + (hd + 1) * XATT_HEAD_DIM].astype(BF16)
        s = _dot_nt(q[:, sl].astype(BF16), k_h) * scale
        m = jnp.max(s, axis=-1, keepdims=True)
        p = jnp.exp(s - m)
        p = p / jnp.sum(p, axis=-1, keepdims=True)
        outs.append(_dot(p.astype(BF16), v_h))
    o = jnp.concatenate(outs, axis=-1).astype(BF16)
    o_ref[...] = h + _dot(o, wo[...])


def cross_attention(h, g, wq, kv, wo, batch, seq_len, tm=256):
    n, d = h.shape
    m = kv.shape[0] // batch
    tiles = seq_len // tm
    return pl.pallas_call(
        _xatt_kernel,
        grid=(n // tm,),
        in_specs=[
            pl.BlockSpec((tm, d), lambda i: (i, 0)),
            pl.BlockSpec((1, d), lambda i: (0, 0)),
            pl.BlockSpec((d, XATT_WIDTH), lambda i: (0, 0)),
            pl.BlockSpec((m, 2 * XATT_WIDTH), lambda i: (i // tiles, 0)),
            pl.BlockSpec((XATT_WIDTH, d), lambda i: (0, 0)),
        ],
        out_specs=pl.BlockSpec((tm, d), lambda i: (i, 0)),
        out_shape=jax.ShapeDtypeStruct((n, d), F32),
        compiler_params=_cparams(("parallel",)),
        name="cross_attention",
    )(h, g.reshape(1, d), wq, kv, wo)


def _router_kernel(h_ref, g_ref, w_ref, b_ref, o_ref):
    hn = _rms(h_ref[...], g_ref[...])
    logits = _dot(hn, w_ref[...], HI) + b_ref[...]
    tm = logits.shape[0]
    lane = lax.broadcasted_iota(jnp.int32, (tm, LANES), 1)
    is_coarse = (lane >= N_EXPERTS) & (lane < N_EXPERTS + N_GROUPS)
    cl = jnp.where(is_coarse, logits, NEG_INF)
    cmax = jnp.max(cl, axis=-1, keepdims=True)
    csum = jnp.sum(jnp.where(is_coarse, jnp.exp(cl - cmax), 0.0), axis=-1, keepdims=True)
    p_g = 1.0 / csum
    lane_f = lane.astype(F32)
    grp_f = (lane >> 3).astype(F32)
    big = float(LANES)
    g_lane = jnp.min(jnp.where(is_coarse & (cl == cmax), lane_f, big), axis=-1, keepdims=True)
    g_idx = g_lane - float(N_EXPERTS)
    in_grp = (lane < N_EXPERTS) & (grp_f == g_idx)
    fl = jnp.where(in_grp, logits, NEG_INF)
    m1 = jnp.max(fl, axis=-1, keepdims=True)
    i1 = jnp.min(jnp.where(in_grp & (fl == m1), lane_f, big), axis=-1, keepdims=True)
    fl2 = jnp.where(lane_f == i1, NEG_INF, fl)
    m2 = jnp.max(fl2, axis=-1, keepdims=True)
    i2 = jnp.min(jnp.where(in_grp & (lane_f != i1) & (fl2 == m2), lane_f, big), axis=-1, keepdims=True)
    e2 = jnp.exp(m2 - m1)
    w1 = p_g / (1.0 + e2)
    w2 = p_g * e2 / (1.0 + e2)
    out = jnp.where(lane == 0, i1,
                    jnp.where(lane == 1, i2,
                              jnp.where(lane == 2, w1, jnp.where(lane == 3, w2, 0.0))))
    o_ref[...] = out


def moe_router(h, g, w_r, b_r, tm=512):
    n, d = h.shape
    return pl.pallas_call(
        _router_kernel,
        grid=(n // tm,),
        in_specs=[
            pl.BlockSpec((tm, d), lambda i: (i, 0)),
            pl.BlockSpec((1, d), lambda i: (0, 0)),
            pl.BlockSpec((d, LANES), lambda i: (0, 0)),
            pl.BlockSpec((1, LANES), lambda i: (0, 0)),
        ],
        out_specs=pl.BlockSpec((tm, LANES), lambda i: (i, 0)),
        out_shape=jax.ShapeDtypeStruct((n, LANES), F32),
        compiler_params=_cparams(("parallel",)),
        name="moe_router",
    )(h, g.reshape(1, d), w_r, b_r)


GATHER_UNROLL = 8


def _start_row_gather(src_hbm, dst, sem, idx_ref, first, stride, n_rows):
    def body(r, carry):
        row = idx_ref[first + r * stride]
        pltpu.make_async_copy(src_hbm.at[pl.ds(row, 1), :], dst.at[pl.ds(r, 1), :], sem).start()
        return carry

    lax.fori_loop(0, n_rows, body, 0, unroll=GATHER_UNROLL)


def _wait_row_gather(dst, sem):
    pltpu.make_async_copy(dst, dst, sem).wait()


def _expert_kernel(te_ref, nused_ref, tok_ref, h_hbm, g_ref, wg, wu, wd, o_ref, xbuf, wgb, wub, wdb, sem):
    t = pl.program_id(0)
    tm = xbuf.shape[1]
    n_used = nused_ref[0]
    slot = t % 2

    @pl.when(t == 0)
    def _():
        _start_row_gather(h_hbm, xbuf.at[0], sem.at[0], tok_ref, 0, 1, tm)

    @pl.when(t + 1 < n_used)
    def _():
        _start_row_gather(h_hbm, xbuf.at[1 - slot], sem.at[1 - slot], tok_ref, (t + 1) * tm, 1, tm)

    @pl.when((t == 0) | (te_ref[t] != te_ref[jnp.maximum(t - 1, 0)]))
    def _():
        wgb[...] = wg[...].astype(BF16)
        wub[...] = wu[...].astype(BF16)
        wdb[...] = wd[...].astype(BF16)

    @pl.when(t < n_used)
    def _():
        _wait_row_gather(xbuf.at[slot], sem.at[slot])
        xn = _rms(xbuf[slot], g_ref[...]).astype(BF16)
        hg = _dot(xn, wgb[...])
        hu = _dot(xn, wub[...])
        act = (hg * _sigmoid(hg) * hu).astype(BF16)
        o_ref[...] = _dot(act, wdb[...])

    @pl.when(t >= n_used)
    def _():
        o_ref[...] = jnp.zeros_like(o_ref)


def moe_experts(h, g, w_gate, w_up, w_down, layer, tile_expert, n_used, row_tok, n_tiles):
    n, d = h.shape
    tm = MOE_TILE
    e0 = layer * N_EXPERTS
    grid_spec = pltpu.PrefetchScalarGridSpec(
        num_scalar_prefetch=3,
        grid=(n_tiles,),
        in_specs=[
            pl.BlockSpec(memory_space=pl.ANY),
            pl.BlockSpec((1, d), lambda t, te, nu, tok: (0, 0)),
            pl.BlockSpec((None, d, D_EXPERT), lambda t, te, nu, tok: (e0 + te[t], 0, 0)),
            pl.BlockSpec((None, d, D_EXPERT), lambda t, te, nu, tok: (e0 + te[t], 0, 0)),
            pl.BlockSpec((None, D_EXPERT, d), lambda t, te, nu, tok: (e0 + te[t], 0, 0)),
        ],
        out_specs=pl.BlockSpec((tm, d), lambda t, te, nu, tok: (t, 0)),
        scratch_shapes=[pltpu.VMEM((2, tm, d), F32), pltpu.VMEM((d, D_EXPERT), BF16),
                        pltpu.VMEM((d, D_EXPERT), BF16), pltpu.VMEM((D_EXPERT, d), BF16),
                        pltpu.SemaphoreType.DMA((2,))],
    )
    return pl.pallas_call(
        _expert_kernel,
        grid_spec=grid_spec,
        out_shape=jax.ShapeDtypeStruct((n_tiles * tm, d), F32),
        compiler_params=_cparams(("arbitrary",)),
        name="moe_experts",
    )(tile_expert, n_used, row_tok, h, g.reshape(1, d), w_gate, w_up, w_down)


def _combine_kernel(final, pos_ref, y_hbm, h_ref, slab_ref, fg_ref, o_ref, buf, sem):
    i = pl.program_id(0)
    n_steps = pl.num_programs(0)
    tm = h_ref.shape[0]
    slot = i % 2

    def start(step, sl):
        for j in range(2):
            _start_row_gather(y_hbm, buf.at[sl, j], sem.at[sl, j], pos_ref, 2 * step * tm + j, 2, tm)

    @pl.when(i == 0)
    def _():
        start(0, 0)

    @pl.when(i + 1 < n_steps)
    def _():
        start(i + 1, 1 - slot)

    for j in range(2):
        _wait_row_gather(buf.at[slot, j], sem.at[slot, j])
    slab = slab_ref[...]
    out = h_ref[...] + slab[:, 2:3] * buf[slot, 0] + slab[:, 3:4] * buf[slot, 1]
    if final:
        out = _rms(out, fg_ref[...])
    o_ref[...] = out


def moe_combine(y_sorted, pos, h, slab, final_gain, final, tm=128):
    n, d = h.shape
    grid_spec = pltpu.PrefetchScalarGridSpec(
        num_scalar_prefetch=1,
        grid=(n // tm,),
        in_specs=[
            pl.BlockSpec(memory_space=pl.ANY),
            pl.BlockSpec((tm, d), lambda i, p: (i, 0)),
            pl.BlockSpec((tm, LANES), lambda i, p: (i, 0)),
            pl.BlockSpec((1, d), lambda i, p: (0, 0)),
        ],
        out_specs=pl.BlockSpec((tm, d), lambda i, p: (i, 0)),
        scratch_shapes=[pltpu.VMEM((2, 2, tm, d), F32), pltpu.SemaphoreType.DMA((2, 2))],
    )
    return pl.pallas_call(
        functools.partial(_combine_kernel, final),
        grid_spec=grid_spec,
        out_shape=jax.ShapeDtypeStruct((n, d), F32),
        compiler_params=_cparams(("arbitrary",)),
        name="moe_combine",
    )(pos, y_sorted, h, slab, final_gain.reshape(1, d))


def _routing_tables(slab, n_tiles):
    n = slab.shape[0]
    tm = MOE_TILE
    e = slab[:, 0:2].astype(jnp.int32).reshape(-1)
    onehot = (e[:, None] == jnp.arange(N_EXPERTS, dtype=jnp.int32)[None, :]).astype(jnp.int32)
    csum = jnp.cumsum(onehot, axis=0)
    counts = csum[-1]
    rank = jnp.take_along_axis(csum, e[:, None], axis=1)[:, 0] - 1
    tiles_per = (counts + tm - 1) // tm
    tile_end = jnp.cumsum(tiles_per)
    tile_start = tile_end - tiles_per
    pos = tile_start[e] * tm + rank
    row_tok = jnp.zeros((n_tiles * tm,), jnp.int32).at[pos].set(jnp.arange(2 * n, dtype=jnp.int32) // 2)
    n_used = tile_end[-1]
    t_idx = jnp.arange(n_tiles, dtype=jnp.int32)
    tile_expert = jnp.searchsorted(tile_end, jnp.minimum(t_idx, n_used - 1), side="right").astype(jnp.int32)
    tile_expert = jnp.minimum(tile_expert, N_EXPERTS - 1)
    return tile_expert, n_used.reshape(1).astype(jnp.int32), row_tok, pos.astype(jnp.int32)


def _pad_rows(w, rows):
    return jnp.pad(w, ((0, rows - w.shape[0]), (0, 0)))


def _pad_cols(w, cols):
    return jnp.pad(w, ((0, 0), (0, cols - w.shape[1])))


def _split_in_weights(w):
    c = C_RWKV
    o = 3 * c
    wd0 = w[..., o:o + DECAY_RANK]
    wd1 = w[..., o + DECAY_RANK:o + 2 * DECAY_RANK]
    o += 2 * DECAY_RANK
    ad0 = w[..., o:o + ICLR_RANK]
    ad1 = w[..., o + ICLR_RANK:o + 2 * ICLR_RANK]
    o += 2 * ICLR_RANK
    gd = w[..., o:o + GATE_RANK]
    o += GATE_RANK
    att = w[..., o:]

    def padl(x):
        pad = [(0, 0)] * (x.ndim - 1) + [(0, RANK_PAD - x.shape[-1])]
        return jnp.pad(x, pad)

    rw = jnp.concatenate([w[..., :3 * c], padl(wd0), padl(wd1), padl(ad0), padl(ad1), gd], axis=-1)
    return rw, att


def kernel(x, mem, w_in, shift_prev, shift_next, decay_w0, decay_w2, iclr_a0, iclr_a2, gate_w2, vres_v0, vres_w1, vres_w2, k_k, k_a, r_k, ln_x_gain, ln_x_bias, att_sink, att_out_gain, rel_bias_table, w_out, norm_mix, norm_xatt, mem_norm, xatt_wq, xatt_wk, xatt_wv, xatt_wo, norm_moe, router_coarse_w, router_coarse_b, router_fine_w, router_fine_b, expert_w_gate, expert_w_up, expert_w_down, final_norm):
    batch, seq_len, d = x.shape
    depth = w_in.shape[0]
    n = batch * seq_len
    mem_len = mem.shape[1]
    h = x.reshape(n, d)
    memf = mem.reshape(batch * mem_len, d)
    bias = _t5_bias(rel_bias_table)
    n_tiles = (2 * n) // MOE_TILE + N_EXPERTS
    wg_all = expert_w_gate.reshape(depth * N_EXPERTS, d, D_EXPERT)
    wu_all = expert_w_up.reshape(depth * N_EXPERTS, d, D_EXPERT)
    wd_all = expert_w_down.reshape(depth * N_EXPERTS, D_EXPERT, d)
    v_first = None
    for l in range(depth):
        w_rw, w_at = _split_in_weights(w_in[l])
        mup, _ = _split_in_weights(shift_prev[l][None, :])
        mun, _ = _split_in_weights(shift_next[l][None, :])
        w2p = jnp.pad(decay_w2[l], ((0, 0), (0, RANK_PAD - DECAY_RANK), (0, 0))).astype(BF16)
        a2p = jnp.pad(iclr_a2[l], ((0, 0), (0, RANK_PAD - ICLR_RANK), (0, 0))).astype(BF16)
        if l == 0:
            vres = None
        else:
            vres = (vres_v0[l - 1], _pad_cols(vres_w1[l - 1], RANK_PAD).astype(BF16),
                    _pad_rows(vres_w2[l - 1], RANK_PAD).astype(BF16))
        u = norm_matmul(h, norm_mix[l], w_rw.astype(BF16), 512, 768)
        qkv = norm_matmul(h, norm_mix[l], w_at.astype(BF16), 512, 768)
        r, v, kk, a0, a1, lw0, lw1, kd0, kd1, g = rwkv_prep(
            u, seq_len, mup, mun, decay_w0[l], w2p, iclr_a0[l], a2p, gate_w2[l].astype(BF16),
            k_k[l], k_a[l], vres, v_first)
        if l == 0:
            v_first = v
        yf, yb = rwkv_chunk(r, v, kk, a0, a1, lw0, lw1, kd0, kd1, batch, seq_len)
        y_r = rwkv_post(yf, yb, r, v, kd0, kd1, g, r_k[l], ln_x_gain[l], ln_x_bias[l])
        y_a = window_attention(qkv, bias, att_sink[l], att_out_gain[l], batch, seq_len)
        wo = w_out[l].astype(BF16)
        h = out_proj(y_r, y_a, wo[:C_RWKV], wo[C_RWKV:], h)
        wkv = jnp.concatenate([xatt_wk[l], xatt_wv[l]], axis=1).astype(BF16)
        kv = norm_matmul(memf, mem_norm[l], wkv, memf.shape[0], 2 * XATT_WIDTH)
        h = cross_attention(h, norm_xatt[l], xatt_wq[l].astype(BF16), kv, xatt_wo[l].astype(BF16), batch, seq_len)
        w_r = _pad_cols(jnp.concatenate([router_fine_w[l], router_coarse_w[l]], axis=1), LANES)
        b_r = _pad_cols(jnp.concatenate([router_fine_b[l].reshape(1, -1), router_coarse_b[l].reshape(1, -1)], axis=1), LANES)
        slab = moe_router(h, norm_moe[l], w_r, b_r)
        tile_expert, n_used, row_tok, pos = _routing_tables(slab, n_tiles)
        y_sorted = moe_experts(h, norm_moe[l], wg_all, wu_all, wd_all, l, tile_expert, n_used, row_tok, n_tiles)
        h = moe_combine(y_sorted, pos, h, slab, final_norm, l == depth - 1)
    return h.reshape(batch, seq_len, d)
```

```python
import functools
import math

import jax
import jax.numpy as jnp
from jax import lax
from jax.experimental import pallas as pl
from jax.experimental.pallas import tpu as pltpu

F32 = jnp.float32
BF16 = jnp.bfloat16
HI = lax.Precision.HIGHEST

D_MODEL = 2048
RWKV_HEADS = 16
HEAD_DIM = 64
C_RWKV = RWKV_HEADS * HEAD_DIM
ATT_Q_HEADS = 16
ATT_KV_HEADS = 4
ATT_WIDTH = ATT_Q_HEADS * HEAD_DIM
ATT_KV_WIDTH = ATT_KV_HEADS * HEAD_DIM
WINDOW = 128
BLOCK = 128
DECAY_RANK = 96
ICLR_RANK = 96
VRES_RANK = 64
GATE_RANK = 256
GN_EPS = 64e-5
N_BUCKETS = 32
MAX_DISTANCE = 128
XATT_HEADS = 4
XATT_HEAD_DIM = 128
XATT_WIDTH = XATT_HEADS * XATT_HEAD_DIM
N_GROUPS = 4
EXPERTS_PER_GROUP = 8
N_EXPERTS = N_GROUPS * EXPERTS_PER_GROUP
D_EXPERT = 512
EPS = 1e-6
NEG_INF = -1e30

LANES = 128
RANK_PAD = 128
R0, K0, V0 = 0, C_RWKV, 2 * C_RWKV
WD0 = 3 * C_RWKV
WD1 = WD0 + RANK_PAD
AD0 = WD1 + RANK_PAD
AD1 = AD0 + RANK_PAD
GD0 = AD1 + RANK_PAD
RW_COLS = GD0 + GATE_RANK
AT_COLS = ATT_WIDTH + 2 * ATT_KV_WIDTH

CHUNK = 64
MOE_TILE = 256
VMEM_LIMIT = 56 * 1024 * 1024


def _cparams(sem, vmem=VMEM_LIMIT):
    return pltpu.CompilerParams(dimension_semantics=sem, vmem_limit_bytes=vmem)


def _sigmoid(x):
    return 1.0 / (1.0 + jnp.exp(-x))


def _dot(a, b, prec=None):
    return jnp.dot(a, b, preferred_element_type=F32, precision=prec)


def _dot_nt(a, b, prec=None):
    return lax.dot_general(a, b, (((1,), (1,)), ((), ())), preferred_element_type=F32, precision=prec)


def _dot_tn(a, b, prec=None):
    return lax.dot_general(a, b, (((0,), (0,)), ((), ())), preferred_element_type=F32, precision=prec)


def _rms(x, g):
    ms = jnp.mean(x * x, axis=-1, keepdims=True)
    return x * lax.rsqrt(ms + EPS) * g


def _split_bf16(x):
    hi = x.astype(BF16)
    lo = (x - hi.astype(F32)).astype(BF16)
    return hi, lo


def _head_block_ones():
    ri = lax.broadcasted_iota(jnp.int32, (LANES, LANES), 0)
    ci = lax.broadcasted_iota(jnp.int32, (LANES, LANES), 1)
    return ((ri >> 6) == (ci >> 6)).astype(BF16)


def _head_sums(x, bd2):
    rows = x.shape[0]
    hi, lo = _split_bf16(x)
    outs = []
    for gi in range(x.shape[1] // LANES):
        sl = slice(gi * LANES, (gi + 1) * LANES)
        both = _dot(jnp.concatenate([hi[:, sl], lo[:, sl]], axis=0), bd2)
        outs.append(both[0:rows] + both[rows:])
    return outs[0] if len(outs) == 1 else jnp.concatenate(outs, axis=1)


def _norm_mm_kernel(x_ref, g_ref, w_ref, o_ref, xn_ref):
    @pl.when(pl.program_id(1) == 0)
    def _():
        xn_ref[...] = _rms(x_ref[...], g_ref[...]).astype(BF16)

    o_ref[...] = _dot(xn_ref[...], w_ref[...]).astype(o_ref.dtype)


def norm_matmul(x, g, w, tm, tn, out_dtype=F32):
    n, d = x.shape
    nc = w.shape[1]
    return pl.pallas_call(
        _norm_mm_kernel,
        grid=(n // tm, nc // tn),
        in_specs=[
            pl.BlockSpec((tm, d), lambda i, j: (i, 0)),
            pl.BlockSpec((1, d), lambda i, j: (0, 0)),
            pl.BlockSpec((d, tn), lambda i, j: (0, j)),
        ],
        out_specs=pl.BlockSpec((tm, tn), lambda i, j: (i, j)),
        out_shape=jax.ShapeDtypeStruct((n, nc), out_dtype),
        scratch_shapes=[pltpu.VMEM((tm, d), BF16)],
        compiler_params=_cparams(("parallel", "arbitrary")),
        name="norm_matmul",
    )(x, g.reshape(1, d), w)


def _resident(shape):
    nd = len(shape)
    return pl.BlockSpec(shape, lambda *_: (0,) * nd, pipeline_mode=pl.Buffered(1))


def _in_proj_kernel(x_ref, g_ref, w_ref, u_ref, qkv_ref):
    xn = _rms(x_ref[...], g_ref[...]).astype(BF16)
    nu = u_ref.shape[1]
    u_ref[...] = _dot(xn, w_ref[:, 0:nu])
    qkv_ref[...] = _dot(xn, w_ref[:, nu:])


def in_proj(x, g, w, n_rw, tm=256):
    n, d = x.shape
    nc = w.shape[1]
    return pl.pallas_call(
        _in_proj_kernel,
        grid=(n // tm,),
        in_specs=[
            pl.BlockSpec((tm, d), lambda i: (i, 0)),
            _resident((1, d)),
            _resident((d, nc)),
        ],
        out_specs=[pl.BlockSpec((tm, n_rw), lambda i: (i, 0)), pl.BlockSpec((tm, nc - n_rw), lambda i: (i, 0))],
        out_shape=[jax.ShapeDtypeStruct((n, n_rw), F32), jax.ShapeDtypeStruct((n, nc - n_rw), F32)],
        compiler_params=_cparams(("parallel",)),
        name="in_proj",
    )(x, g.reshape(1, d), w)


def _prep_kernel(seq_tiles, has_vres, *refs):
    if has_vres:
        (u_ref, hp_ref, hn_ref, mup_ref, mun_ref, w0_ref, w2_ref, a0_ref, a2_ref, gw_ref, kk_ref, ka_ref, rk_ref,
         vf_ref, v0_ref, v1_ref, v2_ref,
         r_o, v_o, kk_o, b0_o, b1_o, lw0_o, lw1_o, kd0_o, kd1_o, g_o, bg_o) = refs
    else:
        (u_ref, hp_ref, hn_ref, mup_ref, mun_ref, w0_ref, w2_ref, a0_ref, a2_ref, gw_ref, kk_ref, ka_ref, rk_ref,
         r_o, v_o, kk_o, b0_o, b1_o, lw0_o, lw1_o, kd0_o, kd1_o, g_o, bg_o) = refs
    tm = u_ref.shape[0]
    i = pl.program_id(0)
    it = i % seq_tiles
    has_prev = jnp.where(it != 0, 1.0, 0.0)
    has_next = jnp.where(it != seq_tiles - 1, 1.0, 0.0)
    row = lax.broadcasted_iota(jnp.int32, (tm, 1), 0)

    def shifted(c0, c1):
        u = u_ref[:, c0:c1]
        pr = hp_ref[7:8, c0:c1] * has_prev
        nx = hn_ref[0:1, c0:c1] * has_next
        prev = jnp.where(row == 0, pr, pltpu.roll(u, 1, 0))
        nxt = jnp.where(row == tm - 1, nx, pltpu.roll(u, tm - 1, 0))
        return u + mup_ref[:, c0:c1] * (prev - u) + mun_ref[:, c0:c1] * (nxt - u)

    bd2 = _head_block_ones()

    r = shifted(R0, R0 + C_RWKV)
    k = shifted(K0, K0 + C_RWKV)
    v = shifted(V0, V0 + C_RWKV)
    r_o[...] = r.astype(r_o.dtype)
    if has_vres:
        low = _dot(v.astype(BF16), v1_ref[...])
        mix = _sigmoid(v0_ref[...] + _dot(low.astype(BF16), v2_ref[...]))
        v = v + (vf_ref[...].astype(F32) - v) * mix
    v_o[...] = v.astype(v_o.dtype)
    kkraw = k * kk_ref[...]
    kk = kkraw / jnp.maximum(jnp.sqrt(_head_sums(kkraw * kkraw, bd2)), 1e-12)
    kk_o[...] = kk.astype(kk_o.dtype)
    ka = ka_ref[...]
    lw_scale = -math.exp(-0.5)
    kd_sum = None
    for z, (wc, ac, lw_o, b_o, kd_o) in enumerate(
            ((WD0, AD0, lw0_o, b0_o, kd0_o), (WD1, AD1, lw1_o, b1_o, kd1_o))):
        wd = shifted(wc, wc + RANK_PAD)
        w_pre = w0_ref[z:z + 1, :] + _dot(jnp.tanh(wd).astype(BF16), w2_ref[z])
        lw_o[...] = lw_scale * _sigmoid(w_pre)
        ad = shifted(ac, ac + RANK_PAD)
        a = _sigmoid(a0_ref[z:z + 1, :] + _dot(ad.astype(BF16), a2_ref[z]))
        b_o[...] = (kk * a).astype(b_o.dtype)
        kd = k * (1.0 + (a - 1.0) * ka)
        kd_o[...] = kd.astype(kd_o.dtype)
        kd_sum = kd if kd_sum is None else kd_sum + kd
    gd = shifted(GD0, GD0 + GATE_RANK)
    g = _dot(_sigmoid(gd).astype(BF16), gw_ref[...])
    g_o[...] = g.astype(g_o.dtype)
    bonus = _head_sums(r * kd_sum * rk_ref[...], bd2) * v
    bg_o[...] = (bonus * g).astype(bg_o.dtype)


def rwkv_prep(u, seq_len, mup, mun, w0, w2p, a0, a2p, gw, k_k, k_a, r_k, vres, v_first, tm=256):
    n = u.shape[0]
    seq_tiles = seq_len // tm
    hb = tm // 8
    nblk8 = n // 8
    has_vres = vres is not None
    c = C_RWKV

    def full(shape):
        nd = len(shape)
        return pl.BlockSpec(shape, lambda i: (0,) * nd)

    in_specs = [
        pl.BlockSpec((tm, RW_COLS), lambda i: (i, 0)),
        pl.BlockSpec((8, RW_COLS), lambda i: (jnp.maximum(i * hb - 1, 0), 0)),
        pl.BlockSpec((8, RW_COLS), lambda i: (jnp.minimum((i + 1) * hb, nblk8 - 1), 0)),
        full((1, RW_COLS)), full((1, RW_COLS)),
        full((2, c)), full((2, RANK_PAD, c)), full((2, c)), full((2, RANK_PAD, c)),
        full((GATE_RANK, c)), full((1, c)), full((1, c)), full((1, c)),
    ]
    args = [u, u, u, mup, mun, w0, w2p, a0, a2p, gw, k_k.reshape(1, c), k_a.reshape(1, c), r_k.reshape(1, c)]
    if has_vres:
        v0, v1p, v2p = vres
        in_specs += [pl.BlockSpec((tm, c), lambda i: (i, 0)), full((1, c)), full((c, RANK_PAD)), full((RANK_PAD, c))]
        args += [v_first, v0.reshape(1, c), v1p, v2p]
    out_spec = pl.BlockSpec((tm, c), lambda i: (i, 0))
    dtypes = [BF16] * 5 + [F32] * 2 + [BF16] * 4
    return pl.pallas_call(
        functools.partial(_prep_kernel, seq_tiles, has_vres),
        grid=(n // tm,),
        in_specs=in_specs,
        out_specs=[out_spec] * len(dtypes),
        out_shape=[jax.ShapeDtypeStruct((n, c), dt) for dt in dtypes],
        compiler_params=_cparams(("parallel",)),
        name="rwkv_prep",
    )(*args)


def _chunk_kernel(rf, vf, kkf, af, lwf, kdf, rb, vb, kkb, ab, lwb, kdb, yf_o, yb_o, s_ref):
    L, width = rf.shape
    L2 = 2 * L
    npairs = width // LANES

    @pl.when(pl.program_id(1) == 0)
    def _():
        s_ref[...] = jnp.zeros_like(s_ref)

    lane = lax.broadcasted_iota(jnp.int32, (1, LANES), 1)
    m1 = lane < HEAD_DIM
    t_i = lax.broadcasted_iota(jnp.int32, (L, L2), 0)
    s_i = lax.broadcasted_iota(jnp.int32, (L, L2), 1) & (L - 1)
    r2 = lax.broadcasted_iota(jnp.int32, (L2, L2), 0)
    c2 = lax.broadcasted_iota(jnp.int32, (L2, L2), 1)
    same = (r2 >= L) == (c2 >= L)
    tt = r2 & (L - 1)
    st = c2 & (L - 1)

    def stack(x):
        z = jnp.zeros_like(x)
        return jnp.concatenate([jnp.where(m1, x, z), jnp.where(m1, z, x)], axis=0)

    chains = []
    for rev, ins in ((False, (rf, vf, kkf, af, lwf, kdf)), (True, (rb, vb, kkb, ab, lwb, kdb))):
        if rev:
            tri2 = (s_i >= t_i).astype(BF16)
            strict = same & (st > tt)
            incl = same & (st >= tt)
        else:
            tri2 = (s_i <= t_i).astype(BF16)
            strict = same & (st < tt)
            incl = same & (st <= tt)
        r, vb16, kk, bn, lw, kd = (x[...] for x in ins)
        r, kk, bn, kd = (x.astype(F32) for x in (r, kk, bn, kd))
        c = _dot(tri2, jnp.concatenate(_split_bf16(lw), axis=0))
        c_last = c[0:1, :] if rev else c[L - 1:L, :]
        decay_last = jnp.exp(c_last)
        e_pos = jnp.exp(c)
        e_neg = jnp.exp(-c)
        e_rem = jnp.exp(c_last - c)
        a_t = (-kk * jnp.exp(c - lw)).astype(BF16)
        b_t = (bn * e_neg).astype(BF16)
        k_t = (kd * e_neg).astype(BF16)
        r_t = (r * e_pos).astype(BF16)
        b_h = (bn * e_rem).astype(BF16)
        k_h = (kd * e_rem).astype(BF16)
        for pr in range(npairs):
            sl = slice(pr * LANES, (pr + 1) * LANES)
            chains.append(dict(
                strict=strict, incl=incl, s_ref=s_ref.at[1 if rev else 0, pr], decay_last=decay_last[:, sl],
                ar=jnp.concatenate([stack(a_t[:, sl]), stack(r_t[:, sl])], axis=0),
                bk=jnp.concatenate([stack(b_t[:, sl]), stack(k_t[:, sl])], axis=0),
                bkh=jnp.concatenate([stack(b_h[:, sl]), stack(k_h[:, sl])], axis=0),
                v_s=stack(vb16[:, sl])))
    for ch in chains:
        ch["s0"] = ch["s_ref"][...]
        ch["sc"] = _dot_nt(ch["ar"], ch["bk"])
    for ch in chains:
        ch["from_state"] = _dot_nt(ch["ar"], ch["s0"].astype(BF16))
    for ch in chains:
        sc = ch["sc"]
        m_ak = jnp.where(ch["strict"], sc[0:L2, L2:], 0.0).astype(BF16)
        ch["m_r"] = jnp.concatenate([jnp.where(ch["incl"], sc[L2:, 0:L2], 0.0).astype(BF16),
                                     jnp.where(ch["incl"], sc[L2:, L2:], 0.0).astype(BF16)], axis=1)
        ch["p"] = jnp.where(ch["strict"], sc[0:L2, 0:L2], 0.0)
        ch["x"] = ch["from_state"][0:L2] + _dot(m_ak, ch["v_s"])
    n_steps = int(math.log2(L))
    for step in range(n_steps):
        for ch in chains:
            pb = ch["p"].astype(BF16)
            xb = ch["x"].astype(BF16)
            if step < n_steps - 1:
                res = _dot(pb, jnp.concatenate([xb, pb], axis=1))
                ch["x"] = ch["x"] + res[:, 0:LANES]
                ch["p"] = res[:, LANES:]
            else:
                ch["x"] = ch["x"] + _dot(pb, xb)
    for ch in chains:
        ch["uv"] = jnp.concatenate([ch["x"].astype(BF16), ch["v_s"]], axis=0)
        ch["y2"] = ch["from_state"][L2:] + _dot(ch["m_r"], ch["uv"])
    for ch in chains:
        ch["s_ref"][...] = ch["s0"] * ch["decay_last"] + _dot_tn(ch["uv"], ch["bkh"])
    for y_o, group in ((yf_o, chains[:npairs]), (yb_o, chains[npairs:])):
        y_o[...] = jnp.concatenate([ch["y2"][0:L, :] + ch["y2"][L:L2, :] for ch in group], axis=1).astype(y_o.dtype)


def rwkv_chunk(r, v, kk, b0, b1, lw0, lw1, kd0, kd1, batch, seq_len):
    c = C_RWKV
    L = CHUNK
    nc = seq_len // L

    def r3(x):
        return x.reshape(batch, seq_len, c)

    fwd = pl.BlockSpec((None, L, c), lambda b, t: (b, t, 0))
    bwd = pl.BlockSpec((None, L, c), lambda b, t: (b, nc - 1 - t, 0))
    sds = jax.ShapeDtypeStruct((batch, seq_len, c), BF16)
    yf, yb = pl.pallas_call(
        _chunk_kernel,
        grid=(batch, nc),
        in_specs=[fwd] * 6 + [bwd] * 6,
        out_specs=[fwd, bwd],
        out_shape=[sds, sds],
        scratch_shapes=[pltpu.VMEM((2, c // LANES, LANES, LANES), F32)],
        compiler_params=_cparams(("parallel", "arbitrary")),
        name="rwkv_chunk",
    )(r3(r), r3(v), r3(kk), r3(b0), r3(lw0), r3(kd0), r3(r), r3(v), r3(kk), r3(b1), r3(lw1), r3(kd1))
    return yf.reshape(-1, c), yb.reshape(-1, c)


def _post_kernel(yf, yb, g, bg, lng, lnb, o_ref):
    bd2 = _head_block_ones()
    y = yf[...].astype(F32) + yb[...].astype(F32)
    mu = _head_sums(y, bd2) * (1.0 / HEAD_DIM)
    d = y - mu
    var = _head_sums(d * d, bd2) * (1.0 / HEAD_DIM)
    yn = d * lax.rsqrt(var + GN_EPS) * lng[...] + lnb[...]
    o_ref[...] = (yn * g[...].astype(F32) + bg[...].astype(F32)).astype(o_ref.dtype)


def rwkv_post(yf, yb, g, bg, ln_g, ln_b, tm=512):
    n, c = yf.shape
    blk = pl.BlockSpec((tm, c), lambda i: (i, 0))
    vec = pl.BlockSpec((1, c), lambda i: (0, 0))
    return pl.pallas_call(
        _post_kernel,
        grid=(n // tm,),
        in_specs=[blk] * 4 + [vec] * 2,
        out_specs=blk,
        out_shape=jax.ShapeDtypeStruct((n, c), BF16),
        compiler_params=_cparams(("parallel",)),
        name="rwkv_post",
    )(yf, yb, g, bg, ln_g.reshape(1, c), ln_b.reshape(1, c))


def _attn_kernel(q_ref, kp, kc, kn, vp, vc, vn, bias_ref, sink_ref, gain_ref, o_ref):
    n = pl.program_id(1)
    nb = pl.num_programs(1)
    i = lax.broadcasted_iota(jnp.int32, (BLOCK, 3 * BLOCK), 0)
    j = lax.broadcasted_iota(jnp.int32, (BLOCK, 3 * BLOCK), 1)
    rel = j - BLOCK - i
    valid = (jnp.abs(rel) <= WINDOW) & ((j >= BLOCK) | (n > 0)) & ((j < 2 * BLOCK) | (n < nb - 1))
    kb = jnp.concatenate([kp[...], kc[...], kn[...]], axis=0).astype(BF16)
    vb = jnp.concatenate([vp[...], vc[...], vn[...]], axis=0).astype(BF16)
    q = (q_ref[...] * (HEAD_DIM ** -0.5)).astype(BF16)
    group = ATT_Q_HEADS // ATT_KV_HEADS
    valid_g = jnp.concatenate([valid] * group, axis=0)
    row_head = lax.broadcasted_iota(jnp.int32, (group * BLOCK, 1), 0) >> 7

    def scores(kh):
        qg = jnp.concatenate([q[:, (kh * group + gi) * HEAD_DIM:(kh * group + gi + 1) * HEAD_DIM]
                              for gi in range(group)], axis=0)
        return _dot_nt(qg, kb[:, kh * HEAD_DIM:(kh + 1) * HEAD_DIM])

    def finish(kh, s):
        bias = bias_ref[kh * group:(kh + 1) * group].reshape(group * BLOCK, 3 * BLOCK)
        s = jnp.where(valid_g, s + bias, NEG_INF)
        sk = jnp.zeros((group * BLOCK, 1), F32)
        for gi in range(group):
            sk = jnp.where(row_head == gi, sink_ref[kh * group + gi], sk)
        m = jnp.maximum(jnp.max(s, axis=-1, keepdims=True), sk)
        p = jnp.exp(s - m)
        denom = jnp.sum(p, axis=-1, keepdims=True) + jnp.exp(sk - m)
        og = _dot(p.astype(BF16), vb[:, kh * HEAD_DIM:(kh + 1) * HEAD_DIM]) / denom
        return [og[gi * BLOCK:(gi + 1) * BLOCK] for gi in range(group)]

    outs = []
    pending = scores(0)
    for kh in range(ATT_KV_HEADS):
        nxt = scores(kh + 1) if kh + 1 < ATT_KV_HEADS else None
        outs += finish(kh, pending)
        pending = nxt
    o = jnp.concatenate(outs, axis=-1)
    o_ref[...] = _rms(o, gain_ref[...]).astype(o_ref.dtype)


def window_attention(qkv, bias, sink, gain, batch, seq_len):
    n = qkv.shape[0]
    nb = seq_len // BLOCK
    kcol = ATT_WIDTH // ATT_KV_WIDTH
    vcol = kcol + 1

    def rows(off):
        def f(b, t):
            return jnp.clip(t + off, 0, nb - 1) + b * nb
        return f

    def kv_spec(col, off):
        f = rows(off)
        return pl.BlockSpec((BLOCK, ATT_KV_WIDTH), lambda b, t: (f(b, t), col))

    return pl.pallas_call(
        _attn_kernel,
        grid=(batch, nb),
        in_specs=[
            pl.BlockSpec((BLOCK, ATT_WIDTH), lambda b, t: (b * nb + t, 0)),
            kv_spec(kcol, -1), kv_spec(kcol, 0), kv_spec(kcol, 1),
            kv_spec(vcol, -1), kv_spec(vcol, 0), kv_spec(vcol, 1),
            pl.BlockSpec((ATT_Q_HEADS, BLOCK, 3 * BLOCK), lambda b, t: (0, 0, 0)),
            pl.BlockSpec(memory_space=pltpu.SMEM),
            pl.BlockSpec((1, ATT_WIDTH), lambda b, t: (0, 0)),
        ],
        out_specs=pl.BlockSpec((BLOCK, ATT_WIDTH), lambda b, t: (b * nb + t, 0)),
        out_shape=jax.ShapeDtypeStruct((n, ATT_WIDTH), BF16),
        compiler_params=_cparams(("parallel", "parallel")),
        name="window_attention",
    )(qkv, qkv, qkv, qkv, qkv, qkv, qkv, bias, sink, gain.reshape(1, ATT_WIDTH))


def _t5_bias(rel_bias_table):
    i = jnp.arange(BLOCK)[:, None]
    j = jnp.arange(3 * BLOCK)[None, :]
    rel = j - BLOCK - i
    nbk = N_BUCKETS // 2
    max_exact = nbk // 2
    ret = jnp.where(rel > 0, nbk, 0)
    nabs = jnp.abs(rel)
    large = max_exact + (jnp.log(jnp.maximum(nabs, max_exact).astype(jnp.float32) / max_exact)
                         / math.log(MAX_DISTANCE / max_exact) * (nbk - max_exact)).astype(jnp.int32)
    large = jnp.minimum(large, nbk - 1)
    bucket = ret + jnp.where(nabs < max_exact, nabs, large)
    onehot = (bucket[None, :, :] == jnp.arange(N_BUCKETS)[:, None, None]).astype(F32)
    return jnp.einsum("bh,bij->hij", rel_bias_table.astype(F32), onehot, precision=HI)


def _outproj_kernel(yr, ya, w1, w2, h_ref, o_ref):
    o_ref[...] = h_ref[...] + _dot(yr[...], w1[...]) + _dot(ya[...], w2[...])


def out_proj(y_r, y_a, w1, w2, h, tm=512):
    n, d = h.shape
    c = y_r.shape[1]
    return pl.pallas_call(
        _outproj_kernel,
        grid=(n // tm,),
        in_specs=[
            pl.BlockSpec((tm, c), lambda i: (i, 0)),
            pl.BlockSpec((tm, c), lambda i: (i, 0)),
            _resident((c, d)),
            _resident((c, d)),
            pl.BlockSpec((tm, d), lambda i: (i, 0)),
        ],
        out_specs=pl.BlockSpec((tm, d), lambda i: (i, 0)),
        out_shape=jax.ShapeDtypeStruct((n, d), F32),
        compiler_params=_cparams(("parallel",)),
        name="out_proj",
    )(y_r, y_a, w1, w2, h)


def _xatt_kernel(h_ref, g_ref, wq, kv_ref, wo, o_ref):
    h = h_ref[...]
    hn = _rms(h, g_ref[...]).astype(BF16)
    q = _dot(hn, wq[...])
    kv = kv_ref[...]
    scale = XATT_HEAD_DIM ** -0.5
    outs = []
    for hd in range(XATT_HEADS):
        sl = slice(hd * XATT_HEAD_DIM, (hd + 1) * XATT_HEAD_DIM)
        k_h = kv[:, sl].astype(BF16)
        v_h = kv[:, XATT_WIDTH + hd * XATT_HEAD_DIM: XATT_WIDTH + (hd + 1) * XATT_HEAD_DIM].astype(BF16)
        s = _dot_nt(q[:, sl].astype(BF16), k_h) * scale
        m = jnp.max(s, axis=-1, keepdims=True)
        p = jnp.exp(s - m)
        p = p / jnp.sum(p, axis=-1, keepdims=True)
        outs.append(_dot(p.astype(BF16), v_h))
    o = jnp.concatenate(outs, axis=-1).astype(BF16)
    o_ref[...] = h + _dot(o, wo[...])


def cross_attention(h, g, wq, kv, wo, batch, seq_len, tm=256):
    n, d = h.shape
    m = kv.shape[0] // batch
    tiles = seq_len // tm
    return pl.pallas_call(
        _xatt_kernel,
        grid=(n // tm,),
        in_specs=[
            pl.BlockSpec((tm, d), lambda i: (i, 0)),
            pl.BlockSpec((1, d), lambda i: (0, 0)),
            pl.BlockSpec((d, XATT_WIDTH), lambda i: (0, 0)),
            pl.BlockSpec((m, 2 * XATT_WIDTH), lambda i: (i // tiles, 0)),
            pl.BlockSpec((XATT_WIDTH, d), lambda i: (0, 0)),
        ],
        out_specs=pl.BlockSpec((tm, d), lambda i: (i, 0)),
        out_shape=jax.ShapeDtypeStruct((n, d), F32),
        compiler_params=_cparams(("parallel",)),
        name="cross_attention",
    )(h, g.reshape(1, d), wq, kv, wo)


def _router_kernel(h_ref, g_ref, w_ref, b_ref, o_ref):
    hn = _rms(h_ref[...], g_ref[...])
    logits = _dot(hn, w_ref[...], HI) + b_ref[...]
    tm = logits.shape[0]
    lane = lax.broadcasted_iota(jnp.int32, (tm, LANES), 1)
    is_coarse = (lane >= N_EXPERTS) & (lane < N_EXPERTS + N_GROUPS)
    cl = jnp.where(is_coarse, logits, NEG_INF)
    cmax = jnp.max(cl, axis=-1, keepdims=True)
    csum = jnp.sum(jnp.where(is_coarse, jnp.exp(cl - cmax), 0.0), axis=-1, keepdims=True)
    p_g = 1.0 / csum
    lane_f = lane.astype(F32)
    grp_f = (lane >> 3).astype(F32)
    big = float(LANES)
    g_lane = jnp.min(jnp.where(is_coarse & (cl == cmax), lane_f, big), axis=-1, keepdims=True)
    g_idx = g_lane - float(N_EXPERTS)
    in_grp = (lane < N_EXPERTS) & (grp_f == g_idx)
    fl = jnp.where(in_grp, logits, NEG_INF)
    m1 = jnp.max(fl, axis=-1, keepdims=True)
    i1 = jnp.min(jnp.where(in_grp & (fl == m1), lane_f, big), axis=-1, keepdims=True)
    fl2 = jnp.where(lane_f == i1, NEG_INF, fl)
    m2 = jnp.max(fl2, axis=-1, keepdims=True)
    i2 = jnp.min(jnp.where(in_grp & (lane_f != i1) & (fl2 == m2), lane_f, big), axis=-1, keepdims=True)
    e2 = jnp.exp(m2 - m1)
    w1 = p_g / (1.0 + e2)
    w2 = p_g * e2 / (1.0 + e2)
    out = jnp.where(lane == 0, i1,
                    jnp.where(lane == 1, i2,
                              jnp.where(lane == 2, w1, jnp.where(lane == 3, w2, 0.0))))
    o_ref[...] = out


def moe_router(h, g, w_r, b_r, tm=512):
    n, d = h.shape
    return pl.pallas_call(
        _router_kernel,
        grid=(n // tm,),
        in_specs=[
            pl.BlockSpec((tm, d), lambda i: (i, 0)),
            pl.BlockSpec((1, d), lambda i: (0, 0)),
            pl.BlockSpec((d, LANES), lambda i: (0, 0)),
            pl.BlockSpec((1, LANES), lambda i: (0, 0)),
        ],
        out_specs=pl.BlockSpec((tm, LANES), lambda i: (i, 0)),
        out_shape=jax.ShapeDtypeStruct((n, LANES), F32),
        compiler_params=_cparams(("parallel",)),
        name="moe_router",
    )(h, g.reshape(1, d), w_r, b_r)


GATHER_UNROLL = 8


def _start_row_gather(src_hbm, dst, sem, idx_ref, first, stride, n_rows):
    def body(r, carry):
        row = idx_ref[first + r * stride]
        pltpu.make_async_copy(src_hbm.at[pl.ds(row, 1), :], dst.at[pl.ds(r, 1), :], sem).start()
        return carry

    lax.fori_loop(0, n_rows, body, 0, unroll=GATHER_UNROLL)


def _wait_row_gather(dst, sem):
    pltpu.make_async_copy(dst, dst, sem).wait()


def _expert_kernel(e0, te_ref, nxt_ref, par_ref, nused_ref, tok_ref, h_hbm, g_ref, wg_hbm, wu_hbm, wd_hbm, o_ref,
                   xbuf, wgf, wuf, wdf, wgb, wub, wdb, xsem, wsem):
    t = pl.program_id(0)
    tm = xbuf.shape[1]
    n_used = nused_ref[0]
    slot = t % 2

    def weight_copies(expert, ws):
        return [pltpu.make_async_copy(src.at[e0 + expert], dst.at[ws], wsem.at[ws, j])
                for j, (src, dst) in enumerate(((wg_hbm, wgf), (wu_hbm, wuf), (wd_hbm, wdf)))]

    @pl.when(t == 0)
    def _():
        _start_row_gather(h_hbm, xbuf.at[0], xsem.at[0], tok_ref, 0, 1, tm)
        for cp in weight_copies(te_ref[0], 0):
            cp.start()

    @pl.when(t + 1 < n_used)
    def _():
        _start_row_gather(h_hbm, xbuf.at[1 - slot], xsem.at[1 - slot], tok_ref, (t + 1) * tm, 1, tm)

    @pl.when((t == 0) | (te_ref[t] != te_ref[jnp.maximum(t - 1, 0)]))
    def _():
        ws = par_ref[t]
        for cp in weight_copies(te_ref[t], ws):
            cp.wait()

        @pl.when(nxt_ref[t] >= 0)
        def _():
            for cp in weight_copies(nxt_ref[t], 1 - ws):
                cp.start()

        wgb[...] = wgf[ws].astype(BF16)
        wub[...] = wuf[ws].astype(BF16)
        wdb[...] = wdf[ws].astype(BF16)

    @pl.when(t < n_used)
    def _():
        _wait_row_gather(xbuf.at[slot], xsem.at[slot])
        xn = _rms(xbuf[slot], g_ref[...]).astype(BF16)
        hg = _dot(xn, wgb[...])
        hu = _dot(xn, wub[...])
        act = (hg * _sigmoid(hg) * hu).astype(BF16)
        o_ref[...] = _dot(act, wdb[...])

    @pl.when(t >= n_used)
    def _():
        o_ref[...] = jnp.zeros_like(o_ref)


def moe_experts(h, g, w_gate, w_up, w_down, layer, tables, n_tiles):
    n, d = h.shape
    tm = MOE_TILE
    tile_expert, next_expert, parity, n_used, row_tok = tables
    grid_spec = pltpu.PrefetchScalarGridSpec(
        num_scalar_prefetch=5,
        grid=(n_tiles,),
        in_specs=[
            pl.BlockSpec(memory_space=pl.ANY),
            pl.BlockSpec((1, d), lambda t, *_: (0, 0)),
            pl.BlockSpec(memory_space=pl.ANY),
            pl.BlockSpec(memory_space=pl.ANY),
            pl.BlockSpec(memory_space=pl.ANY),
        ],
        out_specs=pl.BlockSpec((tm, d), lambda t, *_: (t, 0)),
        scratch_shapes=[pltpu.VMEM((2, tm, d), F32),
                        pltpu.VMEM((2, d, D_EXPERT), F32), pltpu.VMEM((2, d, D_EXPERT), F32),
                        pltpu.VMEM((2, D_EXPERT, d), F32),
                        pltpu.VMEM((d, D_EXPERT), BF16), pltpu.VMEM((d, D_EXPERT), BF16),
                        pltpu.VMEM((D_EXPERT, d), BF16),
                        pltpu.SemaphoreType.DMA((2,)), pltpu.SemaphoreType.DMA((2, 3))],
    )
    return pl.pallas_call(
        functools.partial(_expert_kernel, layer * N_EXPERTS),
        grid_spec=grid_spec,
        out_shape=jax.ShapeDtypeStruct((n_tiles * tm, d), F32),
        compiler_params=_cparams(("arbitrary",)),
        name="moe_experts",
    )(tile_expert, next_expert, parity, n_used, row_tok, h, g.reshape(1, d), w_gate, w_up, w_down)


def _combine_kernel(final, pos_ref, y_hbm, h_ref, slab_ref, fg_ref, o_ref, buf, sem):
    i = pl.program_id(0)
    n_steps = pl.num_programs(0)
    tm = h_ref.shape[0]
    slot = i % 2

    def start(step, sl):
        for j in range(2):
            _start_row_gather(y_hbm, buf.at[sl, j], sem.at[sl, j], pos_ref, 2 * step * tm + j, 2, tm)

    @pl.when(i == 0)
    def _():
        start(0, 0)

    @pl.when(i + 1 < n_steps)
    def _():
        start(i + 1, 1 - slot)

    for j in range(2):
        _wait_row_gather(buf.at[slot, j], sem.at[slot, j])
    slab = slab_ref[...]
    out = h_ref[...] + slab[:, 2:3] * buf[slot, 0] + slab[:, 3:4] * buf[slot, 1]
    if final:
        out = _rms(out, fg_ref[...])
    o_ref[...] = out


def moe_combine(y_sorted, pos, h, slab, final_gain, final, tm=128):
    n, d = h.shape
    grid_spec = pltpu.PrefetchScalarGridSpec(
        num_scalar_prefetch=1,
        grid=(n // tm,),
        in_specs=[
            pl.BlockSpec(memory_space=pl.ANY),
            pl.BlockSpec((tm, d), lambda i, p: (i, 0)),
            pl.BlockSpec((tm, LANES), lambda i, p: (i, 0)),
            pl.BlockSpec((1, d), lambda i, p: (0, 0)),
        ],
        out_specs=pl.BlockSpec((tm, d), lambda i, p: (i, 0)),
        scratch_shapes=[pltpu.VMEM((2, 2, tm, d), F32), pltpu.SemaphoreType.DMA((2, 2))],
    )
    return pl.pallas_call(
        functools.partial(_combine_kernel, final),
        grid_spec=grid_spec,
        out_shape=jax.ShapeDtypeStruct((n, d), F32),
        compiler_params=_cparams(("arbitrary",)),
        name="moe_combine",
    )(pos, y_sorted, h, slab, final_gain.reshape(1, d))


def _routing_tables(slab, n_tiles):
    n = slab.shape[0]
    tm = MOE_TILE
    e = slab[:, 0:2].astype(jnp.int32).reshape(-1)
    onehot = (e[:, None] == jnp.arange(N_EXPERTS, dtype=jnp.int32)[None, :]).astype(jnp.int32)
    csum = jnp.cumsum(onehot, axis=0)
    counts = csum[-1]
    rank = jnp.take_along_axis(csum, e[:, None], axis=1)[:, 0] - 1
    tiles_per = (counts + tm - 1) // tm
    tile_end = jnp.cumsum(tiles_per)
    tile_start = tile_end - tiles_per
    pos = tile_start[e] * tm + rank
    row_tok = jnp.zeros((n_tiles * tm,), jnp.int32).at[pos].set(jnp.arange(2 * n, dtype=jnp.int32) // 2)
    n_used = tile_end[-1]
    t_idx = jnp.arange(n_tiles, dtype=jnp.int32)
    tile_expert = jnp.searchsorted(tile_end, jnp.minimum(t_idx, n_used - 1), side="right").astype(jnp.int32)
    tile_expert = jnp.minimum(tile_expert, N_EXPERTS - 1)
    after = tile_end[tile_expert]
    next_expert = jnp.where(after < n_used, tile_expert[jnp.minimum(after, n_tiles - 1)], -1).astype(jnp.int32)
    change = jnp.concatenate([jnp.zeros((1,), jnp.int32), (tile_expert[1:] != tile_expert[:-1]).astype(jnp.int32)])
    parity = (jnp.cumsum(change) % 2).astype(jnp.int32)
    tables = (tile_expert, next_expert, parity, n_used.reshape(1).astype(jnp.int32), row_tok)
    return tables, pos.astype(jnp.int32)


def _pad_rows(w, rows):
    return jnp.pad(w, ((0, rows - w.shape[0]), (0, 0)))


def _pad_cols(w, cols):
    return jnp.pad(w, ((0, 0), (0, cols - w.shape[1])))


def _split_in_weights(w):
    c = C_RWKV
    o = 3 * c
    wd0 = w[..., o:o + DECAY_RANK]
    wd1 = w[..., o + DECAY_RANK:o + 2 * DECAY_RANK]
    o += 2 * DECAY_RANK
    ad0 = w[..., o:o + ICLR_RANK]
    ad1 = w[..., o + ICLR_RANK:o + 2 * ICLR_RANK]
    o += 2 * ICLR_RANK
    gd = w[..., o:o + GATE_RANK]
    o += GATE_RANK
    att = w[..., o:]

    def padl(x):
        pad = [(0, 0)] * (x.ndim - 1) + [(0, RANK_PAD - x.shape[-1])]
        return jnp.pad(x, pad)

    rw = jnp.concatenate([w[..., :3 * c], padl(wd0), padl(wd1), padl(ad0), padl(ad1), gd], axis=-1)
    return rw, att


def kernel(x, mem, w_in, shift_prev, shift_next, decay_w0, decay_w2, iclr_a0, iclr_a2, gate_w2, vres_v0, vres_w1, vres_w2, k_k, k_a, r_k, ln_x_gain, ln_x_bias, att_sink, att_out_gain, rel_bias_table, w_out, norm_mix, norm_xatt, mem_norm, xatt_wq, xatt_wk, xatt_wv, xatt_wo, norm_moe, router_coarse_w, router_coarse_b, router_fine_w, router_fine_b, expert_w_gate, expert_w_up, expert_w_down, final_norm):
    batch, seq_len, d = x.shape
    depth = w_in.shape[0]
    n = batch * seq_len
    mem_len = mem.shape[1]
    h = x.reshape(n, d)
    memf = mem.reshape(batch * mem_len, d)
    bias = _t5_bias(rel_bias_table)
    n_tiles = (2 * n) // MOE_TILE + N_EXPERTS
    wg_all = expert_w_gate.reshape(depth * N_EXPERTS, d, D_EXPERT)
    wu_all = expert_w_up.reshape(depth * N_EXPERTS, d, D_EXPERT)
    wd_all = expert_w_down.reshape(depth * N_EXPERTS, D_EXPERT, d)
    v_first = None
    for l in range(depth):
        w_rw, w_at = _split_in_weights(w_in[l])
        mup, _ = _split_in_weights(shift_prev[l][None, :])
        mun, _ = _split_in_weights(shift_next[l][None, :])
        w2p = jnp.pad(decay_w2[l], ((0, 0), (0, RANK_PAD - DECAY_RANK), (0, 0))).astype(BF16)
        a2p = jnp.pad(iclr_a2[l], ((0, 0), (0, RANK_PAD - ICLR_RANK), (0, 0))).astype(BF16)
        if l == 0:
            vres = None
        else:
            vres = (vres_v0[l - 1], _pad_cols(vres_w1[l - 1], RANK_PAD).astype(BF16),
                    _pad_rows(vres_w2[l - 1], RANK_PAD).astype(BF16))
        u, qkv = in_proj(h, norm_mix[l], jnp.concatenate([w_rw, w_at], axis=1).astype(BF16), RW_COLS)
        r, v, kk, b0, b1, lw0, lw1, kd0, kd1, g, bg = rwkv_prep(
            u, seq_len, mup, mun, decay_w0[l], w2p, iclr_a0[l], a2p, gate_w2[l].astype(BF16),
            k_k[l], k_a[l], r_k[l], vres, v_first)
        if l == 0:
            v_first = v
        yf, yb = rwkv_chunk(r, v, kk, b0, b1, lw0, lw1, kd0, kd1, batch, seq_len)
        y_r = rwkv_post(yf, yb, g, bg, ln_x_gain[l], ln_x_bias[l])
        y_a = window_attention(qkv, bias, att_sink[l], att_out_gain[l], batch, seq_len)
        wo = w_out[l].astype(BF16)
        h = out_proj(y_r, y_a, wo[:C_RWKV], wo[C_RWKV:], h)
        wkv = jnp.concatenate([xatt_wk[l], xatt_wv[l]], axis=1).astype(BF16)
        kv = norm_matmul(memf, mem_norm[l], wkv, memf.shape[0], 2 * XATT_WIDTH)
        h = cross_attention(h, norm_xatt[l], xatt_wq[l].astype(BF16), kv, xatt_wo[l].astype(BF16), batch, seq_len)
        w_r = _pad_cols(jnp.concatenate([router_fine_w[l], router_coarse_w[l]], axis=1), LANES)
        b_r = _pad_cols(jnp.concatenate([router_fine_b[l].reshape(1, -1), router_coarse_b[l].reshape(1, -1)], axis=1), LANES)
        slab = moe_router(h, norm_moe[l], w_r, b_r)
        tables, pos = _routing_tables(slab, n_tiles)
        y_sorted = moe_experts(h, norm_moe[l], wg_all, wu_all, wd_all, l, tables, n_tiles)
        h = moe_combine(y_sorted, pos, h, slab, final_norm, l == depth - 1)
    return h.reshape(batch, seq_len, d)
```

```python
import functools
import math

import jax
import jax.numpy as jnp
from jax import lax
from jax.experimental import pallas as pl
from jax.experimental.pallas import tpu as pltpu

F32 = jnp.float32
BF16 = jnp.bfloat16
HI = lax.Precision.HIGHEST

D_MODEL = 2048
RWKV_HEADS = 16
HEAD_DIM = 64
C_RWKV = RWKV_HEADS * HEAD_DIM
ATT_Q_HEADS = 16
ATT_KV_HEADS = 4
ATT_WIDTH = ATT_Q_HEADS * HEAD_DIM
ATT_KV_WIDTH = ATT_KV_HEADS * HEAD_DIM
WINDOW = 128
BLOCK = 128
DECAY_RANK = 96
ICLR_RANK = 96
VRES_RANK = 64
GATE_RANK = 256
GN_EPS = 64e-5
N_BUCKETS = 32
MAX_DISTANCE = 128
XATT_HEADS = 4
XATT_HEAD_DIM = 128
XATT_WIDTH = XATT_HEADS * XATT_HEAD_DIM
N_GROUPS = 4
EXPERTS_PER_GROUP = 8
N_EXPERTS = N_GROUPS * EXPERTS_PER_GROUP
D_EXPERT = 512
EPS = 1e-6
NEG_INF = -1e30

LANES = 128
RANK_PAD = 128
R0, K0, V0 = 0, C_RWKV, 2 * C_RWKV
WD0 = 3 * C_RWKV
WD1 = WD0 + RANK_PAD
AD0 = WD1 + RANK_PAD
AD1 = AD0 + RANK_PAD
GD0 = AD1 + RANK_PAD
RW_COLS = GD0 + GATE_RANK
AT_COLS = ATT_WIDTH + 2 * ATT_KV_WIDTH

CHUNK = 64
MOE_TILE = 256
MOE_ROW_SLOTS = 4
VMEM_LIMIT = 56 * 1024 * 1024


def _cparams(sem, vmem=VMEM_LIMIT):
    return pltpu.CompilerParams(dimension_semantics=sem, vmem_limit_bytes=vmem)


def _sigmoid(x):
    return 1.0 / (1.0 + jnp.exp(-x))


def _dot(a, b, prec=None):
    return jnp.dot(a, b, preferred_element_type=F32, precision=prec)


def _dot_nt(a, b, prec=None):
    return lax.dot_general(a, b, (((1,), (1,)), ((), ())), preferred_element_type=F32, precision=prec)


def _dot_tn(a, b, prec=None):
    return lax.dot_general(a, b, (((0,), (0,)), ((), ())), preferred_element_type=F32, precision=prec)


def _rms(x, g):
    ms = jnp.mean(x * x, axis=-1, keepdims=True)
    return x * lax.rsqrt(ms + EPS) * g


def _split_bf16(x):
    hi = x.astype(BF16)
    lo = (x - hi.astype(F32)).astype(BF16)
    return hi, lo


def _head_block_ones():
    ri = lax.broadcasted_iota(jnp.int32, (LANES, LANES), 0)
    ci = lax.broadcasted_iota(jnp.int32, (LANES, LANES), 1)
    return ((ri >> 6) == (ci >> 6)).astype(BF16)


def _head_sums(x, bd2):
    rows = x.shape[0]
    hi, lo = _split_bf16(x)
    outs = []
    for gi in range(x.shape[1] // LANES):
        sl = slice(gi * LANES, (gi + 1) * LANES)
        both = _dot(jnp.concatenate([hi[:, sl], lo[:, sl]], axis=0), bd2)
        outs.append(both[0:rows] + both[rows:])
    return outs[0] if len(outs) == 1 else jnp.concatenate(outs, axis=1)


def _norm_mm_kernel(x_ref, g_ref, w_ref, o_ref, xn_ref):
    @pl.when(pl.program_id(1) == 0)
    def _():
        xn_ref[...] = _rms(x_ref[...], g_ref[...]).astype(BF16)

    o_ref[...] = _dot(xn_ref[...], w_ref[...]).astype(o_ref.dtype)


def norm_matmul(x, g, w, tm, tn, out_dtype=F32):
    n, d = x.shape
    nc = w.shape[1]
    return pl.pallas_call(
        _norm_mm_kernel,
        grid=(n // tm, nc // tn),
        in_specs=[
            pl.BlockSpec((tm, d), lambda i, j: (i, 0)),
            pl.BlockSpec((1, d), lambda i, j: (0, 0)),
            pl.BlockSpec((d, tn), lambda i, j: (0, j)),
        ],
        out_specs=pl.BlockSpec((tm, tn), lambda i, j: (i, j)),
        out_shape=jax.ShapeDtypeStruct((n, nc), out_dtype),
        scratch_shapes=[pltpu.VMEM((tm, d), BF16)],
        compiler_params=_cparams(("parallel", "arbitrary")),
        name="norm_matmul",
    )(x, g.reshape(1, d), w)


def _resident(shape):
    nd = len(shape)
    return pl.BlockSpec(shape, lambda *_: (0,) * nd, pipeline_mode=pl.Buffered(1))


def _in_proj_kernel(x_ref, g_ref, w_ref, u_ref, qkv_ref):
    xn = _rms(x_ref[...], g_ref[...]).astype(BF16)
    nu = u_ref.shape[1]
    u_ref[...] = _dot(xn, w_ref[:, 0:nu])
    qkv_ref[...] = _dot(xn, w_ref[:, nu:])


def in_proj(x, g, w, n_rw, tm=256):
    n, d = x.shape
    nc = w.shape[1]
    return pl.pallas_call(
        _in_proj_kernel,
        grid=(n // tm,),
        in_specs=[
            pl.BlockSpec((tm, d), lambda i: (i, 0)),
            _resident((1, d)),
            _resident((d, nc)),
        ],
        out_specs=[pl.BlockSpec((tm, n_rw), lambda i: (i, 0)), pl.BlockSpec((tm, nc - n_rw), lambda i: (i, 0))],
        out_shape=[jax.ShapeDtypeStruct((n, n_rw), F32), jax.ShapeDtypeStruct((n, nc - n_rw), F32)],
        compiler_params=_cparams(("parallel",)),
        name="in_proj",
    )(x, g.reshape(1, d), w)


def _prep_kernel(seq_tiles, has_vres, *refs):
    if has_vres:
        (u_ref, hp_ref, hn_ref, mup_ref, mun_ref, w0_ref, w2_ref, a0_ref, a2_ref, gw_ref, kk_ref, ka_ref, rk_ref,
         vf_ref, v0_ref, v1_ref, v2_ref,
         r_o, v_o, kk_o, b0_o, b1_o, lw0_o, lw1_o, kd0_o, kd1_o, g_o, bg_o) = refs
    else:
        (u_ref, hp_ref, hn_ref, mup_ref, mun_ref, w0_ref, w2_ref, a0_ref, a2_ref, gw_ref, kk_ref, ka_ref, rk_ref,
         r_o, v_o, kk_o, b0_o, b1_o, lw0_o, lw1_o, kd0_o, kd1_o, g_o, bg_o) = refs
    tm = u_ref.shape[0]
    i = pl.program_id(0)
    it = i % seq_tiles
    has_prev = jnp.where(it != 0, 1.0, 0.0)
    has_next = jnp.where(it != seq_tiles - 1, 1.0, 0.0)
    row = lax.broadcasted_iota(jnp.int32, (tm, 1), 0)

    def shifted(c0, c1):
        u = u_ref[:, c0:c1]
        pr = hp_ref[7:8, c0:c1] * has_prev
        nx = hn_ref[0:1, c0:c1] * has_next
        prev = jnp.where(row == 0, pr, pltpu.roll(u, 1, 0))
        nxt = jnp.where(row == tm - 1, nx, pltpu.roll(u, tm - 1, 0))
        return u + mup_ref[:, c0:c1] * (prev - u) + mun_ref[:, c0:c1] * (nxt - u)

    bd2 = _head_block_ones()

    r = shifted(R0, R0 + C_RWKV)
    k = shifted(K0, K0 + C_RWKV)
    v = shifted(V0, V0 + C_RWKV)
    r_o[...] = r.astype(r_o.dtype)
    if has_vres:
        low = _dot(v.astype(BF16), v1_ref[...])
        mix = _sigmoid(v0_ref[...] + _dot(low.astype(BF16), v2_ref[...]))
        v = v + (vf_ref[...].astype(F32) - v) * mix
    v_o[...] = v.astype(v_o.dtype)
    kkraw = k * kk_ref[...]
    kk = kkraw / jnp.maximum(jnp.sqrt(_head_sums(kkraw * kkraw, bd2)), 1e-12)
    kk_o[...] = kk.astype(kk_o.dtype)
    ka = ka_ref[...]
    lw_scale = -math.exp(-0.5)
    kd_sum = None
    for z, (wc, ac, lw_o, b_o, kd_o) in enumerate(
            ((WD0, AD0, lw0_o, b0_o, kd0_o), (WD1, AD1, lw1_o, b1_o, kd1_o))):
        wd = shifted(wc, wc + RANK_PAD)
        w_pre = w0_ref[z:z + 1, :] + _dot(jnp.tanh(wd).astype(BF16), w2_ref[z])
        lw_o[...] = lw_scale * _sigmoid(w_pre)
        ad = shifted(ac, ac + RANK_PAD)
        a = _sigmoid(a0_ref[z:z + 1, :] + _dot(ad.astype(BF16), a2_ref[z]))
        b_o[...] = (kk * a).astype(b_o.dtype)
        kd = k * (1.0 + (a - 1.0) * ka)
        kd_o[...] = kd.astype(kd_o.dtype)
        kd_sum = kd if kd_sum is None else kd_sum + kd
    gd = shifted(GD0, GD0 + GATE_RANK)
    g = _dot(_sigmoid(gd).astype(BF16), gw_ref[...])
    g_o[...] = g.astype(g_o.dtype)
    bonus = _head_sums(r * kd_sum * rk_ref[...], bd2) * v
    bg_o[...] = (bonus * g).astype(bg_o.dtype)


def rwkv_prep(u, seq_len, mup, mun, w0, w2p, a0, a2p, gw, k_k, k_a, r_k, vres, v_first, tm=256):
    n = u.shape[0]
    seq_tiles = seq_len // tm
    hb = tm // 8
    nblk8 = n // 8
    has_vres = vres is not None
    c = C_RWKV

    def full(shape):
        nd = len(shape)
        return pl.BlockSpec(shape, lambda i: (0,) * nd)

    in_specs = [
        pl.BlockSpec((tm, RW_COLS), lambda i: (i, 0)),
        pl.BlockSpec((8, RW_COLS), lambda i: (jnp.maximum(i * hb - 1, 0), 0)),
        pl.BlockSpec((8, RW_COLS), lambda i: (jnp.minimum((i + 1) * hb, nblk8 - 1), 0)),
        full((1, RW_COLS)), full((1, RW_COLS)),
        full((2, c)), full((2, RANK_PAD, c)), full((2, c)), full((2, RANK_PAD, c)),
        full((GATE_RANK, c)), full((1, c)), full((1, c)), full((1, c)),
    ]
    args = [u, u, u, mup, mun, w0, w2p, a0, a2p, gw, k_k.reshape(1, c), k_a.reshape(1, c), r_k.reshape(1, c)]
    if has_vres:
        v0, v1p, v2p = vres
        in_specs += [pl.BlockSpec((tm, c), lambda i: (i, 0)), full((1, c)), full((c, RANK_PAD)), full((RANK_PAD, c))]
        args += [v_first, v0.reshape(1, c), v1p, v2p]
    out_spec = pl.BlockSpec((tm, c), lambda i: (i, 0))
    dtypes = [BF16] * 5 + [F32] * 2 + [BF16] * 4
    return pl.pallas_call(
        functools.partial(_prep_kernel, seq_tiles, has_vres),
        grid=(n // tm,),
        in_specs=in_specs,
        out_specs=[out_spec] * len(dtypes),
        out_shape=[jax.ShapeDtypeStruct((n, c), dt) for dt in dtypes],
        compiler_params=_cparams(("parallel",)),
        name="rwkv_prep",
    )(*args)


def _chunk_kernel(rf, vf, kkf, af, lwf, kdf, rb, vb, kkb, ab, lwb, kdb, yf_o, yb_o, s_ref):
    L, width = rf.shape
    L2 = 2 * L
    npairs = width // LANES

    @pl.when(pl.program_id(1) == 0)
    def _():
        s_ref[...] = jnp.zeros_like(s_ref)

    lane = lax.broadcasted_iota(jnp.int32, (1, LANES), 1)
    m1 = lane < HEAD_DIM
    t_i = lax.broadcasted_iota(jnp.int32, (L, L2), 0)
    s_i = lax.broadcasted_iota(jnp.int32, (L, L2), 1) & (L - 1)
    r2 = lax.broadcasted_iota(jnp.int32, (L2, L2), 0)
    c2 = lax.broadcasted_iota(jnp.int32, (L2, L2), 1)
    same = (r2 >= L) == (c2 >= L)
    tt = r2 & (L - 1)
    st = c2 & (L - 1)

    def stack(x):
        z = jnp.zeros_like(x)
        return jnp.concatenate([jnp.where(m1, x, z), jnp.where(m1, z, x)], axis=0)

    chains = []
    for rev, ins in ((False, (rf, vf, kkf, af, lwf, kdf)), (True, (rb, vb, kkb, ab, lwb, kdb))):
        if rev:
            tri2 = (s_i >= t_i).astype(BF16)
            strict = same & (st > tt)
            incl = same & (st >= tt)
        else:
            tri2 = (s_i <= t_i).astype(BF16)
            strict = same & (st < tt)
            incl = same & (st <= tt)
        r, vb16, kk, bn, lw, kd = (x[...] for x in ins)
        r, kk, bn, kd = (x.astype(F32) for x in (r, kk, bn, kd))
        c = _dot(tri2, jnp.concatenate(_split_bf16(lw), axis=0))
        c_last = c[0:1, :] if rev else c[L - 1:L, :]
        decay_last = jnp.exp(c_last)
        e_pos = jnp.exp(c)
        e_neg = jnp.exp(-c)
        e_rem = jnp.exp(c_last - c)
        a_t = (-kk * jnp.exp(c - lw)).astype(BF16)
        b_t = (bn * e_neg).astype(BF16)
        k_t = (kd * e_neg).astype(BF16)
        r_t = (r * e_pos).astype(BF16)
        b_h = (bn * e_rem).astype(BF16)
        k_h = (kd * e_rem).astype(BF16)
        for pr in range(npairs):
            sl = slice(pr * LANES, (pr + 1) * LANES)
            chains.append(dict(
                strict=strict, incl=incl, s_ref=s_ref.at[1 if rev else 0, pr], decay_last=decay_last[:, sl],
                ar=jnp.concatenate([stack(a_t[:, sl]), stack(r_t[:, sl])], axis=0),
                bk=jnp.concatenate([stack(b_t[:, sl]), stack(k_t[:, sl])], axis=0),
                bkh=jnp.concatenate([stack(b_h[:, sl]), stack(k_h[:, sl])], axis=0),
                v_s=stack(vb16[:, sl])))
    for ch in chains:
        ch["s0"] = ch["s_ref"][...]
        ch["sc"] = _dot_nt(ch["ar"], ch["bk"])
    for ch in chains:
        ch["from_state"] = _dot_nt(ch["ar"], ch["s0"].astype(BF16))
    for ch in chains:
        sc = ch["sc"]
        m_ak = jnp.where(ch["strict"], sc[0:L2, L2:], 0.0).astype(BF16)
        ch["m_r"] = jnp.concatenate([jnp.where(ch["incl"], sc[L2:, 0:L2], 0.0).astype(BF16),
                                     jnp.where(ch["incl"], sc[L2:, L2:], 0.0).astype(BF16)], axis=1)
        ch["p"] = jnp.where(ch["strict"], sc[0:L2, 0:L2], 0.0)
        ch["x"] = ch["from_state"][0:L2] + _dot(m_ak, ch["v_s"])
    n_steps = int(math.log2(L))
    for step in range(n_steps):
        for ch in chains:
            pb = ch["p"].astype(BF16)
            xb = ch["x"].astype(BF16)
            if step < n_steps - 1:
                res = _dot(pb, jnp.concatenate([xb, pb], axis=1))
                ch["x"] = ch["x"] + res[:, 0:LANES]
                ch["p"] = res[:, LANES:]
            else:
                ch["x"] = ch["x"] + _dot(pb, xb)
    for ch in chains:
        ch["uv"] = jnp.concatenate([ch["x"].astype(BF16), ch["v_s"]], axis=0)
        ch["y2"] = ch["from_state"][L2:] + _dot(ch["m_r"], ch["uv"])
    for ch in chains:
        ch["s_ref"][...] = ch["s0"] * ch["decay_last"] + _dot_tn(ch["uv"], ch["bkh"])
    for y_o, group in ((yf_o, chains[:npairs]), (yb_o, chains[npairs:])):
        y_o[...] = jnp.concatenate([ch["y2"][0:L, :] + ch["y2"][L:L2, :] for ch in group], axis=1).astype(y_o.dtype)


def rwkv_chunk(r, v, kk, b0, b1, lw0, lw1, kd0, kd1, batch, seq_len):
    c = C_RWKV
    L = CHUNK
    nc = seq_len // L

    def r3(x):
        return x.reshape(batch, seq_len, c)

    fwd = pl.BlockSpec((None, L, c), lambda b, t: (b, t, 0))
    bwd = pl.BlockSpec((None, L, c), lambda b, t: (b, nc - 1 - t, 0))
    sds = jax.ShapeDtypeStruct((batch, seq_len, c), BF16)
    yf, yb = pl.pallas_call(
        _chunk_kernel,
        grid=(batch, nc),
        in_specs=[fwd] * 6 + [bwd] * 6,
        out_specs=[fwd, bwd],
        out_shape=[sds, sds],
        scratch_shapes=[pltpu.VMEM((2, c // LANES, LANES, LANES), F32)],
        compiler_params=_cparams(("parallel", "arbitrary")),
        name="rwkv_chunk",
    )(r3(r), r3(v), r3(kk), r3(b0), r3(lw0), r3(kd0), r3(r), r3(v), r3(kk), r3(b1), r3(lw1), r3(kd1))
    return yf.reshape(-1, c), yb.reshape(-1, c)


def _post_kernel(yf, yb, g, bg, lng, lnb, o_ref):
    bd2 = _head_block_ones()
    y = yf[...].astype(F32) + yb[...].astype(F32)
    mu = _head_sums(y, bd2) * (1.0 / HEAD_DIM)
    d = y - mu
    var = _head_sums(d * d, bd2) * (1.0 / HEAD_DIM)
    yn = d * lax.rsqrt(var + GN_EPS) * lng[...] + lnb[...]
    o_ref[...] = (yn * g[...].astype(F32) + bg[...].astype(F32)).astype(o_ref.dtype)


def rwkv_post(yf, yb, g, bg, ln_g, ln_b, tm=512):
    n, c = yf.shape
    blk = pl.BlockSpec((tm, c), lambda i: (i, 0))
    vec = pl.BlockSpec((1, c), lambda i: (0, 0))
    return pl.pallas_call(
        _post_kernel,
        grid=(n // tm,),
        in_specs=[blk] * 4 + [vec] * 2,
        out_specs=blk,
        out_shape=jax.ShapeDtypeStruct((n, c), BF16),
        compiler_params=_cparams(("parallel",)),
        name="rwkv_post",
    )(yf, yb, g, bg, ln_g.reshape(1, c), ln_b.reshape(1, c))


def _attn_kernel(q_ref, kp, kc, kn, vp, vc, vn, bias_ref, sink_ref, gain_ref, o_ref):
    n = pl.program_id(1)
    nb = pl.num_programs(1)
    i = lax.broadcasted_iota(jnp.int32, (BLOCK, 3 * BLOCK), 0)
    j = lax.broadcasted_iota(jnp.int32, (BLOCK, 3 * BLOCK), 1)
    rel = j - BLOCK - i
    valid = (jnp.abs(rel) <= WINDOW) & ((j >= BLOCK) | (n > 0)) & ((j < 2 * BLOCK) | (n < nb - 1))
    kb = jnp.concatenate([kp[...], kc[...], kn[...]], axis=0).astype(BF16)
    vb = jnp.concatenate([vp[...], vc[...], vn[...]], axis=0).astype(BF16)
    q = (q_ref[...] * (HEAD_DIM ** -0.5)).astype(BF16)
    group = ATT_Q_HEADS // ATT_KV_HEADS
    valid_g = jnp.concatenate([valid] * group, axis=0)
    row_head = lax.broadcasted_iota(jnp.int32, (group * BLOCK, 1), 0) >> 7

    def scores(kh):
        qg = jnp.concatenate([q[:, (kh * group + gi) * HEAD_DIM:(kh * group + gi + 1) * HEAD_DIM]
                              for gi in range(group)], axis=0)
        return _dot_nt(qg, kb[:, kh * HEAD_DIM:(kh + 1) * HEAD_DIM])

    def softmax(kh, s):
        bias = bias_ref[kh * group:(kh + 1) * group].reshape(group * BLOCK, 3 * BLOCK)
        s = jnp.where(valid_g, s + bias, NEG_INF)
        sk = jnp.zeros((group * BLOCK, 1), F32)
        for gi in range(group):
            sk = jnp.where(row_head == gi, sink_ref[kh * group + gi], sk)
        m = jnp.maximum(jnp.max(s, axis=-1, keepdims=True), sk)
        p = jnp.exp(s - m)
        denom = jnp.sum(p, axis=-1, keepdims=True) + jnp.exp(sk - m)
        return p.astype(BF16), denom

    all_scores = [scores(kh) for kh in range(ATT_KV_HEADS)]
    probs = [softmax(kh, s) for kh, s in enumerate(all_scores)]
    outs = []
    for kh, (p, denom) in enumerate(probs):
        og = _dot(p, vb[:, kh * HEAD_DIM:(kh + 1) * HEAD_DIM]) / denom
        outs += [og[gi * BLOCK:(gi + 1) * BLOCK] for gi in range(group)]
    o = jnp.concatenate(outs, axis=-1)
    o_ref[...] = _rms(o, gain_ref[...]).astype(o_ref.dtype)


def window_attention(qkv, bias, sink, gain, batch, seq_len):
    n = qkv.shape[0]
    nb = seq_len // BLOCK
    kcol = ATT_WIDTH // ATT_KV_WIDTH
    vcol = kcol + 1

    def rows(off):
        def f(b, t):
            return jnp.clip(t + off, 0, nb - 1) + b * nb
        return f

    def kv_spec(col, off):
        f = rows(off)
        return pl.BlockSpec((BLOCK, ATT_KV_WIDTH), lambda b, t: (f(b, t), col))

    return pl.pallas_call(
        _attn_kernel,
        grid=(batch, nb),
        in_specs=[
            pl.BlockSpec((BLOCK, ATT_WIDTH), lambda b, t: (b * nb + t, 0)),
            kv_spec(kcol, -1), kv_spec(kcol, 0), kv_spec(kcol, 1),
            kv_spec(vcol, -1), kv_spec(vcol, 0), kv_spec(vcol, 1),
            pl.BlockSpec((ATT_Q_HEADS, BLOCK, 3 * BLOCK), lambda b, t: (0, 0, 0)),
            pl.BlockSpec(memory_space=pltpu.SMEM),
            pl.BlockSpec((1, ATT_WIDTH), lambda b, t: (0, 0)),
        ],
        out_specs=pl.BlockSpec((BLOCK, ATT_WIDTH), lambda b, t: (b * nb + t, 0)),
        out_shape=jax.ShapeDtypeStruct((n, ATT_WIDTH), BF16),
        compiler_params=_cparams(("parallel", "parallel")),
        name="window_attention",
    )(qkv, qkv, qkv, qkv, qkv, qkv, qkv, bias, sink, gain.reshape(1, ATT_WIDTH))


def _t5_bias(rel_bias_table):
    i = jnp.arange(BLOCK)[:, None]
    j = jnp.arange(3 * BLOCK)[None, :]
    rel = j - BLOCK - i
    nbk = N_BUCKETS // 2
    max_exact = nbk // 2
    ret = jnp.where(rel > 0, nbk, 0)
    nabs = jnp.abs(rel)
    large = max_exact + (jnp.log(jnp.maximum(nabs, max_exact).astype(jnp.float32) / max_exact)
                         / math.log(MAX_DISTANCE / max_exact) * (nbk - max_exact)).astype(jnp.int32)
    large = jnp.minimum(large, nbk - 1)
    bucket = ret + jnp.where(nabs < max_exact, nabs, large)
    onehot = (bucket[None, :, :] == jnp.arange(N_BUCKETS)[:, None, None]).astype(F32)
    return jnp.einsum("bh,bij->hij", rel_bias_table.astype(F32), onehot, precision=HI)


def _outproj_kernel(yr, ya, w1, w2, h_ref, o_ref):
    o_ref[...] = h_ref[...] + _dot(yr[...], w1[...]) + _dot(ya[...], w2[...])


def out_proj(y_r, y_a, w1, w2, h, tm=512):
    n, d = h.shape
    c = y_r.shape[1]
    return pl.pallas_call(
        _outproj_kernel,
        grid=(n // tm,),
        in_specs=[
            pl.BlockSpec((tm, c), lambda i: (i, 0)),
            pl.BlockSpec((tm, c), lambda i: (i, 0)),
            _resident((c, d)),
            _resident((c, d)),
            pl.BlockSpec((tm, d), lambda i: (i, 0)),
        ],
        out_specs=pl.BlockSpec((tm, d), lambda i: (i, 0)),
        out_shape=jax.ShapeDtypeStruct((n, d), F32),
        compiler_params=_cparams(("parallel",)),
        name="out_proj",
    )(y_r, y_a, w1, w2, h)


def _xatt_kernel(h_ref, g_ref, wq, kv_ref, wo, gm_ref, wr_ref, br_ref, o_ref, slab_ref):
    h = h_ref[...]
    hn = _rms(h, g_ref[...]).astype(BF16)
    q = _dot(hn, wq[...])
    kv = kv_ref[...]
    scale = XATT_HEAD_DIM ** -0.5
    outs = []
    for hd in range(XATT_HEADS):
        sl = slice(hd * XATT_HEAD_DIM, (hd + 1) * XATT_HEAD_DIM)
        k_h = kv[:, sl].astype(BF16)
        v_h = kv[:, XATT_WIDTH + hd * XATT_HEAD_DIM: XATT_WIDTH + (hd + 1) * XATT_HEAD_DIM].astype(BF16)
        s = _dot_nt(q[:, sl].astype(BF16), k_h) * scale
        m = jnp.max(s, axis=-1, keepdims=True)
        p = jnp.exp(s - m)
        p = p / jnp.sum(p, axis=-1, keepdims=True)
        outs.append(_dot(p.astype(BF16), v_h))
    o = jnp.concatenate(outs, axis=-1).astype(BF16)
    h_new = h + _dot(o, wo[...])
    o_ref[...] = h_new
    slab_ref[...] = _route(h_new, gm_ref[...], wr_ref[...], br_ref[...])


def cross_attention_router(h, g, wq, kv, wo, g_moe, w_r, b_r, batch, seq_len, tm=256):
    n, d = h.shape
    m = kv.shape[0] // batch
    tiles = seq_len // tm
    return pl.pallas_call(
        _xatt_kernel,
        grid=(n // tm,),
        in_specs=[
            pl.BlockSpec((tm, d), lambda i: (i, 0)),
            _resident((1, d)),
            _resident((d, XATT_WIDTH)),
            pl.BlockSpec((m, 2 * XATT_WIDTH), lambda i: (i // tiles, 0)),
            _resident((XATT_WIDTH, d)),
            _resident((1, d)),
            _resident((d, 2 * LANES)),
            _resident((1, LANES)),
        ],
        out_specs=[pl.BlockSpec((tm, d), lambda i: (i, 0)), pl.BlockSpec((tm, LANES), lambda i: (i, 0))],
        out_shape=[jax.ShapeDtypeStruct((n, d), F32), jax.ShapeDtypeStruct((n, LANES), F32)],
        compiler_params=_cparams(("parallel",)),
        name="cross_attention_router",
    )(h, g.reshape(1, d), wq, kv, wo, g_moe.reshape(1, d), w_r, b_r)


def _route(h, g, w2, b):
    hn_hi, hn_lo = _split_bf16(_rms(h, g))
    both = _dot(hn_hi, w2) + _dot(hn_lo, w2)
    logits = both[:, 0:LANES] + both[:, LANES:] + b
    tm = logits.shape[0]
    lane = lax.broadcasted_iota(jnp.int32, (tm, LANES), 1)
    is_coarse = (lane >= N_EXPERTS) & (lane < N_EXPERTS + N_GROUPS)
    cl = jnp.where(is_coarse, logits, NEG_INF)
    cmax = jnp.max(cl, axis=-1, keepdims=True)
    csum = jnp.sum(jnp.where(is_coarse, jnp.exp(cl - cmax), 0.0), axis=-1, keepdims=True)
    p_g = 1.0 / csum
    lane_f = lane.astype(F32)
    grp_f = (lane >> 3).astype(F32)
    big = float(LANES)
    g_lane = jnp.min(jnp.where(is_coarse & (cl == cmax), lane_f, big), axis=-1, keepdims=True)
    g_idx = g_lane - float(N_EXPERTS)
    in_grp = (lane < N_EXPERTS) & (grp_f == g_idx)
    fl = jnp.where(in_grp, logits, NEG_INF)
    m1 = jnp.max(fl, axis=-1, keepdims=True)
    i1 = jnp.min(jnp.where(in_grp & (fl == m1), lane_f, big), axis=-1, keepdims=True)
    fl2 = jnp.where(lane_f == i1, NEG_INF, fl)
    m2 = jnp.max(fl2, axis=-1, keepdims=True)
    i2 = jnp.min(jnp.where(in_grp & (lane_f != i1) & (fl2 == m2), lane_f, big), axis=-1, keepdims=True)
    e2 = jnp.exp(m2 - m1)
    w1 = p_g / (1.0 + e2)
    w2 = p_g * e2 / (1.0 + e2)
    return jnp.where(lane == 0, i1,
                     jnp.where(lane == 1, i2,
                               jnp.where(lane == 2, w1, jnp.where(lane == 3, w2, 0.0))))


GATHER_UNROLL = 8


def _start_row_gather(src_hbm, dst, sem, idx_ref, first, stride, n_rows):
    def body(r, carry):
        row = idx_ref[first + r * stride]
        pltpu.make_async_copy(src_hbm.at[pl.ds(row, 1), :], dst.at[pl.ds(r, 1), :], sem).start()
        return carry

    lax.fori_loop(0, n_rows, body, 0, unroll=GATHER_UNROLL)


def _wait_row_gather(dst, sem):
    pltpu.make_async_copy(dst, dst, sem).wait()


def _expert_kernel(e0, te_ref, nxt_ref, par_ref, nused_ref, tok_ref, h_hbm, g_ref, wg_hbm, wu_hbm, wd_hbm, o_ref,
                   xbuf, wgf, wuf, wdf, wgb, wub, wdb, xsem, wsem):
    t = pl.program_id(0)
    nslots, tm = xbuf.shape[0], xbuf.shape[1]
    ahead = nslots - 1
    n_used = nused_ref[0]
    slot = lax.rem(t, nslots)

    def weight_copies(expert, ws):
        return [pltpu.make_async_copy(src.at[e0 + expert], dst.at[ws], wsem.at[ws, j])
                for j, (src, dst) in enumerate(((wg_hbm, wgf), (wu_hbm, wuf), (wd_hbm, wdf)))]

    @pl.when(t == 0)
    def _():
        for cp in weight_copies(te_ref[0], 0):
            cp.start()
        for j in range(ahead):
            @pl.when(j < n_used)
            def _():
                _start_row_gather(h_hbm, xbuf.at[j], xsem.at[j], tok_ref, j * tm, 1, tm)

    @pl.when(t + ahead < n_used)
    def _():
        nslot = lax.rem(t + ahead, nslots)
        _start_row_gather(h_hbm, xbuf.at[nslot], xsem.at[nslot], tok_ref, (t + ahead) * tm, 1, tm)

    @pl.when((t == 0) | (te_ref[t] != te_ref[jnp.maximum(t - 1, 0)]))
    def _():
        ws = par_ref[t]
        for cp in weight_copies(te_ref[t], ws):
            cp.wait()

        @pl.when(nxt_ref[t] >= 0)
        def _():
            for cp in weight_copies(nxt_ref[t], 1 - ws):
                cp.start()

        wgb[...] = wgf[ws].astype(BF16)
        wub[...] = wuf[ws].astype(BF16)
        wdb[...] = wdf[ws].astype(BF16)

    @pl.when(t < n_used)
    def _():
        _wait_row_gather(xbuf.at[slot], xsem.at[slot])
        xn = _rms(xbuf[slot], g_ref[...]).astype(BF16)
        hg = _dot(xn, wgb[...])
        hu = _dot(xn, wub[...])
        act = (hg * _sigmoid(hg) * hu).astype(BF16)
        o_ref[...] = _dot(act, wdb[...])

    @pl.when(t >= n_used)
    def _():
        o_ref[...] = jnp.zeros_like(o_ref)


def moe_experts(h, g, w_gate, w_up, w_down, layer, tables, n_tiles):
    n, d = h.shape
    tm = MOE_TILE
    tile_expert, next_expert, parity, n_used, row_tok = tables
    grid_spec = pltpu.PrefetchScalarGridSpec(
        num_scalar_prefetch=5,
        grid=(n_tiles,),
        in_specs=[
            pl.BlockSpec(memory_space=pl.ANY),
            pl.BlockSpec((1, d), lambda t, *_: (0, 0)),
            pl.BlockSpec(memory_space=pl.ANY),
            pl.BlockSpec(memory_space=pl.ANY),
            pl.BlockSpec(memory_space=pl.ANY),
        ],
        out_specs=pl.BlockSpec((tm, d), lambda t, *_: (t, 0)),
        scratch_shapes=[pltpu.VMEM((MOE_ROW_SLOTS, tm, d), F32),
                        pltpu.VMEM((2, d, D_EXPERT), F32), pltpu.VMEM((2, d, D_EXPERT), F32),
                        pltpu.VMEM((2, D_EXPERT, d), F32),
                        pltpu.VMEM((d, D_EXPERT), BF16), pltpu.VMEM((d, D_EXPERT), BF16),
                        pltpu.VMEM((D_EXPERT, d), BF16),
                        pltpu.SemaphoreType.DMA((MOE_ROW_SLOTS,)), pltpu.SemaphoreType.DMA((2, 3))],
    )
    return pl.pallas_call(
        functools.partial(_expert_kernel, layer * N_EXPERTS),
        grid_spec=grid_spec,
        out_shape=jax.ShapeDtypeStruct((n_tiles * tm, d), F32),
        compiler_params=_cparams(("arbitrary",)),
        name="moe_experts",
    )(tile_expert, next_expert, parity, n_used, row_tok, h, g.reshape(1, d), w_gate, w_up, w_down)


def _combine_kernel(final, pos_ref, y_hbm, h_ref, slab_ref, fg_ref, o_ref, buf, sem):
    i = pl.program_id(0)
    n_steps = pl.num_programs(0)
    tm = h_ref.shape[0]
    slot = i % 2

    def start(step, sl):
        for j in range(2):
            _start_row_gather(y_hbm, buf.at[sl, j], sem.at[sl, j], pos_ref, 2 * step * tm + j, 2, tm)

    @pl.when(i == 0)
    def _():
        start(0, 0)

    @pl.when(i + 1 < n_steps)
    def _():
        start(i + 1, 1 - slot)

    for j in range(2):
        _wait_row_gather(buf.at[slot, j], sem.at[slot, j])
    slab = slab_ref[...]
    out = h_ref[...] + slab[:, 2:3] * buf[slot, 0] + slab[:, 3:4] * buf[slot, 1]
    if final:
        out = _rms(out, fg_ref[...])
    o_ref[...] = out


def moe_combine(y_sorted, pos, h, slab, final_gain, final, tm=128):
    n, d = h.shape
    grid_spec = pltpu.PrefetchScalarGridSpec(
        num_scalar_prefetch=1,
        grid=(n // tm,),
        in_specs=[
            pl.BlockSpec(memory_space=pl.ANY),
            pl.BlockSpec((tm, d), lambda i, p: (i, 0)),
            pl.BlockSpec((tm, LANES), lambda i, p: (i, 0)),
            pl.BlockSpec((1, d), lambda i, p: (0, 0)),
        ],
        out_specs=pl.BlockSpec((tm, d), lambda i, p: (i, 0)),
        scratch_shapes=[pltpu.VMEM((2, 2, tm, d), F32), pltpu.SemaphoreType.DMA((2, 2))],
    )
    return pl.pallas_call(
        functools.partial(_combine_kernel, final),
        grid_spec=grid_spec,
        out_shape=jax.ShapeDtypeStruct((n, d), F32),
        compiler_params=_cparams(("arbitrary",)),
        name="moe_combine",
    )(pos, y_sorted, h, slab, final_gain.reshape(1, d))


def _routing_tables(slab, n_tiles):
    n = slab.shape[0]
    tm = MOE_TILE
    e = slab[:, 0:2].astype(jnp.int32).reshape(-1)
    onehot = (e[:, None] == jnp.arange(N_EXPERTS, dtype=jnp.int32)[None, :]).astype(jnp.int32)
    csum = jnp.cumsum(onehot, axis=0)
    counts = csum[-1]
    rank = jnp.take_along_axis(csum, e[:, None], axis=1)[:, 0] - 1
    tiles_per = (counts + tm - 1) // tm
    tile_end = jnp.cumsum(tiles_per)
    tile_start = tile_end - tiles_per
    pos = tile_start[e] * tm + rank
    row_tok = jnp.zeros((n_tiles * tm,), jnp.int32).at[pos].set(jnp.arange(2 * n, dtype=jnp.int32) // 2)
    n_used = tile_end[-1]
    t_idx = jnp.arange(n_tiles, dtype=jnp.int32)
    tile_expert = jnp.searchsorted(tile_end, jnp.minimum(t_idx, n_used - 1), side="right").astype(jnp.int32)
    tile_expert = jnp.minimum(tile_expert, N_EXPERTS - 1)
    after = tile_end[tile_expert]
    next_expert = jnp.where(after < n_used, tile_expert[jnp.minimum(after, n_tiles - 1)], -1).astype(jnp.int32)
    change = jnp.concatenate([jnp.zeros((1,), jnp.int32), (tile_expert[1:] != tile_expert[:-1]).astype(jnp.int32)])
    parity = (jnp.cumsum(change) % 2).astype(jnp.int32)
    tables = (tile_expert, next_expert, parity, n_used.reshape(1).astype(jnp.int32), row_tok)
    return tables, pos.astype(jnp.int32)


def _pad_rows(w, rows):
    return jnp.pad(w, ((0, rows - w.shape[0]), (0, 0)))


def _pad_cols(w, cols):
    return jnp.pad(w, ((0, 0), (0, cols - w.shape[1])))


def _split_in_weights(w):
    c = C_RWKV
    o = 3 * c
    wd0 = w[..., o:o + DECAY_RANK]
    wd1 = w[..., o + DECAY_RANK:o + 2 * DECAY_RANK]
    o += 2 * DECAY_RANK
    ad0 = w[..., o:o + ICLR_RANK]
    ad1 = w[..., o + ICLR_RANK:o + 2 * ICLR_RANK]
    o += 2 * ICLR_RANK
    gd = w[..., o:o + GATE_RANK]
    o += GATE_RANK
    att = w[..., o:]

    def padl(x):
        pad = [(0, 0)] * (x.ndim - 1) + [(0, RANK_PAD - x.shape[-1])]
        return jnp.pad(x, pad)

    rw = jnp.concatenate([w[..., :3 * c], padl(wd0), padl(wd1), padl(ad0), padl(ad1), gd], axis=-1)
    return rw, att


def kernel(x, mem, w_in, shift_prev, shift_next, decay_w0, decay_w2, iclr_a0, iclr_a2, gate_w2, vres_v0, vres_w1, vres_w2, k_k, k_a, r_k, ln_x_gain, ln_x_bias, att_sink, att_out_gain, rel_bias_table, w_out, norm_mix, norm_xatt, mem_norm, xatt_wq, xatt_wk, xatt_wv, xatt_wo, norm_moe, router_coarse_w, router_coarse_b, router_fine_w, router_fine_b, expert_w_gate, expert_w_up, expert_w_down, final_norm):
    batch, seq_len, d = x.shape
    depth = w_in.shape[0]
    n = batch * seq_len
    mem_len = mem.shape[1]
    h = x.reshape(n, d)
    memf = mem.reshape(batch * mem_len, d)
    bias = _t5_bias(rel_bias_table)
    n_tiles = (2 * n) // MOE_TILE + N_EXPERTS
    wg_all = expert_w_gate.reshape(depth * N_EXPERTS, d, D_EXPERT)
    wu_all = expert_w_up.reshape(depth * N_EXPERTS, d, D_EXPERT)
    wd_all = expert_w_down.reshape(depth * N_EXPERTS, D_EXPERT, d)
    v_first = None
    for l in range(depth):
        w_rw, w_at = _split_in_weights(w_in[l])
        mup, _ = _split_in_weights(shift_prev[l][None, :])
        mun, _ = _split_in_weights(shift_next[l][None, :])
        w2p = jnp.pad(decay_w2[l], ((0, 0), (0, RANK_PAD - DECAY_RANK), (0, 0))).astype(BF16)
        a2p = jnp.pad(iclr_a2[l], ((0, 0), (0, RANK_PAD - ICLR_RANK), (0, 0))).astype(BF16)
        if l == 0:
            vres = None
        else:
            vres = (vres_v0[l - 1], _pad_cols(vres_w1[l - 1], RANK_PAD).astype(BF16),
                    _pad_rows(vres_w2[l - 1], RANK_PAD).astype(BF16))
        u, qkv = in_proj(h, norm_mix[l], jnp.concatenate([w_rw, w_at], axis=1).astype(BF16), RW_COLS)
        r, v, kk, b0, b1, lw0, lw1, kd0, kd1, g, bg = rwkv_prep(
            u, seq_len, mup, mun, decay_w0[l], w2p, iclr_a0[l], a2p, gate_w2[l].astype(BF16),
            k_k[l], k_a[l], r_k[l], vres, v_first)
        if l == 0:
            v_first = v
        yf, yb = rwkv_chunk(r, v, kk, b0, b1, lw0, lw1, kd0, kd1, batch, seq_len)
        y_r = rwkv_post(yf, yb, g, bg, ln_x_gain[l], ln_x_bias[l])
        y_a = window_attention(qkv, bias, att_sink[l], att_out_gain[l], batch, seq_len)
        wo = w_out[l].astype(BF16)
        h = out_proj(y_r, y_a, wo[:C_RWKV], wo[C_RWKV:], h)
        wkv = jnp.concatenate([xatt_wk[l], xatt_wv[l]], axis=1).astype(BF16)
        kv = norm_matmul(memf, mem_norm[l], wkv, memf.shape[0], 2 * XATT_WIDTH)
        w_r = _pad_cols(jnp.concatenate([router_fine_w[l], router_coarse_w[l]], axis=1), LANES)
        b_r = _pad_cols(jnp.concatenate([router_fine_b[l].reshape(1, -1), router_coarse_b[l].reshape(1, -1)], axis=1), LANES)
        w_r2 = jnp.concatenate(_split_bf16(w_r), axis=1)
        h, slab = cross_attention_router(h, norm_xatt[l], xatt_wq[l].astype(BF16), kv, xatt_wo[l].astype(BF16),
                                         norm_moe[l], w_r2, b_r, batch, seq_len)
        tables, pos = _routing_tables(slab, n_tiles)
        y_sorted = moe_experts(h, norm_moe[l], wg_all, wu_all, wd_all, l, tables, n_tiles)
        h = moe_combine(y_sorted, pos, h, slab, final_norm, l == depth - 1)
    return h.reshape(batch, seq_len, d)
```

```python
import functools
import math

import jax
import jax.numpy as jnp
from jax import lax
from jax.experimental import pallas as pl
from jax.experimental.pallas import tpu as pltpu

F32 = jnp.float32
BF16 = jnp.bfloat16
HI = lax.Precision.HIGHEST

D_MODEL = 2048
RWKV_HEADS = 16
HEAD_DIM = 64
C_RWKV = RWKV_HEADS * HEAD_DIM
ATT_Q_HEADS = 16
ATT_KV_HEADS = 4
ATT_WIDTH = ATT_Q_HEADS * HEAD_DIM
ATT_KV_WIDTH = ATT_KV_HEADS * HEAD_DIM
WINDOW = 128
BLOCK = 128
DECAY_RANK = 96
ICLR_RANK = 96
VRES_RANK = 64
GATE_RANK = 256
GN_EPS = 64e-5
N_BUCKETS = 32
MAX_DISTANCE = 128
XATT_HEADS = 4
XATT_HEAD_DIM = 128
XATT_WIDTH = XATT_HEADS * XATT_HEAD_DIM
N_GROUPS = 4
EXPERTS_PER_GROUP = 8
N_EXPERTS = N_GROUPS * EXPERTS_PER_GROUP
D_EXPERT = 512
EPS = 1e-6
NEG_INF = -1e30

LANES = 128
RANK_PAD = 128
R0, K0, V0 = 0, C_RWKV, 2 * C_RWKV
WD0 = 3 * C_RWKV
WD1 = WD0 + RANK_PAD
AD0 = WD1 + RANK_PAD
AD1 = AD0 + RANK_PAD
GD0 = AD1 + RANK_PAD
RW_COLS = GD0 + GATE_RANK
AT_COLS = ATT_WIDTH + 2 * ATT_KV_WIDTH

CHUNK = 64
MOE_TILE = 256
MOE_ROW_SLOTS = 4
VMEM_LIMIT = 56 * 1024 * 1024


def _cparams(sem, vmem=VMEM_LIMIT):
    return pltpu.CompilerParams(dimension_semantics=sem, vmem_limit_bytes=vmem)


def _sigmoid(x):
    return 1.0 / (1.0 + jnp.exp(-x))


def _dot(a, b, prec=None):
    return jnp.dot(a, b, preferred_element_type=F32, precision=prec)


def _dot_nt(a, b, prec=None):
    return lax.dot_general(a, b, (((1,), (1,)), ((), ())), preferred_element_type=F32, precision=prec)


def _dot_tn(a, b, prec=None):
    return lax.dot_general(a, b, (((0,), (0,)), ((), ())), preferred_element_type=F32, precision=prec)


def _rms(x, g):
    ms = jnp.mean(x * x, axis=-1, keepdims=True)
    return x * lax.rsqrt(ms + EPS) * g


def _split_bf16(x):
    hi = x.astype(BF16)
    lo = (x - hi.astype(F32)).astype(BF16)
    return hi, lo


def _head_block_ones():
    ri = lax.broadcasted_iota(jnp.int32, (LANES, LANES), 0)
    ci = lax.broadcasted_iota(jnp.int32, (LANES, LANES), 1)
    return ((ri >> 6) == (ci >> 6)).astype(BF16)


def _head_sums(x, bd2):
    rows = x.shape[0]
    hi, lo = _split_bf16(x)
    outs = []
    for gi in range(x.shape[1] // LANES):
        sl = slice(gi * LANES, (gi + 1) * LANES)
        both = _dot(jnp.concatenate([hi[:, sl], lo[:, sl]], axis=0), bd2)
        outs.append(both[0:rows] + both[rows:])
    return outs[0] if len(outs) == 1 else jnp.concatenate(outs, axis=1)


def _layer_block(shape, layer, index=None):
    index = (0,) * len(shape) if index is None else index
    return pl.BlockSpec((None,) + tuple(shape), lambda *_: (layer,) + tuple(index), pipeline_mode=pl.Buffered(1))


def _norm_mm_kernel(x_ref, g_ref, w_ref, o_ref, xn_ref):
    @pl.when(pl.program_id(1) == 0)
    def _():
        xn_ref[...] = _rms(x_ref[...], g_ref[...]).astype(BF16)

    o_ref[...] = _dot(xn_ref[...], w_ref[...]).astype(o_ref.dtype)


def norm_matmul(x, g, w, layer, tm, tn, out_dtype=F32):
    n, d = x.shape
    nc = w.shape[2]
    return pl.pallas_call(
        _norm_mm_kernel,
        grid=(n // tm, nc // tn),
        in_specs=[
            pl.BlockSpec((tm, d), lambda i, j: (i, 0)),
            pl.BlockSpec((None, 1, d), lambda i, j: (layer, 0, 0)),
            pl.BlockSpec((None, d, tn), lambda i, j: (layer, 0, j)),
        ],
        out_specs=pl.BlockSpec((tm, tn), lambda i, j: (i, j)),
        out_shape=jax.ShapeDtypeStruct((n, nc), out_dtype),
        scratch_shapes=[pltpu.VMEM((tm, d), BF16)],
        compiler_params=_cparams(("parallel", "arbitrary")),
        name="norm_matmul",
    )(x, g, w)


def _in_proj_kernel(x_ref, g_ref, w_ref, u_ref, qkv_ref):
    xn = _rms(x_ref[...], g_ref[...]).astype(BF16)
    nu = u_ref.shape[1]
    u_ref[...] = _dot(xn, w_ref[:, 0:nu])
    qkv_ref[...] = _dot(xn, w_ref[:, nu:])


def in_proj(x, g, w, layer, n_rw, tm=256):
    n, d = x.shape
    nc = w.shape[2]
    return pl.pallas_call(
        _in_proj_kernel,
        grid=(n // tm,),
        in_specs=[
            pl.BlockSpec((tm, d), lambda i: (i, 0)),
            _layer_block((1, d), layer),
            _layer_block((d, nc), layer),
        ],
        out_specs=[pl.BlockSpec((tm, n_rw), lambda i: (i, 0)), pl.BlockSpec((tm, nc - n_rw), lambda i: (i, 0))],
        out_shape=[jax.ShapeDtypeStruct((n, n_rw), F32), jax.ShapeDtypeStruct((n, nc - n_rw), F32)],
        compiler_params=_cparams(("parallel",)),
        name="in_proj",
    )(x, g, w)


def _prep_kernel(seq_tiles, has_vres, *refs):
    if has_vres:
        (u_ref, hp_ref, hn_ref, mup_ref, mun_ref, w0_ref, w2_ref, a0_ref, a2_ref, gw_ref, kk_ref, ka_ref, rk_ref,
         vf_ref, v0_ref, v1_ref, v2_ref,
         r_o, v_o, kk_o, b0_o, b1_o, lw0_o, lw1_o, kd0_o, kd1_o, g_o, bg_o) = refs
    else:
        (u_ref, hp_ref, hn_ref, mup_ref, mun_ref, w0_ref, w2_ref, a0_ref, a2_ref, gw_ref, kk_ref, ka_ref, rk_ref,
         r_o, v_o, kk_o, b0_o, b1_o, lw0_o, lw1_o, kd0_o, kd1_o, g_o, bg_o) = refs
    tm = u_ref.shape[0]
    i = pl.program_id(0)
    it = i % seq_tiles
    has_prev = jnp.where(it != 0, 1.0, 0.0)
    has_next = jnp.where(it != seq_tiles - 1, 1.0, 0.0)
    row = lax.broadcasted_iota(jnp.int32, (tm, 1), 0)

    def shifted(c0, c1):
        u = u_ref[:, c0:c1]
        pr = hp_ref[7:8, c0:c1] * has_prev
        nx = hn_ref[0:1, c0:c1] * has_next
        prev = jnp.where(row == 0, pr, pltpu.roll(u, 1, 0))
        nxt = jnp.where(row == tm - 1, nx, pltpu.roll(u, tm - 1, 0))
        return u + mup_ref[:, c0:c1] * (prev - u) + mun_ref[:, c0:c1] * (nxt - u)

    bd2 = _head_block_ones()

    r = shifted(R0, R0 + C_RWKV)
    k = shifted(K0, K0 + C_RWKV)
    v = shifted(V0, V0 + C_RWKV)
    r_o[...] = r.astype(r_o.dtype)
    if has_vres:
        low = _dot(v.astype(BF16), v1_ref[...])
        mix = _sigmoid(v0_ref[...] + _dot(low.astype(BF16), v2_ref[...]))
        v = v + (vf_ref[...].astype(F32) - v) * mix
    v_o[...] = v.astype(v_o.dtype)
    kkraw = k * kk_ref[...]
    kk = kkraw / jnp.maximum(jnp.sqrt(_head_sums(kkraw * kkraw, bd2)), 1e-12)
    kk_o[...] = kk.astype(kk_o.dtype)
    ka = ka_ref[...]
    lw_scale = -math.exp(-0.5)
    kd_sum = None
    for z, (wc, ac, lw_o, b_o, kd_o) in enumerate(
            ((WD0, AD0, lw0_o, b0_o, kd0_o), (WD1, AD1, lw1_o, b1_o, kd1_o))):
        wd = shifted(wc, wc + RANK_PAD)
        w_pre = w0_ref[z:z + 1, :] + _dot(jnp.tanh(wd).astype(BF16), w2_ref[z])
        lw_o[...] = lw_scale * _sigmoid(w_pre)
        ad = shifted(ac, ac + RANK_PAD)
        a = _sigmoid(a0_ref[z:z + 1, :] + _dot(ad.astype(BF16), a2_ref[z]))
        b_o[...] = (kk * a).astype(b_o.dtype)
        kd = k * (1.0 + (a - 1.0) * ka)
        kd_o[...] = kd.astype(kd_o.dtype)
        kd_sum = kd if kd_sum is None else kd_sum + kd
    gd = shifted(GD0, GD0 + GATE_RANK)
    g = _dot(_sigmoid(gd).astype(BF16), gw_ref[...])
    g_o[...] = g.astype(g_o.dtype)
    bonus = _head_sums(r * kd_sum * rk_ref[...], bd2) * v
    bg_o[...] = (bonus * g).astype(bg_o.dtype)


def rwkv_prep(u, seq_len, layer, params, vres, v_first, tm=256):
    n = u.shape[0]
    seq_tiles = seq_len // tm
    hb = tm // 8
    nblk8 = n // 8
    has_vres = vres is not None
    c = C_RWKV
    in_specs = [
        pl.BlockSpec((tm, RW_COLS), lambda i: (i, 0)),
        pl.BlockSpec((8, RW_COLS), lambda i: (jnp.maximum(i * hb - 1, 0), 0)),
        pl.BlockSpec((8, RW_COLS), lambda i: (jnp.minimum((i + 1) * hb, nblk8 - 1), 0)),
    ] + [_layer_block(p.shape[1:], layer) for p in params]
    args = [u, u, u] + list(params)
    if has_vres:
        in_specs += [pl.BlockSpec((tm, c), lambda i: (i, 0))] + [_layer_block(p.shape[1:], layer - 1) for p in vres]
        args += [v_first] + list(vres)
    out_spec = pl.BlockSpec((tm, c), lambda i: (i, 0))
    dtypes = [BF16] * 5 + [F32] * 2 + [BF16] * 4
    return pl.pallas_call(
        functools.partial(_prep_kernel, seq_tiles, has_vres),
        grid=(n // tm,),
        in_specs=in_specs,
        out_specs=[out_spec] * len(dtypes),
        out_shape=[jax.ShapeDtypeStruct((n, c), dt) for dt in dtypes],
        compiler_params=_cparams(("parallel",)),
        name="rwkv_prep",
    )(*args)


def _chunk_kernel(rf, vf, kkf, af, lwf, kdf, rb, vb, kkb, ab, lwb, kdb, yf_o, yb_o, s_ref):
    L, width = rf.shape
    L2 = 2 * L
    npairs = width // LANES

    @pl.when(pl.program_id(1) == 0)
    def _():
        s_ref[...] = jnp.zeros_like(s_ref)

    lane = lax.broadcasted_iota(jnp.int32, (1, LANES), 1)
    m1 = lane < HEAD_DIM
    t_i = lax.broadcasted_iota(jnp.int32, (L, L2), 0)
    s_i = lax.broadcasted_iota(jnp.int32, (L, L2), 1) & (L - 1)
    r2 = lax.broadcasted_iota(jnp.int32, (L2, L2), 0)
    c2 = lax.broadcasted_iota(jnp.int32, (L2, L2), 1)
    same = (r2 >= L) == (c2 >= L)
    tt = r2 & (L - 1)
    st = c2 & (L - 1)

    def stack(x):
        z = jnp.zeros_like(x)
        return jnp.concatenate([jnp.where(m1, x, z), jnp.where(m1, z, x)], axis=0)

    chains = []
    for rev, ins in ((False, (rf, vf, kkf, af, lwf, kdf)), (True, (rb, vb, kkb, ab, lwb, kdb))):
        if rev:
            tri2 = (s_i >= t_i).astype(BF16)
            strict = same & (st > tt)
            incl = same & (st >= tt)
        else:
            tri2 = (s_i <= t_i).astype(BF16)
            strict = same & (st < tt)
            incl = same & (st <= tt)
        r, vb16, kk, bn, lw, kd = (x[...] for x in ins)
        r, kk, bn, kd = (x.astype(F32) for x in (r, kk, bn, kd))
        c = _dot(tri2, jnp.concatenate(_split_bf16(lw), axis=0))
        c_last = c[0:1, :] if rev else c[L - 1:L, :]
        decay_last = jnp.exp(c_last)
        e_pos = jnp.exp(c)
        e_neg = jnp.exp(-c)
        e_rem = jnp.exp(c_last - c)
        a_t = (-kk * jnp.exp(c - lw)).astype(BF16)
        b_t = (bn * e_neg).astype(BF16)
        k_t = (kd * e_neg).astype(BF16)
        r_t = (r * e_pos).astype(BF16)
        b_h = (bn * e_rem).astype(BF16)
        k_h = (kd * e_rem).astype(BF16)
        for pr in range(npairs):
            sl = slice(pr * LANES, (pr + 1) * LANES)
            chains.append(dict(
                strict=strict, incl=incl, s_ref=s_ref.at[1 if rev else 0, pr], decay_last=decay_last[:, sl],
                ar=jnp.concatenate([stack(a_t[:, sl]), stack(r_t[:, sl])], axis=0),
                bk=jnp.concatenate([stack(b_t[:, sl]), stack(k_t[:, sl])], axis=0),
                bkh=jnp.concatenate([stack(b_h[:, sl]), stack(k_h[:, sl])], axis=0),
                v_s=stack(vb16[:, sl])))
    for ch in chains:
        ch["s0"] = ch["s_ref"][...]
        ch["sc"] = _dot_nt(ch["ar"], ch["bk"])
    for ch in chains:
        ch["from_state"] = _dot_nt(ch["ar"], ch["s0"].astype(BF16))
    for ch in chains:
        sc = ch["sc"]
        m_ak = jnp.where(ch["strict"], sc[0:L2, L2:], 0.0).astype(BF16)
        ch["m_r"] = jnp.concatenate([jnp.where(ch["incl"], sc[L2:, 0:L2], 0.0).astype(BF16),
                                     jnp.where(ch["incl"], sc[L2:, L2:], 0.0).astype(BF16)], axis=1)
        ch["p"] = jnp.where(ch["strict"], sc[0:L2, 0:L2], 0.0)
        ch["x"] = ch["from_state"][0:L2] + _dot(m_ak, ch["v_s"])
    n_steps = int(math.log2(L))
    for step in range(n_steps):
        for ch in chains:
            pb = ch["p"].astype(BF16)
            xb = ch["x"].astype(BF16)
            if step < n_steps - 1:
                res = _dot(pb, jnp.concatenate([xb, pb], axis=1))
                ch["x"] = ch["x"] + res[:, 0:LANES]
                ch["p"] = res[:, LANES:]
            else:
                ch["x"] = ch["x"] + _dot(pb, xb)
    for ch in chains:
        ch["uv"] = jnp.concatenate([ch["x"].astype(BF16), ch["v_s"]], axis=0)
        ch["y2"] = ch["from_state"][L2:] + _dot(ch["m_r"], ch["uv"])
    for ch in chains:
        ch["s_ref"][...] = ch["s0"] * ch["decay_last"] + _dot_tn(ch["uv"], ch["bkh"])
    for y_o, group in ((yf_o, chains[:npairs]), (yb_o, chains[npairs:])):
        y_o[...] = jnp.concatenate([ch["y2"][0:L, :] + ch["y2"][L:L2, :] for ch in group], axis=1).astype(y_o.dtype)


def rwkv_chunk(r, v, kk, b0, b1, lw0, lw1, kd0, kd1, batch, seq_len):
    c = C_RWKV
    L = CHUNK
    nc = seq_len // L

    def r3(x):
        return x.reshape(batch, seq_len, c)

    fwd = pl.BlockSpec((None, L, c), lambda b, t: (b, t, 0))
    bwd = pl.BlockSpec((None, L, c), lambda b, t: (b, nc - 1 - t, 0))
    sds = jax.ShapeDtypeStruct((batch, seq_len, c), BF16)
    yf, yb = pl.pallas_call(
        _chunk_kernel,
        grid=(batch, nc),
        in_specs=[fwd] * 6 + [bwd] * 6,
        out_specs=[fwd, bwd],
        out_shape=[sds, sds],
        scratch_shapes=[pltpu.VMEM((2, c // LANES, LANES, LANES), F32)],
        compiler_params=_cparams(("parallel", "arbitrary")),
        name="rwkv_chunk",
    )(r3(r), r3(v), r3(kk), r3(b0), r3(lw0), r3(kd0), r3(r), r3(v), r3(kk), r3(b1), r3(lw1), r3(kd1))
    return yf.reshape(-1, c), yb.reshape(-1, c)


def _post_kernel(yf, yb, g, bg, lng, lnb, o_ref):
    bd2 = _head_block_ones()
    y = yf[...].astype(F32) + yb[...].astype(F32)
    mu = _head_sums(y, bd2) * (1.0 / HEAD_DIM)
    d = y - mu
    var = _head_sums(d * d, bd2) * (1.0 / HEAD_DIM)
    yn = d * lax.rsqrt(var + GN_EPS) * lng[...] + lnb[...]
    o_ref[...] = (yn * g[...].astype(F32) + bg[...].astype(F32)).astype(o_ref.dtype)


def rwkv_post(yf, yb, g, bg, ln_g, ln_b, layer, tm=512):
    n, c = yf.shape
    blk = pl.BlockSpec((tm, c), lambda i: (i, 0))
    vec = _layer_block((1, c), layer)
    return pl.pallas_call(
        _post_kernel,
        grid=(n // tm,),
        in_specs=[blk] * 4 + [vec] * 2,
        out_specs=blk,
        out_shape=jax.ShapeDtypeStruct((n, c), BF16),
        compiler_params=_cparams(("parallel",)),
        name="rwkv_post",
    )(yf, yb, g, bg, ln_g, ln_b)


def _attn_kernel(layer, q_ref, kp, kc, kn, vp, vc, vn, bias_ref, sink_ref, gain_ref, o_ref):
    n = pl.program_id(1)
    nb = pl.num_programs(1)
    i = lax.broadcasted_iota(jnp.int32, (BLOCK, 3 * BLOCK), 0)
    j = lax.broadcasted_iota(jnp.int32, (BLOCK, 3 * BLOCK), 1)
    rel = j - BLOCK - i
    valid = (jnp.abs(rel) <= WINDOW) & ((j >= BLOCK) | (n > 0)) & ((j < 2 * BLOCK) | (n < nb - 1))
    kb = jnp.concatenate([kp[...], kc[...], kn[...]], axis=0).astype(BF16)
    vb = jnp.concatenate([vp[...], vc[...], vn[...]], axis=0).astype(BF16)
    q = (q_ref[...] * (HEAD_DIM ** -0.5)).astype(BF16)
    group = ATT_Q_HEADS // ATT_KV_HEADS
    valid_g = jnp.concatenate([valid] * group, axis=0)
    row_head = lax.broadcasted_iota(jnp.int32, (group * BLOCK, 1), 0) >> 7

    def scores(kh):
        qg = jnp.concatenate([q[:, (kh * group + gi) * HEAD_DIM:(kh * group + gi + 1) * HEAD_DIM]
                              for gi in range(group)], axis=0)
        return _dot_nt(qg, kb[:, kh * HEAD_DIM:(kh + 1) * HEAD_DIM])

    def softmax(kh, s):
        bias = bias_ref[kh * group:(kh + 1) * group].reshape(group * BLOCK, 3 * BLOCK)
        s = jnp.where(valid_g, s + bias, NEG_INF)
        sk = jnp.zeros((group * BLOCK, 1), F32)
        for gi in range(group):
            sk = jnp.where(row_head == gi, sink_ref[layer, kh * group + gi], sk)
        m = jnp.maximum(jnp.max(s, axis=-1, keepdims=True), sk)
        p = jnp.exp(s - m)
        denom = jnp.sum(p, axis=-1, keepdims=True) + jnp.exp(sk - m)
        return p.astype(BF16), denom

    all_scores = [scores(kh) for kh in range(ATT_KV_HEADS)]
    probs = [softmax(kh, s) for kh, s in enumerate(all_scores)]
    outs = []
    for kh, (p, denom) in enumerate(probs):
        og = _dot(p, vb[:, kh * HEAD_DIM:(kh + 1) * HEAD_DIM]) / denom
        outs += [og[gi * BLOCK:(gi + 1) * BLOCK] for gi in range(group)]
    o = jnp.concatenate(outs, axis=-1)
    o_ref[...] = _rms(o, gain_ref[...]).astype(o_ref.dtype)


def window_attention(qkv, bias, sink, gain, layer, batch, seq_len):
    n = qkv.shape[0]
    nb = seq_len // BLOCK
    kcol = ATT_WIDTH // ATT_KV_WIDTH
    vcol = kcol + 1

    def rows(off):
        def f(b, t):
            return jnp.clip(t + off, 0, nb - 1) + b * nb
        return f

    def kv_spec(col, off):
        f = rows(off)
        return pl.BlockSpec((BLOCK, ATT_KV_WIDTH), lambda b, t: (f(b, t), col))

    return pl.pallas_call(
        functools.partial(_attn_kernel, layer),
        grid=(batch, nb),
        in_specs=[
            pl.BlockSpec((BLOCK, ATT_WIDTH), lambda b, t: (b * nb + t, 0)),
            kv_spec(kcol, -1), kv_spec(kcol, 0), kv_spec(kcol, 1),
            kv_spec(vcol, -1), kv_spec(vcol, 0), kv_spec(vcol, 1),
            pl.BlockSpec((ATT_Q_HEADS, BLOCK, 3 * BLOCK), lambda b, t: (0, 0, 0)),
            pl.BlockSpec(memory_space=pltpu.SMEM),
            _layer_block((1, ATT_WIDTH), layer),
        ],
        out_specs=pl.BlockSpec((BLOCK, ATT_WIDTH), lambda b, t: (b * nb + t, 0)),
        out_shape=jax.ShapeDtypeStruct((n, ATT_WIDTH), BF16),
        compiler_params=_cparams(("parallel", "parallel")),
        name="window_attention",
    )(qkv, qkv, qkv, qkv, qkv, qkv, qkv, bias, sink, gain)


def _t5_bias(rel_bias_table):
    i = jnp.arange(BLOCK)[:, None]
    j = jnp.arange(3 * BLOCK)[None, :]
    rel = j - BLOCK - i
    nbk = N_BUCKETS // 2
    max_exact = nbk // 2
    ret = jnp.where(rel > 0, nbk, 0)
    nabs = jnp.abs(rel)
    large = max_exact + (jnp.log(jnp.maximum(nabs, max_exact).astype(jnp.float32) / max_exact)
                         / math.log(MAX_DISTANCE / max_exact) * (nbk - max_exact)).astype(jnp.int32)
    large = jnp.minimum(large, nbk - 1)
    bucket = ret + jnp.where(nabs < max_exact, nabs, large)
    onehot = (bucket[None, :, :] == jnp.arange(N_BUCKETS)[:, None, None]).astype(F32)
    return jnp.einsum("bh,bij->hij", rel_bias_table.astype(F32), onehot, precision=HI)


def _outproj_kernel(yr, ya, w1, w2, h_ref, o_ref):
    o_ref[...] = h_ref[...] + _dot(yr[...], w1[...]) + _dot(ya[...], w2[...])


def out_proj(y_r, y_a, w, layer, h, tm=512):
    n, d = h.shape
    c = y_r.shape[1]
    return pl.pallas_call(
        _outproj_kernel,
        grid=(n // tm,),
        in_specs=[
            pl.BlockSpec((tm, c), lambda i: (i, 0)),
            pl.BlockSpec((tm, c), lambda i: (i, 0)),
            _layer_block((c, d), layer, (0, 0)),
            _layer_block((c, d), layer, (1, 0)),
            pl.BlockSpec((tm, d), lambda i: (i, 0)),
        ],
        out_specs=pl.BlockSpec((tm, d), lambda i: (i, 0)),
        out_shape=jax.ShapeDtypeStruct((n, d), F32),
        compiler_params=_cparams(("parallel",)),
        name="out_proj",
    )(y_r, y_a, w, w, h)


def _xatt_kernel(h_ref, g_ref, wq, kv_ref, wo, gm_ref, wr_ref, br_ref, o_ref, slab_ref):
    h = h_ref[...]
    hn = _rms(h, g_ref[...]).astype(BF16)
    q = _dot(hn, wq[...])
    kv = kv_ref[...]
    scale = XATT_HEAD_DIM ** -0.5
    outs = []
    for hd in range(XATT_HEADS):
        sl = slice(hd * XATT_HEAD_DIM, (hd + 1) * XATT_HEAD_DIM)
        k_h = kv[:, sl].astype(BF16)
        v_h = kv[:, XATT_WIDTH + hd * XATT_HEAD_DIM: XATT_WIDTH + (hd + 1) * XATT_HEAD_DIM].astype(BF16)
        s = _dot_nt(q[:, sl].astype(BF16), k_h) * scale
        m = jnp.max(s, axis=-1, keepdims=True)
        p = jnp.exp(s - m)
        p = p / jnp.sum(p, axis=-1, keepdims=True)
        outs.append(_dot(p.astype(BF16), v_h))
    o = jnp.concatenate(outs, axis=-1).astype(BF16)
    h_new = h + _dot(o, wo[...])
    o_ref[...] = h_new
    slab_ref[...] = _route(h_new, gm_ref[...], wr_ref[...], br_ref[...])


def cross_attention_router(h, g, wq, kv, wo, g_moe, w_r, b_r, layer, batch, seq_len, tm=256):
    n, d = h.shape
    m = kv.shape[0] // batch
    tiles = seq_len // tm
    return pl.pallas_call(
        _xatt_kernel,
        grid=(n // tm,),
        in_specs=[
            pl.BlockSpec((tm, d), lambda i: (i, 0)),
            _layer_block((1, d), layer),
            _layer_block((d, XATT_WIDTH), layer),
            pl.BlockSpec((m, 2 * XATT_WIDTH), lambda i: (i // tiles, 0)),
            _layer_block((XATT_WIDTH, d), layer),
            _layer_block((1, d), layer),
            _layer_block((d, 2 * LANES), layer),
            _layer_block((1, LANES), layer),
        ],
        out_specs=[pl.BlockSpec((tm, d), lambda i: (i, 0)), pl.BlockSpec((tm, LANES), lambda i: (i, 0))],
        out_shape=[jax.ShapeDtypeStruct((n, d), F32), jax.ShapeDtypeStruct((n, LANES), F32)],
        compiler_params=_cparams(("parallel",)),
        name="cross_attention_router",
    )(h, g, wq, kv, wo, g_moe, w_r, b_r)


def _route(h, g, w2, b):
    hn_hi, hn_lo = _split_bf16(_rms(h, g))
    both = _dot(hn_hi, w2) + _dot(hn_lo, w2)
    logits = both[:, 0:LANES] + both[:, LANES:] + b
    tm = logits.shape[0]
    lane = lax.broadcasted_iota(jnp.int32, (tm, LANES), 1)
    is_coarse = (lane >= N_EXPERTS) & (lane < N_EXPERTS + N_GROUPS)
    cl = jnp.where(is_coarse, logits, NEG_INF)
    cmax = jnp.max(cl, axis=-1, keepdims=True)
    csum = jnp.sum(jnp.where(is_coarse, jnp.exp(cl - cmax), 0.0), axis=-1, keepdims=True)
    p_g = 1.0 / csum
    lane_f = lane.astype(F32)
    grp_f = (lane >> 3).astype(F32)
    big = float(LANES)
    g_lane = jnp.min(jnp.where(is_coarse & (cl == cmax), lane_f, big), axis=-1, keepdims=True)
    g_idx = g_lane - float(N_EXPERTS)
    in_grp = (lane < N_EXPERTS) & (grp_f == g_idx)
    fl = jnp.where(in_grp, logits, NEG_INF)
    m1 = jnp.max(fl, axis=-1, keepdims=True)
    i1 = jnp.min(jnp.where(in_grp & (fl == m1), lane_f, big), axis=-1, keepdims=True)
    fl2 = jnp.where(lane_f == i1, NEG_INF, fl)
    m2 = jnp.max(fl2, axis=-1, keepdims=True)
    i2 = jnp.min(jnp.where(in_grp & (lane_f != i1) & (fl2 == m2), lane_f, big), axis=-1, keepdims=True)
    e2 = jnp.exp(m2 - m1)
    w1 = p_g / (1.0 + e2)
    w2 = p_g * e2 / (1.0 + e2)
    return jnp.where(lane == 0, i1,
                     jnp.where(lane == 1, i2,
                               jnp.where(lane == 2, w1, jnp.where(lane == 3, w2, 0.0))))


GATHER_UNROLL = 8


def _start_row_gather(src_hbm, dst, sem, idx_ref, first, stride, n_rows):
    def body(r, carry):
        row = idx_ref[first + r * stride]
        pltpu.make_async_copy(src_hbm.at[pl.ds(row, 1), :], dst.at[pl.ds(r, 1), :], sem).start()
        return carry

    lax.fori_loop(0, n_rows, body, 0, unroll=GATHER_UNROLL)


def _wait_row_gather(dst, sem):
    pltpu.make_async_copy(dst, dst, sem).wait()


def _expert_kernel(e0, te_ref, nxt_ref, par_ref, nused_ref, tok_ref, h_hbm, g_ref, wg_hbm, wu_hbm, wd_hbm, o_ref,
                   xbuf, wgf, wuf, wdf, wgb, wub, wdb, xsem, wsem):
    t = pl.program_id(0)
    nslots, tm = xbuf.shape[0], xbuf.shape[1]
    ahead = nslots - 1
    n_used = nused_ref[0]
    slot = lax.rem(t, nslots)

    def weight_copies(expert, ws):
        return [pltpu.make_async_copy(src.at[e0 + expert], dst.at[ws], wsem.at[ws, j])
                for j, (src, dst) in enumerate(((wg_hbm, wgf), (wu_hbm, wuf), (wd_hbm, wdf)))]

    @pl.when(t == 0)
    def _():
        for cp in weight_copies(te_ref[0], 0):
            cp.start()
        for j in range(ahead):
            @pl.when(j < n_used)
            def _():
                _start_row_gather(h_hbm, xbuf.at[j], xsem.at[j], tok_ref, j * tm, 1, tm)

    @pl.when(t + ahead < n_used)
    def _():
        nslot = lax.rem(t + ahead, nslots)
        _start_row_gather(h_hbm, xbuf.at[nslot], xsem.at[nslot], tok_ref, (t + ahead) * tm, 1, tm)

    @pl.when((t == 0) | (te_ref[t] != te_ref[jnp.maximum(t - 1, 0)]))
    def _():
        ws = par_ref[t]
        for cp in weight_copies(te_ref[t], ws):
            cp.wait()

        @pl.when(nxt_ref[t] >= 0)
        def _():
            for cp in weight_copies(nxt_ref[t], 1 - ws):
                cp.start()

        wgb[...] = wgf[ws].astype(BF16)
        wub[...] = wuf[ws].astype(BF16)
        wdb[...] = wdf[ws].astype(BF16)

    @pl.when(t < n_used)
    def _():
        _wait_row_gather(xbuf.at[slot], xsem.at[slot])
        xn = _rms(xbuf[slot], g_ref[...]).astype(BF16)
        hg = _dot(xn, wgb[...])
        hu = _dot(xn, wub[...])
        act = (hg * _sigmoid(hg) * hu).astype(BF16)
        o_ref[...] = _dot(act, wdb[...])

    @pl.when(t >= n_used)
    def _():
        o_ref[...] = jnp.zeros_like(o_ref)


def moe_experts(h, g, w_gate, w_up, w_down, layer, tables, n_tiles):
    n, d = h.shape
    tm = MOE_TILE
    tile_expert, next_expert, parity, n_used, row_tok = tables
    grid_spec = pltpu.PrefetchScalarGridSpec(
        num_scalar_prefetch=5,
        grid=(n_tiles,),
        in_specs=[
            pl.BlockSpec(memory_space=pl.ANY),
            _layer_block((1, d), layer),
            pl.BlockSpec(memory_space=pl.ANY),
            pl.BlockSpec(memory_space=pl.ANY),
            pl.BlockSpec(memory_space=pl.ANY),
        ],
        out_specs=pl.BlockSpec((tm, d), lambda t, *_: (t, 0)),
        scratch_shapes=[pltpu.VMEM((MOE_ROW_SLOTS, tm, d), F32),
                        pltpu.VMEM((2, d, D_EXPERT), F32), pltpu.VMEM((2, d, D_EXPERT), F32),
                        pltpu.VMEM((2, D_EXPERT, d), F32),
                        pltpu.VMEM((d, D_EXPERT), BF16), pltpu.VMEM((d, D_EXPERT), BF16),
                        pltpu.VMEM((D_EXPERT, d), BF16),
                        pltpu.SemaphoreType.DMA((MOE_ROW_SLOTS,)), pltpu.SemaphoreType.DMA((2, 3))],
    )
    return pl.pallas_call(
        functools.partial(_expert_kernel, layer * N_EXPERTS),
        grid_spec=grid_spec,
        out_shape=jax.ShapeDtypeStruct((n_tiles * tm, d), F32),
        compiler_params=_cparams(("arbitrary",)),
        name="moe_experts",
    )(tile_expert, next_expert, parity, n_used, row_tok, h, g, w_gate, w_up, w_down)


def _combine_kernel(final, pos_ref, y_hbm, h_ref, slab_ref, fg_ref, o_ref, buf, sem):
    i = pl.program_id(0)
    n_steps = pl.num_programs(0)
    tm = h_ref.shape[0]
    slot = i % 2

    def start(step, sl):
        for j in range(2):
            _start_row_gather(y_hbm, buf.at[sl, j], sem.at[sl, j], pos_ref, 2 * step * tm + j, 2, tm)

    @pl.when(i == 0)
    def _():
        start(0, 0)

    @pl.when(i + 1 < n_steps)
    def _():
        start(i + 1, 1 - slot)

    for j in range(2):
        _wait_row_gather(buf.at[slot, j], sem.at[slot, j])
    slab = slab_ref[...]
    out = h_ref[...] + slab[:, 2:3] * buf[slot, 0] + slab[:, 3:4] * buf[slot, 1]
    if final:
        out = _rms(out, fg_ref[...])
    o_ref[...] = out


def moe_combine(y_sorted, pos, h, slab, final_gain, final, tm=128):
    n, d = h.shape
    grid_spec = pltpu.PrefetchScalarGridSpec(
        num_scalar_prefetch=1,
        grid=(n // tm,),
        in_specs=[
            pl.BlockSpec(memory_space=pl.ANY),
            pl.BlockSpec((tm, d), lambda i, p: (i, 0)),
            pl.BlockSpec((tm, LANES), lambda i, p: (i, 0)),
            pl.BlockSpec((1, d), lambda i, p: (0, 0)),
        ],
        out_specs=pl.BlockSpec((tm, d), lambda i, p: (i, 0)),
        scratch_shapes=[pltpu.VMEM((2, 2, tm, d), F32), pltpu.SemaphoreType.DMA((2, 2))],
    )
    return pl.pallas_call(
        functools.partial(_combine_kernel, final),
        grid_spec=grid_spec,
        out_shape=jax.ShapeDtypeStruct((n, d), F32),
        compiler_params=_cparams(("arbitrary",)),
        name="moe_combine",
    )(pos, y_sorted, h, slab, final_gain.reshape(1, d))


def _routing_tables(slab, n_tiles):
    n = slab.shape[0]
    tm = MOE_TILE
    e = slab[:, 0:2].astype(jnp.int32).reshape(-1)
    onehot = (e[:, None] == jnp.arange(N_EXPERTS, dtype=jnp.int32)[None, :]).astype(jnp.int32)
    csum = jnp.cumsum(onehot, axis=0)
    counts = csum[-1]
    rank = jnp.take_along_axis(csum, e[:, None], axis=1)[:, 0] - 1
    tiles_per = (counts + tm - 1) // tm
    tile_end = jnp.cumsum(tiles_per)
    tile_start = tile_end - tiles_per
    pos = tile_start[e] * tm + rank
    row_tok = jnp.zeros((n_tiles * tm,), jnp.int32).at[pos].set(jnp.arange(2 * n, dtype=jnp.int32) // 2)
    n_used = tile_end[-1]
    t_idx = jnp.arange(n_tiles, dtype=jnp.int32)
    tile_expert = jnp.searchsorted(tile_end, jnp.minimum(t_idx, n_used - 1), side="right").astype(jnp.int32)
    tile_expert = jnp.minimum(tile_expert, N_EXPERTS - 1)
    after = tile_end[tile_expert]
    next_expert = jnp.where(after < n_used, tile_expert[jnp.minimum(after, n_tiles - 1)], -1).astype(jnp.int32)
    change = jnp.concatenate([jnp.zeros((1,), jnp.int32), (tile_expert[1:] != tile_expert[:-1]).astype(jnp.int32)])
    parity = (jnp.cumsum(change) % 2).astype(jnp.int32)
    tables = (tile_expert, next_expert, parity, n_used.reshape(1).astype(jnp.int32), row_tok)
    return tables, pos.astype(jnp.int32)


def _split_in_weights(w):
    c = C_RWKV
    o = 3 * c
    wd0 = w[..., o:o + DECAY_RANK]
    wd1 = w[..., o + DECAY_RANK:o + 2 * DECAY_RANK]
    o += 2 * DECAY_RANK
    ad0 = w[..., o:o + ICLR_RANK]
    ad1 = w[..., o + ICLR_RANK:o + 2 * ICLR_RANK]
    o += 2 * ICLR_RANK
    gd = w[..., o:o + GATE_RANK]
    o += GATE_RANK
    att = w[..., o:]

    def padl(x):
        pad = [(0, 0)] * (x.ndim - 1) + [(0, RANK_PAD - x.shape[-1])]
        return jnp.pad(x, pad)

    rw = jnp.concatenate([w[..., :3 * c], padl(wd0), padl(wd1), padl(ad0), padl(ad1), gd], axis=-1)
    return rw, att


def kernel(x, mem, w_in, shift_prev, shift_next, decay_w0, decay_w2, iclr_a0, iclr_a2, gate_w2, vres_v0, vres_w1, vres_w2, k_k, k_a, r_k, ln_x_gain, ln_x_bias, att_sink, att_out_gain, rel_bias_table, w_out, norm_mix, norm_xatt, mem_norm, xatt_wq, xatt_wk, xatt_wv, xatt_wo, norm_moe, router_coarse_w, router_coarse_b, router_fine_w, router_fine_b, expert_w_gate, expert_w_up, expert_w_down, final_norm):
    batch, seq_len, d = x.shape
    depth = w_in.shape[0]
    n = batch * seq_len
    mem_len = mem.shape[1]
    h = x.reshape(n, d)
    memf = mem.reshape(batch * mem_len, d)
    bias = _t5_bias(rel_bias_table)
    n_tiles = (2 * n) // MOE_TILE + N_EXPERTS

    def row(p):
        return p[:, None, :]

    def pad_axis(p, axis, size):
        pad = [(0, 0)] * p.ndim
        pad[axis] = (0, size - p.shape[axis])
        return jnp.pad(p, pad)

    w_rw, w_at = _split_in_weights(w_in)
    w_in_all = jnp.concatenate([w_rw, w_at], axis=-1).astype(BF16)
    prep_params = (
        row(_split_in_weights(shift_prev)[0]), row(_split_in_weights(shift_next)[0]),
        decay_w0, pad_axis(decay_w2, 2, RANK_PAD).astype(BF16),
        iclr_a0, pad_axis(iclr_a2, 2, RANK_PAD).astype(BF16),
        gate_w2.astype(BF16), row(k_k), row(k_a), row(r_k))
    vres_params = (row(vres_v0), pad_axis(vres_w1, 2, RANK_PAD).astype(BF16),
                   pad_axis(vres_w2, 1, RANK_PAD).astype(BF16))
    ln_g, ln_b = row(ln_x_gain), row(ln_x_bias)
    att_gain = row(att_out_gain)
    w_out_all = w_out.astype(BF16)
    g_mix, g_xatt, g_mem, g_moe = row(norm_mix), row(norm_xatt), row(mem_norm), row(norm_moe)
    wq_all = xatt_wq.astype(BF16)
    wkv_all = jnp.concatenate([xatt_wk, xatt_wv], axis=-1).astype(BF16)
    wo_all = xatt_wo.astype(BF16)
    w_r = pad_axis(jnp.concatenate([router_fine_w, router_coarse_w], axis=-1), 2, LANES)
    w_r2_all = jnp.concatenate(_split_bf16(w_r), axis=-1)
    b_r_all = row(pad_axis(jnp.concatenate([router_fine_b.reshape(depth, -1), router_coarse_b], axis=-1), 1, LANES))
    wg_all = expert_w_gate.reshape(depth * N_EXPERTS, d, D_EXPERT)
    wu_all = expert_w_up.reshape(depth * N_EXPERTS, d, D_EXPERT)
    wd_all = expert_w_down.reshape(depth * N_EXPERTS, D_EXPERT, d)

    v_first = None
    for l in range(depth):
        u, qkv = in_proj(h, g_mix, w_in_all, l, RW_COLS)
        r, v, kk, b0, b1, lw0, lw1, kd0, kd1, g, bg = rwkv_prep(
            u, seq_len, l, prep_params, vres_params if l > 0 else None, v_first)
        if l == 0:
            v_first = v
        yf, yb = rwkv_chunk(r, v, kk, b0, b1, lw0, lw1, kd0, kd1, batch, seq_len)
        y_r = rwkv_post(yf, yb, g, bg, ln_g, ln_b, l)
        y_a = window_attention(qkv, bias, att_sink, att_gain, l, batch, seq_len)
        h = out_proj(y_r, y_a, w_out_all, l, h)
        kv = norm_matmul(memf, g_mem, wkv_all, l, memf.shape[0], 2 * XATT_WIDTH)
        h, slab = cross_attention_router(h, g_xatt, wq_all, kv, wo_all, g_moe, w_r2_all, b_r_all, l, batch, seq_len)
        tables, pos = _routing_tables(slab, n_tiles)
        y_sorted = moe_experts(h, g_moe, wg_all, wu_all, wd_all, l, tables, n_tiles)
        h = moe_combine(y_sorted, pos, h, slab, final_norm, l == depth - 1)
    return h.reshape(batch, seq_len, d)
```

```python
import functools
import math

import jax
import jax.numpy as jnp
from jax import lax
from jax.experimental import pallas as pl
from jax.experimental.pallas import tpu as pltpu

F32 = jnp.float32
BF16 = jnp.bfloat16
HI = lax.Precision.HIGHEST

D_MODEL = 2048
RWKV_HEADS = 16
HEAD_DIM = 64
C_RWKV = RWKV_HEADS * HEAD_DIM
ATT_Q_HEADS = 16
ATT_KV_HEADS = 4
ATT_WIDTH = ATT_Q_HEADS * HEAD_DIM
ATT_KV_WIDTH = ATT_KV_HEADS * HEAD_DIM
WINDOW = 128
BLOCK = 128
DECAY_RANK = 96
ICLR_RANK = 96
VRES_RANK = 64
GATE_RANK = 256
GN_EPS = 64e-5
N_BUCKETS = 32
MAX_DISTANCE = 128
XATT_HEADS = 4
XATT_HEAD_DIM = 128
XATT_WIDTH = XATT_HEADS * XATT_HEAD_DIM
N_GROUPS = 4
EXPERTS_PER_GROUP = 8
N_EXPERTS = N_GROUPS * EXPERTS_PER_GROUP
D_EXPERT = 512
EPS = 1e-6
NEG_INF = -1e30

LANES = 128
RANK_PAD = 128
R0, K0, V0 = 0, C_RWKV, 2 * C_RWKV
WD0 = 3 * C_RWKV
WD1 = WD0 + RANK_PAD
AD0 = WD1 + RANK_PAD
AD1 = AD0 + RANK_PAD
GD0 = AD1 + RANK_PAD
RW_COLS = GD0 + GATE_RANK
AT_COLS = ATT_WIDTH + 2 * ATT_KV_WIDTH

CHUNK = 64
MOE_TILE = 256
MOE_ROW_SLOTS = 4
VMEM_LIMIT = 56 * 1024 * 1024


def _cparams(sem, vmem=VMEM_LIMIT):
    return pltpu.CompilerParams(dimension_semantics=sem, vmem_limit_bytes=vmem)


def _sigmoid(x):
    return 1.0 / (1.0 + jnp.exp(-x))


def _dot(a, b, prec=None):
    return jnp.dot(a, b, preferred_element_type=F32, precision=prec)


def _dot_nt(a, b, prec=None):
    return lax.dot_general(a, b, (((1,), (1,)), ((), ())), preferred_element_type=F32, precision=prec)


def _dot_tn(a, b, prec=None):
    return lax.dot_general(a, b, (((0,), (0,)), ((), ())), preferred_element_type=F32, precision=prec)


def _rms(x, g):
    ms = jnp.mean(x * x, axis=-1, keepdims=True)
    return x * lax.rsqrt(ms + EPS) * g


def _split_bf16(x):
    hi = x.astype(BF16)
    lo = (x - hi.astype(F32)).astype(BF16)
    return hi, lo


def _head_block_ones():
    ri = lax.broadcasted_iota(jnp.int32, (LANES, LANES), 0)
    ci = lax.broadcasted_iota(jnp.int32, (LANES, LANES), 1)
    return ((ri >> 6) == (ci >> 6)).astype(BF16)


def _head_sums(x, bd2):
    rows = x.shape[0]
    hi, lo = _split_bf16(x)
    outs = []
    for gi in range(x.shape[1] // LANES):
        sl = slice(gi * LANES, (gi + 1) * LANES)
        both = _dot(jnp.concatenate([hi[:, sl], lo[:, sl]], axis=0), bd2)
        outs.append(both[0:rows] + both[rows:])
    return outs[0] if len(outs) == 1 else jnp.concatenate(outs, axis=1)


def _layer_block(shape, layer, index=None):
    index = (0,) * len(shape) if index is None else index
    return pl.BlockSpec((None,) + tuple(shape), lambda *_: (layer,) + tuple(index), pipeline_mode=pl.Buffered(1))


def _norm_mm_kernel(x_ref, g_ref, w_ref, o_ref, xn_ref):
    @pl.when(pl.program_id(1) == 0)
    def _():
        xn_ref[...] = _rms(x_ref[...], g_ref[...]).astype(BF16)

    o_ref[...] = _dot(xn_ref[...], w_ref[...]).astype(o_ref.dtype)


def norm_matmul(x, g, w, layer, tm, tn, out_dtype=F32):
    n, d = x.shape
    nc = w.shape[2]
    return pl.pallas_call(
        _norm_mm_kernel,
        grid=(n // tm, nc // tn),
        in_specs=[
            pl.BlockSpec((tm, d), lambda i, j: (i, 0)),
            pl.BlockSpec((None, 1, d), lambda i, j: (layer, 0, 0)),
            pl.BlockSpec((None, d, tn), lambda i, j: (layer, 0, j)),
        ],
        out_specs=pl.BlockSpec((tm, tn), lambda i, j: (i, j)),
        out_shape=jax.ShapeDtypeStruct((n, nc), out_dtype),
        scratch_shapes=[pltpu.VMEM((tm, d), BF16)],
        compiler_params=_cparams(("parallel", "arbitrary")),
        name="norm_matmul",
    )(x, g, w)


def _in_proj_kernel(x_ref, g_ref, w_ref, u_ref, qkv_ref):
    xn = _rms(x_ref[...], g_ref[...]).astype(BF16)
    nu = u_ref.shape[1]
    u_ref[...] = _dot(xn, w_ref[:, 0:nu])
    qkv_ref[...] = _dot(xn, w_ref[:, nu:])


def in_proj(x, g, w, layer, n_rw, tm=256):
    n, d = x.shape
    nc = w.shape[2]
    return pl.pallas_call(
        _in_proj_kernel,
        grid=(n // tm,),
        in_specs=[
            pl.BlockSpec((tm, d), lambda i: (i, 0)),
            _layer_block((1, d), layer),
            _layer_block((d, nc), layer),
        ],
        out_specs=[pl.BlockSpec((tm, n_rw), lambda i: (i, 0)), pl.BlockSpec((tm, nc - n_rw), lambda i: (i, 0))],
        out_shape=[jax.ShapeDtypeStruct((n, n_rw), F32), jax.ShapeDtypeStruct((n, nc - n_rw), F32)],
        compiler_params=_cparams(("parallel",)),
        name="in_proj",
    )(x, g, w)


def _prep_kernel(seq_tiles, has_vres, *refs):
    if has_vres:
        (u_ref, hp_ref, hn_ref, mup_ref, mun_ref, w0_ref, w2_ref, a0_ref, a2_ref, gw_ref, kk_ref, ka_ref, rk_ref,
         vf_ref, v0_ref, v1_ref, v2_ref,
         r_o, v_o, kk_o, b0_o, b1_o, lw0_o, lw1_o, kd0_o, kd1_o, g_o, bg_o) = refs
    else:
        (u_ref, hp_ref, hn_ref, mup_ref, mun_ref, w0_ref, w2_ref, a0_ref, a2_ref, gw_ref, kk_ref, ka_ref, rk_ref,
         r_o, v_o, kk_o, b0_o, b1_o, lw0_o, lw1_o, kd0_o, kd1_o, g_o, bg_o) = refs
    tm = u_ref.shape[0]
    i = pl.program_id(0)
    it = i % seq_tiles
    has_prev = jnp.where(it != 0, 1.0, 0.0)
    has_next = jnp.where(it != seq_tiles - 1, 1.0, 0.0)
    row = lax.broadcasted_iota(jnp.int32, (tm, 1), 0)

    def shifted(c0, c1):
        u = u_ref[:, c0:c1]
        pr = hp_ref[7:8, c0:c1] * has_prev
        nx = hn_ref[0:1, c0:c1] * has_next
        prev = jnp.where(row == 0, pr, pltpu.roll(u, 1, 0))
        nxt = jnp.where(row == tm - 1, nx, pltpu.roll(u, tm - 1, 0))
        return u + mup_ref[:, c0:c1] * (prev - u) + mun_ref[:, c0:c1] * (nxt - u)

    bd2 = _head_block_ones()

    r = shifted(R0, R0 + C_RWKV)
    k = shifted(K0, K0 + C_RWKV)
    v = shifted(V0, V0 + C_RWKV)
    r_o[...] = r.astype(r_o.dtype)
    if has_vres:
        low = _dot(v.astype(BF16), v1_ref[...])
        mix = _sigmoid(v0_ref[...] + _dot(low.astype(BF16), v2_ref[...]))
        v = v + (vf_ref[...].astype(F32) - v) * mix
    v_o[...] = v.astype(v_o.dtype)
    kkraw = k * kk_ref[...]
    kk = kkraw / jnp.maximum(jnp.sqrt(_head_sums(kkraw * kkraw, bd2)), 1e-12)
    kk_o[...] = kk.astype(kk_o.dtype)
    ka = ka_ref[...]
    lw_scale = -math.exp(-0.5)
    kd_sum = None
    for z, (wc, ac, lw_o, b_o, kd_o) in enumerate(
            ((WD0, AD0, lw0_o, b0_o, kd0_o), (WD1, AD1, lw1_o, b1_o, kd1_o))):
        wd = shifted(wc, wc + RANK_PAD)
        w_pre = w0_ref[z:z + 1, :] + _dot(jnp.tanh(wd).astype(BF16), w2_ref[z])
        lw_o[...] = lw_scale * _sigmoid(w_pre)
        ad = shifted(ac, ac + RANK_PAD)
        a = _sigmoid(a0_ref[z:z + 1, :] + _dot(ad.astype(BF16), a2_ref[z]))
        b_o[...] = (kk * a).astype(b_o.dtype)
        kd = k * (1.0 + (a - 1.0) * ka)
        kd_o[...] = kd.astype(kd_o.dtype)
        kd_sum = kd if kd_sum is None else kd_sum + kd
    gd = shifted(GD0, GD0 + GATE_RANK)
    g = _dot(_sigmoid(gd).astype(BF16), gw_ref[...])
    g_o[...] = g.astype(g_o.dtype)
    bonus = _head_sums(r * kd_sum * rk_ref[...], bd2) * v
    bg_o[...] = (bonus * g).astype(bg_o.dtype)


def rwkv_prep(u, seq_len, layer, params, vres, v_first, tm=256):
    n = u.shape[0]
    seq_tiles = seq_len // tm
    hb = tm // 8
    nblk8 = n // 8
    has_vres = vres is not None
    c = C_RWKV
    in_specs = [
        pl.BlockSpec((tm, RW_COLS), lambda i: (i, 0)),
        pl.BlockSpec((8, RW_COLS), lambda i: (jnp.maximum(i * hb - 1, 0), 0)),
        pl.BlockSpec((8, RW_COLS), lambda i: (jnp.minimum((i + 1) * hb, nblk8 - 1), 0)),
    ] + [_layer_block(p.shape[1:], layer) for p in params]
    args = [u, u, u] + list(params)
    if has_vres:
        in_specs += [pl.BlockSpec((tm, c), lambda i: (i, 0))] + [_layer_block(p.shape[1:], layer - 1) for p in vres]
        args += [v_first] + list(vres)
    out_spec = pl.BlockSpec((tm, c), lambda i: (i, 0))
    dtypes = [BF16] * 5 + [F32] * 2 + [BF16] * 4
    return pl.pallas_call(
        functools.partial(_prep_kernel, seq_tiles, has_vres),
        grid=(n // tm,),
        in_specs=in_specs,
        out_specs=[out_spec] * len(dtypes),
        out_shape=[jax.ShapeDtypeStruct((n, c), dt) for dt in dtypes],
        compiler_params=_cparams(("parallel",)),
        name="rwkv_prep",
    )(*args)


def _chunk_kernel(rf, vf, kkf, af, lwf, kdf, rb, vb, kkb, ab, lwb, kdb, yf_o, yb_o, s_ref):
    L, width = rf.shape
    L2 = 2 * L
    npairs = width // LANES

    @pl.when(pl.program_id(1) == 0)
    def _():
        s_ref[...] = jnp.zeros_like(s_ref)

    lane = lax.broadcasted_iota(jnp.int32, (1, LANES), 1)
    m1 = lane < HEAD_DIM
    t_i = lax.broadcasted_iota(jnp.int32, (L, L2), 0)
    s_i = lax.broadcasted_iota(jnp.int32, (L, L2), 1) & (L - 1)
    ri = lax.broadcasted_iota(jnp.int32, (LANES, LANES), 0)
    ci = lax.broadcasted_iota(jnp.int32, (LANES, LANES), 1)
    same_head = (ri >> 6) == (ci >> 6)

    def stack(x):
        z = jnp.zeros_like(x)
        return jnp.concatenate([jnp.where(m1, x, z), jnp.where(m1, z, x)], axis=0)

    chains = []
    for rev, ins in ((False, (rf, vf, kkf, af, lwf, kdf)), (True, (rb, vb, kkb, ab, lwb, kdb))):
        strict = (s_i > t_i) if rev else (s_i < t_i)
        incl = (s_i >= t_i) if rev else (s_i <= t_i)
        incl4 = jnp.concatenate([incl, incl], axis=1)
        tri2 = incl.astype(BF16)
        r, vb16, kk, bn, lw, kd = (x[...] for x in ins)
        r, kk, bn, kd = (x.astype(F32) for x in (r, kk, bn, kd))
        c = _dot(tri2, jnp.concatenate(_split_bf16(lw), axis=0))
        c_last = c[0:1, :] if rev else c[L - 1:L, :]
        decay_last = jnp.exp(c_last)
        e_pos = jnp.exp(c)
        e_neg = jnp.exp(-c)
        e_rem = jnp.exp(c_last - c)
        a_t = (-kk * jnp.exp(c - lw)).astype(BF16)
        b_t = (bn * e_neg).astype(BF16)
        k_t = (kd * e_neg).astype(BF16)
        r_t = (r * e_pos).astype(BF16)
        b_h = (bn * e_rem).astype(BF16)
        k_h = (kd * e_rem).astype(BF16)
        for pr in range(npairs):
            sl = slice(pr * LANES, (pr + 1) * LANES)
            chains.append(dict(
                strict=strict, incl4=incl4, s_ref=s_ref.at[1 if rev else 0, pr], decay_last=decay_last[:, sl],
                ar=jnp.concatenate([a_t[:, sl], r_t[:, sl]], axis=0),
                bk=jnp.concatenate([stack(b_t[:, sl]), stack(k_t[:, sl])], axis=0),
                bkh=jnp.concatenate([b_h[:, sl], k_h[:, sl]], axis=0),
                v=vb16[:, sl], v_s=stack(vb16[:, sl])))
    for ch in chains:
        ch["s0"] = ch["s_ref"][...]
        ch["sc"] = _dot_nt(ch["ar"], ch["bk"])
    for ch in chains:
        ch["from_state"] = _dot_nt(ch["ar"], ch["s0"].astype(BF16))
    for ch in chains:
        sc = ch["sc"]
        m_ak = jnp.where(ch["strict"], sc[0:L, L2:], 0.0).astype(BF16)
        ch["m_r"] = jnp.where(ch["incl4"], sc[L:, :], 0.0).astype(BF16)
        ch["p"] = jnp.where(ch["strict"], sc[0:L, 0:L2], 0.0)
        ch["x"] = ch["from_state"][0:L] + _dot(m_ak, ch["v_s"])
    n_steps = int(math.log2(L))
    for step in range(n_steps):
        for ch in chains:
            pb = ch["p"].astype(BF16)
            xs = stack(ch["x"].astype(BF16))
            if step < n_steps - 1:
                res = _dot(pb, jnp.concatenate([xs, stack(pb)], axis=1))
                ch["x"] = ch["x"] + res[:, 0:LANES]
                ch["p"] = res[:, LANES:]
            else:
                ch["x"] = ch["x"] + _dot(pb, xs)
    for ch in chains:
        ub = ch["x"].astype(BF16)
        ch["ub"] = ub
        ch["y"] = ch["from_state"][L:] + _dot(ch["m_r"], jnp.concatenate([stack(ub), ch["v_s"]], axis=0))
    for ch in chains:
        upd = _dot_tn(jnp.concatenate([ch["ub"], ch["v"]], axis=0), ch["bkh"])
        ch["s_ref"][...] = ch["s0"] * ch["decay_last"] + jnp.where(same_head, upd, 0.0)
    for y_o, group in ((yf_o, chains[:npairs]), (yb_o, chains[npairs:])):
        y_o[...] = jnp.concatenate([ch["y"] for ch in group], axis=1).astype(y_o.dtype)


def rwkv_chunk(r, v, kk, b0, b1, lw0, lw1, kd0, kd1, batch, seq_len):
    c = C_RWKV
    L = CHUNK
    nc = seq_len // L

    def r3(x):
        return x.reshape(batch, seq_len, c)

    fwd = pl.BlockSpec((None, L, c), lambda b, t: (b, t, 0))
    bwd = pl.BlockSpec((None, L, c), lambda b, t: (b, nc - 1 - t, 0))
    sds = jax.ShapeDtypeStruct((batch, seq_len, c), BF16)
    yf, yb = pl.pallas_call(
        _chunk_kernel,
        grid=(batch, nc),
        in_specs=[fwd] * 6 + [bwd] * 6,
        out_specs=[fwd, bwd],
        out_shape=[sds, sds],
        scratch_shapes=[pltpu.VMEM((2, c // LANES, LANES, LANES), F32)],
        compiler_params=_cparams(("parallel", "arbitrary")),
        name="rwkv_chunk",
    )(r3(r), r3(v), r3(kk), r3(b0), r3(lw0), r3(kd0), r3(r), r3(v), r3(kk), r3(b1), r3(lw1), r3(kd1))
    return yf.reshape(-1, c), yb.reshape(-1, c)


def _post_kernel(yf, yb, g, bg, lng, lnb, o_ref):
    bd2 = _head_block_ones()
    y = yf[...].astype(F32) + yb[...].astype(F32)
    mu = _head_sums(y, bd2) * (1.0 / HEAD_DIM)
    d = y - mu
    var = _head_sums(d * d, bd2) * (1.0 / HEAD_DIM)
    yn = d * lax.rsqrt(var + GN_EPS) * lng[...] + lnb[...]
    o_ref[...] = (yn * g[...].astype(F32) + bg[...].astype(F32)).astype(o_ref.dtype)


def rwkv_post(yf, yb, g, bg, ln_g, ln_b, layer, tm=512):
    n, c = yf.shape
    blk = pl.BlockSpec((tm, c), lambda i: (i, 0))
    vec = _layer_block((1, c), layer)
    return pl.pallas_call(
        _post_kernel,
        grid=(n // tm,),
        in_specs=[blk] * 4 + [vec] * 2,
        out_specs=blk,
        out_shape=jax.ShapeDtypeStruct((n, c), BF16),
        compiler_params=_cparams(("parallel",)),
        name="rwkv_post",
    )(yf, yb, g, bg, ln_g, ln_b)


def _attn_kernel(layer, q_ref, kp, kc, kn, vp, vc, vn, bias_ref, sink_ref, gain_ref, o_ref):
    n = pl.program_id(1)
    nb = pl.num_programs(1)
    i = lax.broadcasted_iota(jnp.int32, (BLOCK, 3 * BLOCK), 0)
    j = lax.broadcasted_iota(jnp.int32, (BLOCK, 3 * BLOCK), 1)
    rel = j - BLOCK - i
    valid = (jnp.abs(rel) <= WINDOW) & ((j >= BLOCK) | (n > 0)) & ((j < 2 * BLOCK) | (n < nb - 1))
    kb = jnp.concatenate([kp[...], kc[...], kn[...]], axis=0).astype(BF16)
    vb = jnp.concatenate([vp[...], vc[...], vn[...]], axis=0).astype(BF16)
    q = (q_ref[...] * (HEAD_DIM ** -0.5)).astype(BF16)
    group = ATT_Q_HEADS // ATT_KV_HEADS
    valid_g = jnp.concatenate([valid] * group, axis=0)
    row_head = lax.broadcasted_iota(jnp.int32, (group * BLOCK, 1), 0) >> 7

    def scores(kh):
        qg = jnp.concatenate([q[:, (kh * group + gi) * HEAD_DIM:(kh * group + gi + 1) * HEAD_DIM]
                              for gi in range(group)], axis=0)
        return _dot_nt(qg, kb[:, kh * HEAD_DIM:(kh + 1) * HEAD_DIM])

    def softmax(kh, s):
        bias = bias_ref[kh * group:(kh + 1) * group].reshape(group * BLOCK, 3 * BLOCK)
        s = jnp.where(valid_g, s + bias, NEG_INF)
        sk = jnp.zeros((group * BLOCK, 1), F32)
        for gi in range(group):
            sk = jnp.where(row_head == gi, sink_ref[layer, kh * group + gi], sk)
        m = jnp.maximum(jnp.max(s, axis=-1, keepdims=True), sk)
        p = jnp.exp(s - m)
        denom = jnp.sum(p, axis=-1, keepdims=True) + jnp.exp(sk - m)
        return p.astype(BF16), denom

    all_scores = [scores(kh) for kh in range(ATT_KV_HEADS)]
    probs = [softmax(kh, s) for kh, s in enumerate(all_scores)]
    outs = []
    for kh, (p, denom) in enumerate(probs):
        og = _dot(p, vb[:, kh * HEAD_DIM:(kh + 1) * HEAD_DIM]) / denom
        outs += [og[gi * BLOCK:(gi + 1) * BLOCK] for gi in range(group)]
    o = jnp.concatenate(outs, axis=-1)
    o_ref[...] = _rms(o, gain_ref[...]).astype(o_ref.dtype)


def window_attention(qkv, bias, sink, gain, layer, batch, seq_len):
    n = qkv.shape[0]
    nb = seq_len // BLOCK
    kcol = ATT_WIDTH // ATT_KV_WIDTH
    vcol = kcol + 1

    def rows(off):
        def f(b, t):
            return jnp.clip(t + off, 0, nb - 1) + b * nb
        return f

    def kv_spec(col, off):
        f = rows(off)
        return pl.BlockSpec((BLOCK, ATT_KV_WIDTH), lambda b, t: (f(b, t), col))

    return pl.pallas_call(
        functools.partial(_attn_kernel, layer),
        grid=(batch, nb),
        in_specs=[
            pl.BlockSpec((BLOCK, ATT_WIDTH), lambda b, t: (b * nb + t, 0)),
            kv_spec(kcol, -1), kv_spec(kcol, 0), kv_spec(kcol, 1),
            kv_spec(vcol, -1), kv_spec(vcol, 0), kv_spec(vcol, 1),
            pl.BlockSpec((ATT_Q_HEADS, BLOCK, 3 * BLOCK), lambda b, t: (0, 0, 0)),
            pl.BlockSpec(memory_space=pltpu.SMEM),
            _layer_block((1, ATT_WIDTH), layer),
        ],
        out_specs=pl.BlockSpec((BLOCK, ATT_WIDTH), lambda b, t: (b * nb + t, 0)),
        out_shape=jax.ShapeDtypeStruct((n, ATT_WIDTH), BF16),
        compiler_params=_cparams(("parallel", "parallel")),
        name="window_attention",
    )(qkv, qkv, qkv, qkv, qkv, qkv, qkv, bias, sink, gain)


def _t5_bias(rel_bias_table):
    i = jnp.arange(BLOCK)[:, None]
    j = jnp.arange(3 * BLOCK)[None, :]
    rel = j - BLOCK - i
    nbk = N_BUCKETS // 2
    max_exact = nbk // 2
    ret = jnp.where(rel > 0, nbk, 0)
    nabs = jnp.abs(rel)
    large = max_exact + (jnp.log(jnp.maximum(nabs, max_exact).astype(jnp.float32) / max_exact)
                         / math.log(MAX_DISTANCE / max_exact) * (nbk - max_exact)).astype(jnp.int32)
    large = jnp.minimum(large, nbk - 1)
    bucket = ret + jnp.where(nabs < max_exact, nabs, large)
    onehot = (bucket[None, :, :] == jnp.arange(N_BUCKETS)[:, None, None]).astype(F32)
    return jnp.einsum("bh,bij->hij", rel_bias_table.astype(F32), onehot, precision=HI)


def _outproj_kernel(yr, ya, w1, w2, h_ref, o_ref):
    o_ref[...] = h_ref[...] + _dot(yr[...], w1[...]) + _dot(ya[...], w2[...])


def out_proj(y_r, y_a, w, layer, h, tm=512):
    n, d = h.shape
    c = y_r.shape[1]
    return pl.pallas_call(
        _outproj_kernel,
        grid=(n // tm,),
        in_specs=[
            pl.BlockSpec((tm, c), lambda i: (i, 0)),
            pl.BlockSpec((tm, c), lambda i: (i, 0)),
            _layer_block((c, d), layer, (0, 0)),
            _layer_block((c, d), layer, (1, 0)),
            pl.BlockSpec((tm, d), lambda i: (i, 0)),
        ],
        out_specs=pl.BlockSpec((tm, d), lambda i: (i, 0)),
        out_shape=jax.ShapeDtypeStruct((n, d), F32),
        compiler_params=_cparams(("parallel",)),
        name="out_proj",
    )(y_r, y_a, w, w, h)


def _xatt_kernel(h_ref, g_ref, wq, kv_ref, wo, gm_ref, wr_ref, br_ref, o_ref, slab_ref):
    h = h_ref[...]
    hn = _rms(h, g_ref[...]).astype(BF16)
    q = _dot(hn, wq[...])
    kv = kv_ref[...]
    scale = XATT_HEAD_DIM ** -0.5
    outs = []
    for hd in range(XATT_HEADS):
        sl = slice(hd * XATT_HEAD_DIM, (hd + 1) * XATT_HEAD_DIM)
        k_h = kv[:, sl].astype(BF16)
        v_h = kv[:, XATT_WIDTH + hd * XATT_HEAD_DIM: XATT_WIDTH + (hd + 1) * XATT_HEAD_DIM].astype(BF16)
        s = _dot_nt(q[:, sl].astype(BF16), k_h) * scale
        m = jnp.max(s, axis=-1, keepdims=True)
        p = jnp.exp(s - m)
        p = p / jnp.sum(p, axis=-1, keepdims=True)
        outs.append(_dot(p.astype(BF16), v_h))
    o = jnp.concatenate(outs, axis=-1).astype(BF16)
    h_new = h + _dot(o, wo[...])
    o_ref[...] = h_new
    slab_ref[...] = _route(h_new, gm_ref[...], wr_ref[...], br_ref[...])


def cross_attention_router(h, g, wq, kv, wo, g_moe, w_r, b_r, layer, batch, seq_len, tm=256):
    n, d = h.shape
    m = kv.shape[0] // batch
    tiles = seq_len // tm
    return pl.pallas_call(
        _xatt_kernel,
        grid=(n // tm,),
        in_specs=[
            pl.BlockSpec((tm, d), lambda i: (i, 0)),
            _layer_block((1, d), layer),
            _layer_block((d, XATT_WIDTH), layer),
            pl.BlockSpec((m, 2 * XATT_WIDTH), lambda i: (i // tiles, 0)),
            _layer_block((XATT_WIDTH, d), layer),
            _layer_block((1, d), layer),
            _layer_block((d, 2 * LANES), layer),
            _layer_block((1, LANES), layer),
        ],
        out_specs=[pl.BlockSpec((tm, d), lambda i: (i, 0)), pl.BlockSpec((tm, LANES), lambda i: (i, 0))],
        out_shape=[jax.ShapeDtypeStruct((n, d), F32), jax.ShapeDtypeStruct((n, LANES), F32)],
        compiler_params=_cparams(("parallel",)),
        name="cross_attention_router",
    )(h, g, wq, kv, wo, g_moe, w_r, b_r)


def _route(h, g, w2, b):
    hn_hi, hn_lo = _split_bf16(_rms(h, g))
    both = _dot(hn_hi, w2) + _dot(hn_lo, w2)
    logits = both[:, 0:LANES] + both[:, LANES:] + b
    tm = logits.shape[0]
    lane = lax.broadcasted_iota(jnp.int32, (tm, LANES), 1)
    is_coarse = (lane >= N_EXPERTS) & (lane < N_EXPERTS + N_GROUPS)
    cl = jnp.where(is_coarse, logits, NEG_INF)
    cmax = jnp.max(cl, axis=-1, keepdims=True)
    csum = jnp.sum(jnp.where(is_coarse, jnp.exp(cl - cmax), 0.0), axis=-1, keepdims=True)
    p_g = 1.0 / csum
    lane_f = lane.astype(F32)
    grp_f = (lane >> 3).astype(F32)
    big = float(LANES)
    g_lane = jnp.min(jnp.where(is_coarse & (cl == cmax), lane_f, big), axis=-1, keepdims=True)
    g_idx = g_lane - float(N_EXPERTS)
    in_grp = (lane < N_EXPERTS) & (grp_f == g_idx)
    fl = jnp.where(in_grp, logits, NEG_INF)
    m1 = jnp.max(fl, axis=-1, keepdims=True)
    i1 = jnp.min(jnp.where(in_grp & (fl == m1), lane_f, big), axis=-1, keepdims=True)
    fl2 = jnp.where(lane_f == i1, NEG_INF, fl)
    m2 = jnp.max(fl2, axis=-1, keepdims=True)
    i2 = jnp.min(jnp.where(in_grp & (lane_f != i1) & (fl2 == m2), lane_f, big), axis=-1, keepdims=True)
    e2 = jnp.exp(m2 - m1)
    w1 = p_g / (1.0 + e2)
    w2 = p_g * e2 / (1.0 + e2)
    return jnp.where(lane == 0, i1,
                     jnp.where(lane == 1, i2,
                               jnp.where(lane == 2, w1, jnp.where(lane == 3, w2, 0.0))))


GATHER_UNROLL = 8


def _start_row_gather(src_hbm, dst, sem, idx_ref, first, stride, n_rows):
    def body(r, carry):
        row = idx_ref[first + r * stride]
        pltpu.make_async_copy(src_hbm.at[pl.ds(row, 1), :], dst.at[pl.ds(r, 1), :], sem).start()
        return carry

    lax.fori_loop(0, n_rows, body, 0, unroll=GATHER_UNROLL)


def _wait_row_gather(dst, sem):
    pltpu.make_async_copy(dst, dst, sem).wait()


def _expert_kernel(e0, te_ref, nxt_ref, par_ref, nused_ref, tok_ref, h_hbm, g_ref, wg_hbm, wu_hbm, wd_hbm, o_ref,
                   xbuf, wgf, wuf, wdf, wgb, wub, wdb, xsem, wsem):
    t = pl.program_id(0)
    nslots, tm = xbuf.shape[0], xbuf.shape[1]
    ahead = nslots - 1
    n_used = nused_ref[0]
    slot = lax.rem(t, nslots)

    def weight_copies(expert, ws):
        return [pltpu.make_async_copy(src.at[e0 + expert], dst.at[ws], wsem.at[ws, j])
                for j, (src, dst) in enumerate(((wg_hbm, wgf), (wu_hbm, wuf), (wd_hbm, wdf)))]

    @pl.when(t == 0)
    def _():
        for cp in weight_copies(te_ref[0], 0):
            cp.start()
        for j in range(ahead):
            @pl.when(j < n_used)
            def _():
                _start_row_gather(h_hbm, xbuf.at[j], xsem.at[j], tok_ref, j * tm, 1, tm)

    @pl.when(t + ahead < n_used)
    def _():
        nslot = lax.rem(t + ahead, nslots)
        _start_row_gather(h_hbm, xbuf.at[nslot], xsem.at[nslot], tok_ref, (t + ahead) * tm, 1, tm)

    @pl.when((t == 0) | (te_ref[t] != te_ref[jnp.maximum(t - 1, 0)]))
    def _():
        ws = par_ref[t]
        for cp in weight_copies(te_ref[t], ws):
            cp.wait()

        @pl.when(nxt_ref[t] >= 0)
        def _():
            for cp in weight_copies(nxt_ref[t], 1 - ws):
                cp.start()

        wgb[...] = wgf[ws].astype(BF16)
        wub[...] = wuf[ws].astype(BF16)
        wdb[...] = wdf[ws].astype(BF16)

    @pl.when(t < n_used)
    def _():
        _wait_row_gather(xbuf.at[slot], xsem.at[slot])
        xn = _rms(xbuf[slot], g_ref[...]).astype(BF16)
        hg = _dot(xn, wgb[...])
        hu = _dot(xn, wub[...])
        act = (hg * _sigmoid(hg) * hu).astype(BF16)
        o_ref[...] = _dot(act, wdb[...])

    @pl.when(t >= n_used)
    def _():
        o_ref[...] = jnp.zeros_like(o_ref)


def moe_experts(h, g, w_gate, w_up, w_down, layer, tables, n_tiles):
    n, d = h.shape
    tm = MOE_TILE
    tile_expert, next_expert, parity, n_used, row_tok = tables
    grid_spec = pltpu.PrefetchScalarGridSpec(
        num_scalar_prefetch=5,
        grid=(n_tiles,),
        in_specs=[
            pl.BlockSpec(memory_space=pl.ANY),
            _layer_block((1, d), layer),
            pl.BlockSpec(memory_space=pl.ANY),
            pl.BlockSpec(memory_space=pl.ANY),
            pl.BlockSpec(memory_space=pl.ANY),
        ],
        out_specs=pl.BlockSpec((tm, d), lambda t, *_: (t, 0)),
        scratch_shapes=[pltpu.VMEM((MOE_ROW_SLOTS, tm, d), F32),
                        pltpu.VMEM((2, d, D_EXPERT), F32), pltpu.VMEM((2, d, D_EXPERT), F32),
                        pltpu.VMEM((2, D_EXPERT, d), F32),
                        pltpu.VMEM((d, D_EXPERT), BF16), pltpu.VMEM((d, D_EXPERT), BF16),
                        pltpu.VMEM((D_EXPERT, d), BF16),
                        pltpu.SemaphoreType.DMA((MOE_ROW_SLOTS,)), pltpu.SemaphoreType.DMA((2, 3))],
    )
    return pl.pallas_call(
        functools.partial(_expert_kernel, layer * N_EXPERTS),
        grid_spec=grid_spec,
        out_shape=jax.ShapeDtypeStruct((n_tiles * tm, d), F32),
        compiler_params=_cparams(("arbitrary",)),
        name="moe_experts",
    )(tile_expert, next_expert, parity, n_used, row_tok, h, g, w_gate, w_up, w_down)


def _combine_kernel(final, pos_ref, y_hbm, h_ref, slab_ref, fg_ref, o_ref, buf, sem):
    i = pl.program_id(0)
    n_steps = pl.num_programs(0)
    tm = h_ref.shape[0]
    slot = i % 2

    def start(step, sl):
        for j in range(2):
            _start_row_gather(y_hbm, buf.at[sl, j], sem.at[sl, j], pos_ref, 2 * step * tm + j, 2, tm)

    @pl.when(i == 0)
    def _():
        start(0, 0)

    @pl.when(i + 1 < n_steps)
    def _():
        start(i + 1, 1 - slot)

    for j in range(2):
        _wait_row_gather(buf.at[slot, j], sem.at[slot, j])
    slab = slab_ref[...]
    out = h_ref[...] + slab[:, 2:3] * buf[slot, 0] + slab[:, 3:4] * buf[slot, 1]
    if final:
        out = _rms(out, fg_ref[...])
    o_ref[...] = out


def moe_combine(y_sorted, pos, h, slab, final_gain, final, tm=128):
    n, d = h.shape
    grid_spec = pltpu.PrefetchScalarGridSpec(
        num_scalar_prefetch=1,
        grid=(n // tm,),
        in_specs=[
            pl.BlockSpec(memory_space=pl.ANY),
            pl.BlockSpec((tm, d), lambda i, p: (i, 0)),
            pl.BlockSpec((tm, LANES), lambda i, p: (i, 0)),
            pl.BlockSpec((1, d), lambda i, p: (0, 0)),
        ],
        out_specs=pl.BlockSpec((tm, d), lambda i, p: (i, 0)),
        scratch_shapes=[pltpu.VMEM((2, 2, tm, d), F32), pltpu.SemaphoreType.DMA((2, 2))],
    )
    return pl.pallas_call(
        functools.partial(_combine_kernel, final),
        grid_spec=grid_spec,
        out_shape=jax.ShapeDtypeStruct((n, d), F32),
        compiler_params=_cparams(("arbitrary",)),
        name="moe_combine",
    )(pos, y_sorted, h, slab, final_gain.reshape(1, d))


def _routing_tables(slab, n_tiles):
    n = slab.shape[0]
    tm = MOE_TILE
    e = slab[:, 0:2].astype(jnp.int32).reshape(-1)
    onehot = (e[:, None] == jnp.arange(N_EXPERTS, dtype=jnp.int32)[None, :]).astype(jnp.int32)
    csum = jnp.cumsum(onehot, axis=0)
    counts = csum[-1]
    rank = jnp.take_along_axis(csum, e[:, None], axis=1)[:, 0] - 1
    tiles_per = (counts + tm - 1) // tm
    tile_end = jnp.cumsum(tiles_per)
    tile_start = tile_end - tiles_per
    pos = tile_start[e] * tm + rank
    row_tok = jnp.zeros((n_tiles * tm,), jnp.int32).at[pos].set(jnp.arange(2 * n, dtype=jnp.int32) // 2)
    n_used = tile_end[-1]
    t_idx = jnp.arange(n_tiles, dtype=jnp.int32)
    tile_expert = jnp.searchsorted(tile_end, jnp.minimum(t_idx, n_used - 1), side="right").astype(jnp.int32)
    tile_expert = jnp.minimum(tile_expert, N_EXPERTS - 1)
    after = tile_end[tile_expert]
    next_expert = jnp.where(after < n_used, tile_expert[jnp.minimum(after, n_tiles - 1)], -1).astype(jnp.int32)
    change = jnp.concatenate([jnp.zeros((1,), jnp.int32), (tile_expert[1:] != tile_expert[:-1]).astype(jnp.int32)])
    parity = (jnp.cumsum(change) % 2).astype(jnp.int32)
    tables = (tile_expert, next_expert, parity, n_used.reshape(1).astype(jnp.int32), row_tok)
    return tables, pos.astype(jnp.int32)


def _split_in_weights(w):
    c = C_RWKV
    o = 3 * c
    wd0 = w[..., o:o + DECAY_RANK]
    wd1 = w[..., o + DECAY_RANK:o + 2 * DECAY_RANK]
    o += 2 * DECAY_RANK
    ad0 = w[..., o:o + ICLR_RANK]
    ad1 = w[..., o + ICLR_RANK:o + 2 * ICLR_RANK]
    o += 2 * ICLR_RANK
    gd = w[..., o:o + GATE_RANK]
    o += GATE_RANK
    att = w[..., o:]

    def padl(x):
        pad = [(0, 0)] * (x.ndim - 1) + [(0, RANK_PAD - x.shape[-1])]
        return jnp.pad(x, pad)

    rw = jnp.concatenate([w[..., :3 * c], padl(wd0), padl(wd1), padl(ad0), padl(ad1), gd], axis=-1)
    return rw, att


def kernel(x, mem, w_in, shift_prev, shift_next, decay_w0, decay_w2, iclr_a0, iclr_a2, gate_w2, vres_v0, vres_w1, vres_w2, k_k, k_a, r_k, ln_x_gain, ln_x_bias, att_sink, att_out_gain, rel_bias_table, w_out, norm_mix, norm_xatt, mem_norm, xatt_wq, xatt_wk, xatt_wv, xatt_wo, norm_moe, router_coarse_w, router_coarse_b, router_fine_w, router_fine_b, expert_w_gate, expert_w_up, expert_w_down, final_norm):
    batch, seq_len, d = x.shape
    depth = w_in.shape[0]
    n = batch * seq_len
    mem_len = mem.shape[1]
    h = x.reshape(n, d)
    memf = mem.reshape(batch * mem_len, d)
    bias = _t5_bias(rel_bias_table)
    n_tiles = (2 * n) // MOE_TILE + N_EXPERTS

    def row(p):
        return p[:, None, :]

    def pad_axis(p, axis, size):
        pad = [(0, 0)] * p.ndim
        pad[axis] = (0, size - p.shape[axis])
        return jnp.pad(p, pad)

    w_rw, w_at = _split_in_weights(w_in)
    w_in_all = jnp.concatenate([w_rw, w_at], axis=-1).astype(BF16)
    prep_params = (
        row(_split_in_weights(shift_prev)[0]), row(_split_in_weights(shift_next)[0]),
        decay_w0, pad_axis(decay_w2, 2, RANK_PAD).astype(BF16),
        iclr_a0, pad_axis(iclr_a2, 2, RANK_PAD).astype(BF16),
        gate_w2.astype(BF16), row(k_k), row(k_a), row(r_k))
    vres_params = (row(vres_v0), pad_axis(vres_w1, 2, RANK_PAD).astype(BF16),
                   pad_axis(vres_w2, 1, RANK_PAD).astype(BF16))
    ln_g, ln_b = row(ln_x_gain), row(ln_x_bias)
    att_gain = row(att_out_gain)
    w_out_all = w_out.astype(BF16)
    g_mix, g_xatt, g_mem, g_moe = row(norm_mix), row(norm_xatt), row(mem_norm), row(norm_moe)
    wq_all = xatt_wq.astype(BF16)
    wkv_all = jnp.concatenate([xatt_wk, xatt_wv], axis=-1).astype(BF16)
    wo_all = xatt_wo.astype(BF16)
    w_r = pad_axis(jnp.concatenate([router_fine_w, router_coarse_w], axis=-1), 2, LANES)
    w_r2_all = jnp.concatenate(_split_bf16(w_r), axis=-1)
    b_r_all = row(pad_axis(jnp.concatenate([router_fine_b.reshape(depth, -1), router_coarse_b], axis=-1), 1, LANES))
    wg_all = expert_w_gate.reshape(depth * N_EXPERTS, d, D_EXPERT)
    wu_all = expert_w_up.reshape(depth * N_EXPERTS, d, D_EXPERT)
    wd_all = expert_w_down.reshape(depth * N_EXPERTS, D_EXPERT, d)

    v_first = None
    for l in range(depth):
        u, qkv = in_proj(h, g_mix, w_in_all, l, RW_COLS)
        r, v, kk, b0, b1, lw0, lw1, kd0, kd1, g, bg = rwkv_prep(
            u, seq_len, l, prep_params, vres_params if l > 0 else None, v_first)
        if l == 0:
            v_first = v
        yf, yb = rwkv_chunk(r, v, kk, b0, b1, lw0, lw1, kd0, kd1, batch, seq_len)
        y_r = rwkv_post(yf, yb, g, bg, ln_g, ln_b, l)
        y_a = window_attention(qkv, bias, att_sink, att_gain, l, batch, seq_len)
        h = out_proj(y_r, y_a, w_out_all, l, h)
        kv = norm_matmul(memf, g_mem, wkv_all, l, memf.shape[0], 2 * XATT_WIDTH)
        h, slab = cross_attention_router(h, g_xatt, wq_all, kv, wo_all, g_moe, w_r2_all, b_r_all, l, batch, seq_len)
        tables, pos = _routing_tables(slab, n_tiles)
        y_sorted = moe_experts(h, g_moe, wg_all, wu_all, wd_all, l, tables, n_tiles)
        h = moe_combine(y_sorted, pos, h, slab, final_norm, l == depth - 1)
    return h.reshape(batch, seq_len, d)
```

```python
import functools
import math

import jax
import jax.numpy as jnp
from jax import lax
from jax.experimental import pallas as pl
from jax.experimental.pallas import tpu as pltpu

F32 = jnp.float32
BF16 = jnp.bfloat16
HI = lax.Precision.HIGHEST

D_MODEL = 2048
RWKV_HEADS = 16
HEAD_DIM = 64
C_RWKV = RWKV_HEADS * HEAD_DIM
ATT_Q_HEADS = 16
ATT_KV_HEADS = 4
ATT_WIDTH = ATT_Q_HEADS * HEAD_DIM
ATT_KV_WIDTH = ATT_KV_HEADS * HEAD_DIM
WINDOW = 128
BLOCK = 128
DECAY_RANK = 96
ICLR_RANK = 96
VRES_RANK = 64
GATE_RANK = 256
GN_EPS = 64e-5
N_BUCKETS = 32
MAX_DISTANCE = 128
XATT_HEADS = 4
XATT_HEAD_DIM = 128
XATT_WIDTH = XATT_HEADS * XATT_HEAD_DIM
N_GROUPS = 4
EXPERTS_PER_GROUP = 8
N_EXPERTS = N_GROUPS * EXPERTS_PER_GROUP
D_EXPERT = 512
EPS = 1e-6
NEG_INF = -1e30

LANES = 128
RANK_PAD = 128
R0, K0, V0 = 0, C_RWKV, 2 * C_RWKV
WD0 = 3 * C_RWKV
WD1 = WD0 + RANK_PAD
AD0 = WD1 + RANK_PAD
AD1 = AD0 + RANK_PAD
GD0 = AD1 + RANK_PAD
RW_COLS = GD0 + GATE_RANK
AT_COLS = ATT_WIDTH + 2 * ATT_KV_WIDTH

CHUNK = 64
MOE_TILE = 256
MOE_ROW_SLOTS = 4
VMEM_LIMIT = 56 * 1024 * 1024


def _cparams(sem, vmem=VMEM_LIMIT):
    return pltpu.CompilerParams(dimension_semantics=sem, vmem_limit_bytes=vmem)


def _sigmoid(x):
    return 1.0 / (1.0 + jnp.exp(-x))


def _dot(a, b, prec=None):
    return jnp.dot(a, b, preferred_element_type=F32, precision=prec)


def _dot_nt(a, b, prec=None):
    return lax.dot_general(a, b, (((1,), (1,)), ((), ())), preferred_element_type=F32, precision=prec)


def _dot_tn(a, b, prec=None):
    return lax.dot_general(a, b, (((0,), (0,)), ((), ())), preferred_element_type=F32, precision=prec)


def _rms(x, g):
    ms = jnp.mean(x * x, axis=-1, keepdims=True)
    return x * lax.rsqrt(ms + EPS) * g


def _split_bf16(x):
    hi = x.astype(BF16)
    lo = (x - hi.astype(F32)).astype(BF16)
    return hi, lo


def _head_block_ones():
    ri = lax.broadcasted_iota(jnp.int32, (LANES, LANES), 0)
    ci = lax.broadcasted_iota(jnp.int32, (LANES, LANES), 1)
    return ((ri >> 6) == (ci >> 6)).astype(BF16)


def _head_sums(x, bd2):
    rows = x.shape[0]
    hi, lo = _split_bf16(x)
    outs = []
    for gi in range(x.shape[1] // LANES):
        sl = slice(gi * LANES, (gi + 1) * LANES)
        both = _dot(jnp.concatenate([hi[:, sl], lo[:, sl]], axis=0), bd2)
        outs.append(both[0:rows] + both[rows:])
    return outs[0] if len(outs) == 1 else jnp.concatenate(outs, axis=1)


def _layer_block(shape, layer, index=None):
    index = (0,) * len(shape) if index is None else index
    return pl.BlockSpec((None,) + tuple(shape), lambda *_: (layer,) + tuple(index), pipeline_mode=pl.Buffered(1))


def _norm_mm_kernel(x_ref, g_ref, w_ref, o_ref, xn_ref):
    @pl.when(pl.program_id(1) == 0)
    def _():
        xn_ref[...] = _rms(x_ref[...], g_ref[...]).astype(BF16)

    o_ref[...] = _dot(xn_ref[...], w_ref[...]).astype(o_ref.dtype)


def norm_matmul(x, g, w, layer, tm, tn, out_dtype=F32):
    n, d = x.shape
    nc = w.shape[2]
    return pl.pallas_call(
        _norm_mm_kernel,
        grid=(n // tm, nc // tn),
        in_specs=[
            pl.BlockSpec((tm, d), lambda i, j: (i, 0)),
            pl.BlockSpec((None, 1, d), lambda i, j: (layer, 0, 0)),
            pl.BlockSpec((None, d, tn), lambda i, j: (layer, 0, j)),
        ],
        out_specs=pl.BlockSpec((tm, tn), lambda i, j: (i, j)),
        out_shape=jax.ShapeDtypeStruct((n, nc), out_dtype),
        scratch_shapes=[pltpu.VMEM((tm, d), BF16)],
        compiler_params=_cparams(("parallel", "arbitrary")),
        name="norm_matmul",
    )(x, g, w)


def _in_proj_kernel(x_ref, g_ref, w_ref, u_ref, qkv_ref):
    xn = _rms(x_ref[...], g_ref[...]).astype(BF16)
    nu = u_ref.shape[1]
    u_ref[...] = _dot(xn, w_ref[:, 0:nu])
    qkv_ref[...] = _dot(xn, w_ref[:, nu:])


def in_proj(x, g, w, layer, n_rw, tm=512):
    n, d = x.shape
    nc = w.shape[2]
    return pl.pallas_call(
        _in_proj_kernel,
        grid=(n // tm,),
        in_specs=[
            pl.BlockSpec((tm, d), lambda i: (i, 0)),
            _layer_block((1, d), layer),
            _layer_block((d, nc), layer),
        ],
        out_specs=[pl.BlockSpec((tm, n_rw), lambda i: (i, 0)), pl.BlockSpec((tm, nc - n_rw), lambda i: (i, 0))],
        out_shape=[jax.ShapeDtypeStruct((n, n_rw), F32), jax.ShapeDtypeStruct((n, nc - n_rw), F32)],
        compiler_params=_cparams(("parallel",)),
        name="in_proj",
    )(x, g, w)


def _prep_kernel(seq_tiles, has_vres, *refs):
    if has_vres:
        (u_ref, hp_ref, hn_ref, mup_ref, mun_ref, w0_ref, w2_ref, a0_ref, a2_ref, gw_ref, kk_ref, ka_ref, rk_ref,
         vf_ref, v0_ref, v1_ref, v2_ref,
         r_o, v_o, kk_o, b0_o, b1_o, lw0_o, lw1_o, kd0_o, kd1_o, g_o, bg_o) = refs
    else:
        (u_ref, hp_ref, hn_ref, mup_ref, mun_ref, w0_ref, w2_ref, a0_ref, a2_ref, gw_ref, kk_ref, ka_ref, rk_ref,
         r_o, v_o, kk_o, b0_o, b1_o, lw0_o, lw1_o, kd0_o, kd1_o, g_o, bg_o) = refs
    tm = u_ref.shape[0]
    i = pl.program_id(0)
    it = i % seq_tiles
    has_prev = jnp.where(it != 0, 1.0, 0.0)
    has_next = jnp.where(it != seq_tiles - 1, 1.0, 0.0)
    row8 = lax.broadcasted_iota(jnp.int32, (8, 1), 0)

    def shifted(c0, c1):
        u = u_ref[:, c0:c1]
        pr = hp_ref[7:8, c0:c1] * has_prev
        nx = hn_ref[0:1, c0:c1] * has_next
        prev = pltpu.roll(u, 1, 0)
        nxt = pltpu.roll(u, tm - 1, 0)
        prev = jnp.concatenate([jnp.where(row8 == 0, pr, prev[0:8]), prev[8:]], axis=0)
        nxt = jnp.concatenate([nxt[:tm - 8], jnp.where(row8 == 7, nx, nxt[tm - 8:])], axis=0)
        mp = mup_ref[:, c0:c1]
        mn = mun_ref[:, c0:c1]
        return u * (1.0 - mp - mn) + mp * prev + mn * nxt

    bd2 = _head_block_ones()

    r = shifted(R0, R0 + C_RWKV)
    k = shifted(K0, K0 + C_RWKV)
    v = shifted(V0, V0 + C_RWKV)
    r_o[...] = r.astype(r_o.dtype)
    if has_vres:
        low = _dot(v.astype(BF16), v1_ref[...])
        mix = _sigmoid(v0_ref[...] + _dot(low.astype(BF16), v2_ref[...]))
        v = v + (vf_ref[...].astype(F32) - v) * mix
    v_o[...] = v.astype(v_o.dtype)
    kkraw = k * kk_ref[...]
    kk = kkraw / jnp.maximum(jnp.sqrt(_head_sums(kkraw * kkraw, bd2)), 1e-12)
    kk_o[...] = kk.astype(kk_o.dtype)
    ka = ka_ref[...]
    lw_scale = -math.exp(-0.5)
    kd_sum = None
    for z, (wc, ac, lw_o, b_o, kd_o) in enumerate(
            ((WD0, AD0, lw0_o, b0_o, kd0_o), (WD1, AD1, lw1_o, b1_o, kd1_o))):
        wd = shifted(wc, wc + RANK_PAD)
        w_pre = w0_ref[z:z + 1, :] + _dot(jnp.tanh(wd).astype(BF16), w2_ref[z])
        lw_o[...] = lw_scale * _sigmoid(w_pre)
        ad = shifted(ac, ac + RANK_PAD)
        a = _sigmoid(a0_ref[z:z + 1, :] + _dot(ad.astype(BF16), a2_ref[z]))
        b_o[...] = (kk * a).astype(b_o.dtype)
        kd = k * (1.0 + (a - 1.0) * ka)
        kd_o[...] = kd.astype(kd_o.dtype)
        kd_sum = kd if kd_sum is None else kd_sum + kd
    gd = shifted(GD0, GD0 + GATE_RANK)
    g = _dot(_sigmoid(gd).astype(BF16), gw_ref[...])
    g_o[...] = g.astype(g_o.dtype)
    bonus = _head_sums(r * kd_sum * rk_ref[...], bd2) * v
    bg_o[...] = (bonus * g).astype(bg_o.dtype)


def rwkv_prep(u, seq_len, layer, params, vres, v_first, tm=256):
    n = u.shape[0]
    seq_tiles = seq_len // tm
    hb = tm // 8
    nblk8 = n // 8
    has_vres = vres is not None
    c = C_RWKV
    in_specs = [
        pl.BlockSpec((tm, RW_COLS), lambda i: (i, 0)),
        pl.BlockSpec((8, RW_COLS), lambda i: (jnp.maximum(i * hb - 1, 0), 0)),
        pl.BlockSpec((8, RW_COLS), lambda i: (jnp.minimum((i + 1) * hb, nblk8 - 1), 0)),
    ] + [_layer_block(p.shape[1:], layer) for p in params]
    args = [u, u, u] + list(params)
    if has_vres:
        in_specs += [pl.BlockSpec((tm, c), lambda i: (i, 0))] + [_layer_block(p.shape[1:], layer - 1) for p in vres]
        args += [v_first] + list(vres)
    out_spec = pl.BlockSpec((tm, c), lambda i: (i, 0))
    dtypes = [BF16] * 5 + [F32] * 2 + [BF16] * 4
    return pl.pallas_call(
        functools.partial(_prep_kernel, seq_tiles, has_vres),
        grid=(n // tm,),
        in_specs=in_specs,
        out_specs=[out_spec] * len(dtypes),
        out_shape=[jax.ShapeDtypeStruct((n, c), dt) for dt in dtypes],
        compiler_params=_cparams(("parallel",)),
        name="rwkv_prep",
    )(*args)


def _chunk_kernel(rf, vf, kkf, af, lwf, kdf, rb, vb, kkb, ab, lwb, kdb, yf_o, yb_o, s_ref):
    nbatch, L, width = rf.shape
    L2 = 2 * L
    npairs = width // LANES

    @pl.when(pl.program_id(0) == 0)
    def _():
        s_ref[...] = jnp.zeros_like(s_ref)

    lane = lax.broadcasted_iota(jnp.int32, (1, LANES), 1)
    m1 = lane < HEAD_DIM
    t_i = lax.broadcasted_iota(jnp.int32, (L, L2), 0)
    s_i = lax.broadcasted_iota(jnp.int32, (L, L2), 1) & (L - 1)
    ri = lax.broadcasted_iota(jnp.int32, (LANES, LANES), 0)
    ci = lax.broadcasted_iota(jnp.int32, (LANES, LANES), 1)
    same_head = (ri >> 6) == (ci >> 6)

    def stack(x):
        z = jnp.zeros_like(x)
        return jnp.concatenate([jnp.where(m1, x, z), jnp.where(m1, z, x)], axis=0)

    chains = []
    for rev, ins in ((False, (rf, vf, kkf, af, lwf, kdf)), (True, (rb, vb, kkb, ab, lwb, kdb))):
        strict = (s_i > t_i) if rev else (s_i < t_i)
        incl = (s_i >= t_i) if rev else (s_i <= t_i)
        incl4 = jnp.concatenate([incl, incl], axis=1)
        tri2 = incl.astype(BF16)
        for bi in range(nbatch):
            r, vb16, kk, bn, lw, kd = (x[bi] for x in ins)
            r, kk, bn, kd = (x.astype(F32) for x in (r, kk, bn, kd))
            c = _dot(tri2, jnp.concatenate(_split_bf16(lw), axis=0))
            c_last = c[0:1, :] if rev else c[L - 1:L, :]
            decay_last = jnp.exp(c_last)
            e_pos = jnp.exp(c)
            e_neg = jnp.exp(-c)
            e_rem = jnp.exp(c_last - c)
            a_t = (-kk * jnp.exp(c - lw)).astype(BF16)
            b_t = (bn * e_neg).astype(BF16)
            k_t = (kd * e_neg).astype(BF16)
            r_t = (r * e_pos).astype(BF16)
            b_h = (bn * e_rem).astype(BF16)
            k_h = (kd * e_rem).astype(BF16)
            for pr in range(npairs):
                sl = slice(pr * LANES, (pr + 1) * LANES)
                chains.append(dict(
                    strict=strict, incl4=incl4, s_ref=s_ref.at[1 if rev else 0, bi, pr],
                    decay_last=decay_last[:, sl],
                    ar=jnp.concatenate([a_t[:, sl], r_t[:, sl]], axis=0),
                    bk=jnp.concatenate([stack(b_t[:, sl]), stack(k_t[:, sl])], axis=0),
                    bkh=jnp.concatenate([b_h[:, sl], k_h[:, sl]], axis=0),
                    v=vb16[:, sl], v_s=stack(vb16[:, sl])))
    for ch in chains:
        ch["s0"] = ch["s_ref"][...]
        ch["sc"] = _dot_nt(ch["ar"], ch["bk"])
    for ch in chains:
        ch["from_state"] = _dot_nt(ch["ar"], ch["s0"].astype(BF16))
    for ch in chains:
        sc = ch["sc"]
        m_ak = jnp.where(ch["strict"], sc[0:L, L2:], 0.0).astype(BF16)
        ch["m_r"] = jnp.where(ch["incl4"], sc[L:, :], 0.0).astype(BF16)
        ch["p"] = jnp.where(ch["strict"], sc[0:L, 0:L2], 0.0)
        ch["x"] = ch["from_state"][0:L] + _dot(m_ak, ch["v_s"])
    n_steps = int(math.log2(L))
    for step in range(n_steps):
        for ch in chains:
            pb = ch["p"].astype(BF16)
            xs = stack(ch["x"].astype(BF16))
            if step < n_steps - 1:
                res = _dot(pb, jnp.concatenate([xs, stack(pb)], axis=1))
                ch["x"] = ch["x"] + res[:, 0:LANES]
                ch["p"] = res[:, LANES:]
            else:
                ch["x"] = ch["x"] + _dot(pb, xs)
    for ch in chains:
        ub = ch["x"].astype(BF16)
        ch["ub"] = ub
        ch["y"] = ch["from_state"][L:] + _dot(ch["m_r"], jnp.concatenate([stack(ub), ch["v_s"]], axis=0))
    for ch in chains:
        upd = _dot_tn(jnp.concatenate([ch["ub"], ch["v"]], axis=0), ch["bkh"])
        ch["s_ref"][...] = ch["s0"] * ch["decay_last"] + jnp.where(same_head, upd, 0.0)
    for di, y_o in enumerate((yf_o, yb_o)):
        for bi in range(nbatch):
            group = chains[(di * nbatch + bi) * npairs:(di * nbatch + bi + 1) * npairs]
            y_o[bi] = jnp.concatenate([ch["y"] for ch in group], axis=1).astype(y_o.dtype)


def rwkv_chunk(r, v, kk, b0, b1, lw0, lw1, kd0, kd1, batch, seq_len):
    c = C_RWKV
    L = CHUNK
    nc = seq_len // L

    def r3(x):
        return x.reshape(batch, seq_len, c)

    fwd = pl.BlockSpec((batch, L, c), lambda t: (0, t, 0))
    bwd = pl.BlockSpec((batch, L, c), lambda t: (0, nc - 1 - t, 0))
    sds = jax.ShapeDtypeStruct((batch, seq_len, c), BF16)
    yf, yb = pl.pallas_call(
        _chunk_kernel,
        grid=(nc,),
        in_specs=[fwd] * 6 + [bwd] * 6,
        out_specs=[fwd, bwd],
        out_shape=[sds, sds],
        scratch_shapes=[pltpu.VMEM((2, batch, c // LANES, LANES, LANES), F32)],
        compiler_params=_cparams(("arbitrary",)),
        name="rwkv_chunk",
    )(r3(r), r3(v), r3(kk), r3(b0), r3(lw0), r3(kd0), r3(r), r3(v), r3(kk), r3(b1), r3(lw1), r3(kd1))
    return yf.reshape(-1, c), yb.reshape(-1, c)


def _post_kernel(yf, yb, g, bg, lng, lnb, o_ref):
    bd2 = _head_block_ones()
    y = yf[...].astype(F32) + yb[...].astype(F32)
    mu = _head_sums(y, bd2) * (1.0 / HEAD_DIM)
    d = y - mu
    var = _head_sums(d * d, bd2) * (1.0 / HEAD_DIM)
    yn = d * lax.rsqrt(var + GN_EPS) * lng[...] + lnb[...]
    o_ref[...] = (yn * g[...].astype(F32) + bg[...].astype(F32)).astype(o_ref.dtype)


def rwkv_post(yf, yb, g, bg, ln_g, ln_b, layer, tm=512):
    n, c = yf.shape
    blk = pl.BlockSpec((tm, c), lambda i: (i, 0))
    vec = _layer_block((1, c), layer)
    return pl.pallas_call(
        _post_kernel,
        grid=(n // tm,),
        in_specs=[blk] * 4 + [vec] * 2,
        out_specs=blk,
        out_shape=jax.ShapeDtypeStruct((n, c), BF16),
        compiler_params=_cparams(("parallel",)),
        name="rwkv_post",
    )(yf, yb, g, bg, ln_g, ln_b)


def _attn_kernel(layer, q_ref, kp, kc, kn, vp, vc, vn, bias_ref, sink_ref, gain_ref, o_ref):
    n = pl.program_id(1)
    nb = pl.num_programs(1)
    i = lax.broadcasted_iota(jnp.int32, (BLOCK, 3 * BLOCK), 0)
    j = lax.broadcasted_iota(jnp.int32, (BLOCK, 3 * BLOCK), 1)
    rel = j - BLOCK - i
    valid = (jnp.abs(rel) <= WINDOW) & ((j >= BLOCK) | (n > 0)) & ((j < 2 * BLOCK) | (n < nb - 1))
    kb = jnp.concatenate([kp[...], kc[...], kn[...]], axis=0).astype(BF16)
    vb = jnp.concatenate([vp[...], vc[...], vn[...]], axis=0).astype(BF16)
    q = (q_ref[...] * (HEAD_DIM ** -0.5)).astype(BF16)
    group = ATT_Q_HEADS // ATT_KV_HEADS
    valid_g = jnp.concatenate([valid] * group, axis=0)
    row_head = lax.broadcasted_iota(jnp.int32, (group * BLOCK, 1), 0) >> 7

    def scores(kh):
        qg = jnp.concatenate([q[:, (kh * group + gi) * HEAD_DIM:(kh * group + gi + 1) * HEAD_DIM]
                              for gi in range(group)], axis=0)
        return _dot_nt(qg, kb[:, kh * HEAD_DIM:(kh + 1) * HEAD_DIM])

    def softmax(kh, s):
        bias = bias_ref[kh * group:(kh + 1) * group].reshape(group * BLOCK, 3 * BLOCK)
        s = jnp.where(valid_g, s + bias, NEG_INF)
        sk = jnp.zeros((group * BLOCK, 1), F32)
        for gi in range(group):
            sk = jnp.where(row_head == gi, sink_ref[layer, kh * group + gi], sk)
        m = jnp.maximum(jnp.max(s, axis=-1, keepdims=True), sk)
        p = jnp.exp(s - m)
        denom = jnp.sum(p, axis=-1, keepdims=True) + jnp.exp(sk - m)
        return p.astype(BF16), denom

    all_scores = [scores(kh) for kh in range(ATT_KV_HEADS)]
    probs = [softmax(kh, s) for kh, s in enumerate(all_scores)]
    outs = []
    for kh, (p, denom) in enumerate(probs):
        og = _dot(p, vb[:, kh * HEAD_DIM:(kh + 1) * HEAD_DIM]) / denom
        outs += [og[gi * BLOCK:(gi + 1) * BLOCK] for gi in range(group)]
    o = jnp.concatenate(outs, axis=-1)
    o_ref[...] = _rms(o, gain_ref[...]).astype(o_ref.dtype)


def window_attention(qkv, bias, sink, gain, layer, batch, seq_len):
    n = qkv.shape[0]
    nb = seq_len // BLOCK
    kcol = ATT_WIDTH // ATT_KV_WIDTH
    vcol = kcol + 1

    def rows(off):
        def f(b, t):
            return jnp.clip(t + off, 0, nb - 1) + b * nb
        return f

    def kv_spec(col, off):
        f = rows(off)
        return pl.BlockSpec((BLOCK, ATT_KV_WIDTH), lambda b, t: (f(b, t), col))

    return pl.pallas_call(
        functools.partial(_attn_kernel, layer),
        grid=(batch, nb),
        in_specs=[
            pl.BlockSpec((BLOCK, ATT_WIDTH), lambda b, t: (b * nb + t, 0)),
            kv_spec(kcol, -1), kv_spec(kcol, 0), kv_spec(kcol, 1),
            kv_spec(vcol, -1), kv_spec(vcol, 0), kv_spec(vcol, 1),
            pl.BlockSpec((ATT_Q_HEADS, BLOCK, 3 * BLOCK), lambda b, t: (0, 0, 0)),
            pl.BlockSpec(memory_space=pltpu.SMEM),
            _layer_block((1, ATT_WIDTH), layer),
        ],
        out_specs=pl.BlockSpec((BLOCK, ATT_WIDTH), lambda b, t: (b * nb + t, 0)),
        out_shape=jax.ShapeDtypeStruct((n, ATT_WIDTH), BF16),
        compiler_params=_cparams(("parallel", "parallel")),
        name="window_attention",
    )(qkv, qkv, qkv, qkv, qkv, qkv, qkv, bias, sink, gain)


def _t5_bias(rel_bias_table):
    i = jnp.arange(BLOCK)[:, None]
    j = jnp.arange(3 * BLOCK)[None, :]
    rel = j - BLOCK - i
    nbk = N_BUCKETS // 2
    max_exact = nbk // 2
    ret = jnp.where(rel > 0, nbk, 0)
    nabs = jnp.abs(rel)
    large = max_exact + (jnp.log(jnp.maximum(nabs, max_exact).astype(jnp.float32) / max_exact)
                         / math.log(MAX_DISTANCE / max_exact) * (nbk - max_exact)).astype(jnp.int32)
    large = jnp.minimum(large, nbk - 1)
    bucket = ret + jnp.where(nabs < max_exact, nabs, large)
    onehot = (bucket[None, :, :] == jnp.arange(N_BUCKETS)[:, None, None]).astype(F32)
    return jnp.einsum("bh,bij->hij", rel_bias_table.astype(F32), onehot, precision=HI)


def _outproj_kernel(yr, ya, w1, w2, h_ref, o_ref):
    o_ref[...] = h_ref[...] + _dot(yr[...], w1[...]) + _dot(ya[...], w2[...])


def out_proj(y_r, y_a, w, layer, h, tm=512):
    n, d = h.shape
    c = y_r.shape[1]
    return pl.pallas_call(
        _outproj_kernel,
        grid=(n // tm,),
        in_specs=[
            pl.BlockSpec((tm, c), lambda i: (i, 0)),
            pl.BlockSpec((tm, c), lambda i: (i, 0)),
            _layer_block((c, d), layer, (0, 0)),
            _layer_block((c, d), layer, (1, 0)),
            pl.BlockSpec((tm, d), lambda i: (i, 0)),
        ],
        out_specs=pl.BlockSpec((tm, d), lambda i: (i, 0)),
        out_shape=jax.ShapeDtypeStruct((n, d), F32),
        compiler_params=_cparams(("parallel",)),
        name="out_proj",
    )(y_r, y_a, w, w, h)


def _xatt_kernel(h_ref, g_ref, wq, kv_ref, wo, gm_ref, wr_ref, br_ref, o_ref, slab_ref):
    h = h_ref[...]
    hn = _rms(h, g_ref[...]).astype(BF16)
    q = _dot(hn, wq[...])
    kv = kv_ref[...]
    scale = XATT_HEAD_DIM ** -0.5
    outs = []
    for hd in range(XATT_HEADS):
        sl = slice(hd * XATT_HEAD_DIM, (hd + 1) * XATT_HEAD_DIM)
        k_h = kv[:, sl].astype(BF16)
        v_h = kv[:, XATT_WIDTH + hd * XATT_HEAD_DIM: XATT_WIDTH + (hd + 1) * XATT_HEAD_DIM].astype(BF16)
        s = _dot_nt(q[:, sl].astype(BF16), k_h) * scale
        m = jnp.max(s, axis=-1, keepdims=True)
        p = jnp.exp(s - m)
        p = p / jnp.sum(p, axis=-1, keepdims=True)
        outs.append(_dot(p.astype(BF16), v_h))
    o = jnp.concatenate(outs, axis=-1).astype(BF16)
    h_new = h + _dot(o, wo[...])
    o_ref[...] = h_new
    slab_ref[...] = _route(h_new, gm_ref[...], wr_ref[...], br_ref[...])


def cross_attention_router(h, g, wq, kv, wo, g_moe, w_r, b_r, layer, batch, seq_len, tm=512):
    n, d = h.shape
    m = kv.shape[0] // batch
    tiles = seq_len // tm
    return pl.pallas_call(
        _xatt_kernel,
        grid=(n // tm,),
        in_specs=[
            pl.BlockSpec((tm, d), lambda i: (i, 0)),
            _layer_block((1, d), layer),
            _layer_block((d, XATT_WIDTH), layer),
            pl.BlockSpec((m, 2 * XATT_WIDTH), lambda i: (i // tiles, 0)),
            _layer_block((XATT_WIDTH, d), layer),
            _layer_block((1, d), layer),
            _layer_block((d, 2 * LANES), layer),
            _layer_block((1, LANES), layer),
        ],
        out_specs=[pl.BlockSpec((tm, d), lambda i: (i, 0)), pl.BlockSpec((tm, LANES), lambda i: (i, 0))],
        out_shape=[jax.ShapeDtypeStruct((n, d), F32), jax.ShapeDtypeStruct((n, LANES), F32)],
        compiler_params=_cparams(("parallel",)),
        name="cross_attention_router",
    )(h, g, wq, kv, wo, g_moe, w_r, b_r)


def _route(h, g, w2, b):
    hn_hi, hn_lo = _split_bf16(_rms(h, g))
    both = _dot(hn_hi, w2) + _dot(hn_lo, w2)
    logits = both[:, 0:LANES] + both[:, LANES:] + b
    tm = logits.shape[0]
    lane = lax.broadcasted_iota(jnp.int32, (tm, LANES), 1)
    is_coarse = (lane >= N_EXPERTS) & (lane < N_EXPERTS + N_GROUPS)
    cl = jnp.where(is_coarse, logits, NEG_INF)
    cmax = jnp.max(cl, axis=-1, keepdims=True)
    csum = jnp.sum(jnp.where(is_coarse, jnp.exp(cl - cmax), 0.0), axis=-1, keepdims=True)
    p_g = 1.0 / csum
    lane_f = lane.astype(F32)
    grp_f = (lane >> 3).astype(F32)
    big = float(LANES)
    g_lane = jnp.min(jnp.where(is_coarse & (cl == cmax), lane_f, big), axis=-1, keepdims=True)
    g_idx = g_lane - float(N_EXPERTS)
    in_grp = (lane < N_EXPERTS) & (grp_f == g_idx)
    fl = jnp.where(in_grp, logits, NEG_INF)
    m1 = jnp.max(fl, axis=-1, keepdims=True)
    i1 = jnp.min(jnp.where(in_grp & (fl == m1), lane_f, big), axis=-1, keepdims=True)
    fl2 = jnp.where(lane_f == i1, NEG_INF, fl)
    m2 = jnp.max(fl2, axis=-1, keepdims=True)
    i2 = jnp.min(jnp.where(in_grp & (lane_f != i1) & (fl2 == m2), lane_f, big), axis=-1, keepdims=True)
    e2 = jnp.exp(m2 - m1)
    w1 = p_g / (1.0 + e2)
    w2 = p_g * e2 / (1.0 + e2)
    return jnp.where(lane == 0, i1,
                     jnp.where(lane == 1, i2,
                               jnp.where(lane == 2, w1, jnp.where(lane == 3, w2, 0.0))))


GATHER_UNROLL = 8


def _start_row_gather(src_hbm, dst, sem, idx_ref, first, stride, n_rows):
    def body(r, carry):
        row = idx_ref[first + r * stride]
        pltpu.make_async_copy(src_hbm.at[pl.ds(row, 1), :], dst.at[pl.ds(r, 1), :], sem).start()
        return carry

    lax.fori_loop(0, n_rows, body, 0, unroll=GATHER_UNROLL)


def _wait_row_gather(dst, sem):
    pltpu.make_async_copy(dst, dst, sem).wait()


def _expert_kernel(e0, te_ref, nxt_ref, par_ref, nused_ref, tok_ref, h_hbm, g_ref, wg_hbm, wu_hbm, wd_hbm, o_ref,
                   xbuf, wgf, wuf, wdf, wgb, wub, wdb, xsem, wsem):
    t = pl.program_id(0)
    nslots, tm = xbuf.shape[0], xbuf.shape[1]
    ahead = nslots - 1
    n_used = nused_ref[0]
    slot = lax.rem(t, nslots)

    def weight_copies(expert, ws):
        return [pltpu.make_async_copy(src.at[e0 + expert], dst.at[ws], wsem.at[ws, j])
                for j, (src, dst) in enumerate(((wg_hbm, wgf), (wu_hbm, wuf), (wd_hbm, wdf)))]

    @pl.when(t == 0)
    def _():
        for cp in weight_copies(te_ref[0], 0):
            cp.start()
        for j in range(ahead):
            @pl.when(j < n_used)
            def _():
                _start_row_gather(h_hbm, xbuf.at[j], xsem.at[j], tok_ref, j * tm, 1, tm)

    @pl.when(t + ahead < n_used)
    def _():
        nslot = lax.rem(t + ahead, nslots)
        _start_row_gather(h_hbm, xbuf.at[nslot], xsem.at[nslot], tok_ref, (t + ahead) * tm, 1, tm)

    @pl.when((t == 0) | (te_ref[t] != te_ref[jnp.maximum(t - 1, 0)]))
    def _():
        ws = par_ref[t]
        for cp in weight_copies(te_ref[t], ws):
            cp.wait()

        @pl.when(nxt_ref[t] >= 0)
        def _():
            for cp in weight_copies(nxt_ref[t], 1 - ws):
                cp.start()

        wgb[...] = wgf[ws].astype(BF16)
        wub[...] = wuf[ws].astype(BF16)
        wdb[...] = wdf[ws].astype(BF16)

    @pl.when(t < n_used)
    def _():
        _wait_row_gather(xbuf.at[slot], xsem.at[slot])
        xn = _rms(xbuf[slot], g_ref[...]).astype(BF16)
        hg = _dot(xn, wgb[...])
        hu = _dot(xn, wub[...])
        act = (hg * _sigmoid(hg) * hu).astype(BF16)
        o_ref[...] = _dot(act, wdb[...])

    @pl.when(t >= n_used)
    def _():
        o_ref[...] = jnp.zeros_like(o_ref)


def moe_experts(h, g, w_gate, w_up, w_down, layer, tables, n_tiles):
    n, d = h.shape
    tm = MOE_TILE
    tile_expert, next_expert, parity, n_used, row_tok = tables
    grid_spec = pltpu.PrefetchScalarGridSpec(
        num_scalar_prefetch=5,
        grid=(n_tiles,),
        in_specs=[
            pl.BlockSpec(memory_space=pl.ANY),
            _layer_block((1, d), layer),
            pl.BlockSpec(memory_space=pl.ANY),
            pl.BlockSpec(memory_space=pl.ANY),
            pl.BlockSpec(memory_space=pl.ANY),
        ],
        out_specs=pl.BlockSpec((tm, d), lambda t, *_: (t, 0)),
        scratch_shapes=[pltpu.VMEM((MOE_ROW_SLOTS, tm, d), F32),
                        pltpu.VMEM((2, d, D_EXPERT), F32), pltpu.VMEM((2, d, D_EXPERT), F32),
                        pltpu.VMEM((2, D_EXPERT, d), F32),
                        pltpu.VMEM((d, D_EXPERT), BF16), pltpu.VMEM((d, D_EXPERT), BF16),
                        pltpu.VMEM((D_EXPERT, d), BF16),
                        pltpu.SemaphoreType.DMA((MOE_ROW_SLOTS,)), pltpu.SemaphoreType.DMA((2, 3))],
    )
    return pl.pallas_call(
        functools.partial(_expert_kernel, layer * N_EXPERTS),
        grid_spec=grid_spec,
        out_shape=jax.ShapeDtypeStruct((n_tiles * tm, d), F32),
        compiler_params=_cparams(("arbitrary",)),
        name="moe_experts",
    )(tile_expert, next_expert, parity, n_used, row_tok, h, g, w_gate, w_up, w_down)


def _combine_kernel(final, pos_ref, y_hbm, h_ref, slab_ref, fg_ref, o_ref, buf, sem):
    i = pl.program_id(0)
    n_steps = pl.num_programs(0)
    tm = h_ref.shape[0]
    slot = i % 2

    def start(step, sl):
        for j in range(2):
            _start_row_gather(y_hbm, buf.at[sl, j], sem.at[sl, j], pos_ref, 2 * step * tm + j, 2, tm)

    @pl.when(i == 0)
    def _():
        start(0, 0)

    @pl.when(i + 1 < n_steps)
    def _():
        start(i + 1, 1 - slot)

    for j in range(2):
        _wait_row_gather(buf.at[slot, j], sem.at[slot, j])
    slab = slab_ref[...]
    out = h_ref[...] + slab[:, 2:3] * buf[slot, 0] + slab[:, 3:4] * buf[slot, 1]
    if final:
        out = _rms(out, fg_ref[...])
    o_ref[...] = out


def moe_combine(y_sorted, pos, h, slab, final_gain, final, tm=256):
    n, d = h.shape
    grid_spec = pltpu.PrefetchScalarGridSpec(
        num_scalar_prefetch=1,
        grid=(n // tm,),
        in_specs=[
            pl.BlockSpec(memory_space=pl.ANY),
            pl.BlockSpec((tm, d), lambda i, p: (i, 0)),
            pl.BlockSpec((tm, LANES), lambda i, p: (i, 0)),
            pl.BlockSpec((1, d), lambda i, p: (0, 0)),
        ],
        out_specs=pl.BlockSpec((tm, d), lambda i, p: (i, 0)),
        scratch_shapes=[pltpu.VMEM((2, 2, tm, d), F32), pltpu.SemaphoreType.DMA((2, 2))],
    )
    return pl.pallas_call(
        functools.partial(_combine_kernel, final),
        grid_spec=grid_spec,
        out_shape=jax.ShapeDtypeStruct((n, d), F32),
        compiler_params=_cparams(("arbitrary",)),
        name="moe_combine",
    )(pos, y_sorted, h, slab, final_gain.reshape(1, d))


def _routing_tables(slab, n_tiles):
    n = slab.shape[0]
    tm = MOE_TILE
    e = slab[:, 0:2].astype(jnp.int32).reshape(-1)
    onehot = (e[:, None] == jnp.arange(N_EXPERTS, dtype=jnp.int32)[None, :]).astype(jnp.int32)
    csum = jnp.cumsum(onehot, axis=0)
    counts = csum[-1]
    rank = jnp.take_along_axis(csum, e[:, None], axis=1)[:, 0] - 1
    tiles_per = (counts + tm - 1) // tm
    tile_end = jnp.cumsum(tiles_per)
    tile_start = tile_end - tiles_per
    pos = tile_start[e] * tm + rank
    row_tok = jnp.zeros((n_tiles * tm,), jnp.int32).at[pos].set(jnp.arange(2 * n, dtype=jnp.int32) // 2)
    n_used = tile_end[-1]
    t_idx = jnp.arange(n_tiles, dtype=jnp.int32)
    tile_expert = jnp.searchsorted(tile_end, jnp.minimum(t_idx, n_used - 1), side="right").astype(jnp.int32)
    tile_expert = jnp.minimum(tile_expert, N_EXPERTS - 1)
    after = tile_end[tile_expert]
    next_expert = jnp.where(after < n_used, tile_expert[jnp.minimum(after, n_tiles - 1)], -1).astype(jnp.int32)
    change = jnp.concatenate([jnp.zeros((1,), jnp.int32), (tile_expert[1:] != tile_expert[:-1]).astype(jnp.int32)])
    parity = (jnp.cumsum(change) % 2).astype(jnp.int32)
    tables = (tile_expert, next_expert, parity, n_used.reshape(1).astype(jnp.int32), row_tok)
    return tables, pos.astype(jnp.int32)


def _split_in_weights(w):
    c = C_RWKV
    o = 3 * c
    wd0 = w[..., o:o + DECAY_RANK]
    wd1 = w[..., o + DECAY_RANK:o + 2 * DECAY_RANK]
    o += 2 * DECAY_RANK
    ad0 = w[..., o:o + ICLR_RANK]
    ad1 = w[..., o + ICLR_RANK:o + 2 * ICLR_RANK]
    o += 2 * ICLR_RANK
    gd = w[..., o:o + GATE_RANK]
    o += GATE_RANK
    att = w[..., o:]

    def padl(x):
        pad = [(0, 0)] * (x.ndim - 1) + [(0, RANK_PAD - x.shape[-1])]
        return jnp.pad(x, pad)

    rw = jnp.concatenate([w[..., :3 * c], padl(wd0), padl(wd1), padl(ad0), padl(ad1), gd], axis=-1)
    return rw, att


def kernel(x, mem, w_in, shift_prev, shift_next, decay_w0, decay_w2, iclr_a0, iclr_a2, gate_w2, vres_v0, vres_w1, vres_w2, k_k, k_a, r_k, ln_x_gain, ln_x_bias, att_sink, att_out_gain, rel_bias_table, w_out, norm_mix, norm_xatt, mem_norm, xatt_wq, xatt_wk, xatt_wv, xatt_wo, norm_moe, router_coarse_w, router_coarse_b, router_fine_w, router_fine_b, expert_w_gate, expert_w_up, expert_w_down, final_norm):
    batch, seq_len, d = x.shape
    depth = w_in.shape[0]
    n = batch * seq_len
    mem_len = mem.shape[1]
    h = x.reshape(n, d)
    memf = mem.reshape(batch * mem_len, d)
    bias = _t5_bias(rel_bias_table)
    n_tiles = (2 * n) // MOE_TILE + N_EXPERTS

    def row(p):
        return p[:, None, :]

    def pad_axis(p, axis, size):
        pad = [(0, 0)] * p.ndim
        pad[axis] = (0, size - p.shape[axis])
        return jnp.pad(p, pad)

    w_rw, w_at = _split_in_weights(w_in)
    w_in_all = jnp.concatenate([w_rw, w_at], axis=-1).astype(BF16)
    prep_params = (
        row(_split_in_weights(shift_prev)[0]), row(_split_in_weights(shift_next)[0]),
        decay_w0, pad_axis(decay_w2, 2, RANK_PAD).astype(BF16),
        iclr_a0, pad_axis(iclr_a2, 2, RANK_PAD).astype(BF16),
        gate_w2.astype(BF16), row(k_k), row(k_a), row(r_k))
    vres_params = (row(vres_v0), pad_axis(vres_w1, 2, RANK_PAD).astype(BF16),
                   pad_axis(vres_w2, 1, RANK_PAD).astype(BF16))
    ln_g, ln_b = row(ln_x_gain), row(ln_x_bias)
    att_gain = row(att_out_gain)
    w_out_all = w_out.astype(BF16)
    g_mix, g_xatt, g_mem, g_moe = row(norm_mix), row(norm_xatt), row(mem_norm), row(norm_moe)
    wq_all = xatt_wq.astype(BF16)
    wkv_all = jnp.concatenate([xatt_wk, xatt_wv], axis=-1).astype(BF16)
    wo_all = xatt_wo.astype(BF16)
    w_r = pad_axis(jnp.concatenate([router_fine_w, router_coarse_w], axis=-1), 2, LANES)
    w_r2_all = jnp.concatenate(_split_bf16(w_r), axis=-1)
    b_r_all = row(pad_axis(jnp.concatenate([router_fine_b.reshape(depth, -1), router_coarse_b], axis=-1), 1, LANES))
    wg_all = expert_w_gate.reshape(depth * N_EXPERTS, d, D_EXPERT)
    wu_all = expert_w_up.reshape(depth * N_EXPERTS, d, D_EXPERT)
    wd_all = expert_w_down.reshape(depth * N_EXPERTS, D_EXPERT, d)

    v_first = None
    for l in range(depth):
        u, qkv = in_proj(h, g_mix, w_in_all, l, RW_COLS)
        r, v, kk, b0, b1, lw0, lw1, kd0, kd1, g, bg = rwkv_prep(
            u, seq_len, l, prep_params, vres_params if l > 0 else None, v_first)
        if l == 0:
            v_first = v
        yf, yb = rwkv_chunk(r, v, kk, b0, b1, lw0, lw1, kd0, kd1, batch, seq_len)
        y_r = rwkv_post(yf, yb, g, bg, ln_g, ln_b, l)
        y_a = window_attention(qkv, bias, att_sink, att_gain, l, batch, seq_len)
        h = out_proj(y_r, y_a, w_out_all, l, h)
        kv = norm_matmul(memf, g_mem, wkv_all, l, memf.shape[0], 2 * XATT_WIDTH)
        h, slab = cross_attention_router(h, g_xatt, wq_all, kv, wo_all, g_moe, w_r2_all, b_r_all, l, batch, seq_len)
        tables, pos = _routing_tables(slab, n_tiles)
        y_sorted = moe_experts(h, g_moe, wg_all, wu_all, wd_all, l, tables, n_tiles)
        h = moe_combine(y_sorted, pos, h, slab, final_norm, l == depth - 1)
    return h.reshape(batch, seq_len, d)
```

```python
import functools
import math

import jax
import jax.numpy as jnp
from jax import lax
from jax.experimental import pallas as pl
from jax.experimental.pallas import tpu as pltpu

F32 = jnp.float32
BF16 = jnp.bfloat16
HI = lax.Precision.HIGHEST

D_MODEL = 2048
RWKV_HEADS = 16
HEAD_DIM = 64
C_RWKV = RWKV_HEADS * HEAD_DIM
ATT_Q_HEADS = 16
ATT_KV_HEADS = 4
ATT_WIDTH = ATT_Q_HEADS * HEAD_DIM
ATT_KV_WIDTH = ATT_KV_HEADS * HEAD_DIM
WINDOW = 128
BLOCK = 128
DECAY_RANK = 96
ICLR_RANK = 96
VRES_RANK = 64
GATE_RANK = 256
GN_EPS = 64e-5
N_BUCKETS = 32
MAX_DISTANCE = 128
XATT_HEADS = 4
XATT_HEAD_DIM = 128
XATT_WIDTH = XATT_HEADS * XATT_HEAD_DIM
N_GROUPS = 4
EXPERTS_PER_GROUP = 8
N_EXPERTS = N_GROUPS * EXPERTS_PER_GROUP
D_EXPERT = 512
EPS = 1e-6
NEG_INF = -1e30
LOG2E = math.log2(math.e)

LANES = 128
RANK_PAD = 128
R0, K0, V0 = 0, C_RWKV, 2 * C_RWKV
WD0 = 3 * C_RWKV
WD1 = WD0 + RANK_PAD
AD0 = WD1 + RANK_PAD
AD1 = AD0 + RANK_PAD
GD0 = AD1 + RANK_PAD
RW_COLS = GD0 + GATE_RANK
AT_COLS = ATT_WIDTH + 2 * ATT_KV_WIDTH

CHUNK = 64
MOE_TILE = 256
MOE_ROW_SLOTS = 4
VMEM_LIMIT = 56 * 1024 * 1024


def _cparams(sem, vmem=VMEM_LIMIT):
    return pltpu.CompilerParams(dimension_semantics=sem, vmem_limit_bytes=vmem)


def _sigmoid(x):
    return 1.0 / (1.0 + jnp.exp(-x))


def _dot(a, b, prec=None):
    return jnp.dot(a, b, preferred_element_type=F32, precision=prec)


def _dot_nt(a, b, prec=None):
    return lax.dot_general(a, b, (((1,), (1,)), ((), ())), preferred_element_type=F32, precision=prec)


def _dot_tn(a, b, prec=None):
    return lax.dot_general(a, b, (((0,), (0,)), ((), ())), preferred_element_type=F32, precision=prec)


def _rms(x, g):
    ms = jnp.mean(x * x, axis=-1, keepdims=True)
    return x * lax.rsqrt(ms + EPS) * g


def _split_bf16(x):
    hi = x.astype(BF16)
    lo = (x - hi.astype(F32)).astype(BF16)
    return hi, lo


def _head_block_ones():
    ri = lax.broadcasted_iota(jnp.int32, (LANES, LANES), 0)
    ci = lax.broadcasted_iota(jnp.int32, (LANES, LANES), 1)
    return ((ri >> 6) == (ci >> 6)).astype(BF16)


def _head_sums(x, bd2):
    rows = x.shape[0]
    hi, lo = _split_bf16(x)
    outs = []
    for gi in range(x.shape[1] // LANES):
        sl = slice(gi * LANES, (gi + 1) * LANES)
        both = _dot(jnp.concatenate([hi[:, sl], lo[:, sl]], axis=0), bd2)
        outs.append(both[0:rows] + both[rows:])
    return outs[0] if len(outs) == 1 else jnp.concatenate(outs, axis=1)


def _layer_block(shape, layer, index=None):
    index = (0,) * len(shape) if index is None else index
    return pl.BlockSpec((None,) + tuple(shape), lambda *_: (layer,) + tuple(index), pipeline_mode=pl.Buffered(1))


def _norm_mm_kernel(x_ref, g_ref, w_ref, o_ref, xn_ref):
    @pl.when(pl.program_id(1) == 0)
    def _():
        xn_ref[...] = _rms(x_ref[...], g_ref[...]).astype(BF16)

    o_ref[...] = _dot(xn_ref[...], w_ref[...]).astype(o_ref.dtype)


def norm_matmul(x, g, w, layer, tm, tn, out_dtype=F32):
    n, d = x.shape
    nc = w.shape[2]
    return pl.pallas_call(
        _norm_mm_kernel,
        grid=(n // tm, nc // tn),
        in_specs=[
            pl.BlockSpec((tm, d), lambda i, j: (i, 0)),
            pl.BlockSpec((None, 1, d), lambda i, j: (layer, 0, 0)),
            pl.BlockSpec((None, d, tn), lambda i, j: (layer, 0, j)),
        ],
        out_specs=pl.BlockSpec((tm, tn), lambda i, j: (i, j)),
        out_shape=jax.ShapeDtypeStruct((n, nc), out_dtype),
        scratch_shapes=[pltpu.VMEM((tm, d), BF16)],
        compiler_params=_cparams(("parallel", "arbitrary")),
        name="norm_matmul",
    )(x, g, w)


def _in_proj_kernel(x_ref, g_ref, w_ref, u_ref, qkv_ref):
    xn = _rms(x_ref[...], g_ref[...]).astype(BF16)
    nu = u_ref.shape[1]
    u_ref[...] = _dot(xn, w_ref[:, 0:nu])
    qkv_ref[...] = _dot(xn, w_ref[:, nu:])


def in_proj(x, g, w, layer, n_rw, tm=512):
    n, d = x.shape
    nc = w.shape[2]
    return pl.pallas_call(
        _in_proj_kernel,
        grid=(n // tm,),
        in_specs=[
            pl.BlockSpec((tm, d), lambda i: (i, 0)),
            _layer_block((1, d), layer),
            _layer_block((d, nc), layer),
        ],
        out_specs=[pl.BlockSpec((tm, n_rw), lambda i: (i, 0)), pl.BlockSpec((tm, nc - n_rw), lambda i: (i, 0))],
        out_shape=[jax.ShapeDtypeStruct((n, n_rw), F32), jax.ShapeDtypeStruct((n, nc - n_rw), F32)],
        compiler_params=_cparams(("parallel",)),
        name="in_proj",
    )(x, g, w)


def _prep_kernel(seq_tiles, has_vres, *refs):
    if has_vres:
        (u_ref, hp_ref, hn_ref, mup_ref, mun_ref, w0_ref, w2_ref, a0_ref, a2_ref, gw_ref, kk_ref, ka_ref, rk_ref,
         vf_ref, v0_ref, v1_ref, v2_ref,
         r_o, v_o, kk_o, b0_o, b1_o, lw0_o, lw1_o, kd0_o, kd1_o, g_o, bg_o) = refs
    else:
        (u_ref, hp_ref, hn_ref, mup_ref, mun_ref, w0_ref, w2_ref, a0_ref, a2_ref, gw_ref, kk_ref, ka_ref, rk_ref,
         r_o, v_o, kk_o, b0_o, b1_o, lw0_o, lw1_o, kd0_o, kd1_o, g_o, bg_o) = refs
    tm = u_ref.shape[0]
    i = pl.program_id(0)
    it = i % seq_tiles
    has_prev = jnp.where(it != 0, 1.0, 0.0)
    has_next = jnp.where(it != seq_tiles - 1, 1.0, 0.0)
    row8 = lax.broadcasted_iota(jnp.int32, (8, 1), 0)

    def shifted(c0, c1):
        u = u_ref[:, c0:c1]
        pr = hp_ref[7:8, c0:c1] * has_prev
        nx = hn_ref[0:1, c0:c1] * has_next
        prev = pltpu.roll(u, 1, 0)
        nxt = pltpu.roll(u, tm - 1, 0)
        prev = jnp.concatenate([jnp.where(row8 == 0, pr, prev[0:8]), prev[8:]], axis=0)
        nxt = jnp.concatenate([nxt[:tm - 8], jnp.where(row8 == 7, nx, nxt[tm - 8:])], axis=0)
        mp = mup_ref[:, c0:c1]
        mn = mun_ref[:, c0:c1]
        return u * (1.0 - mp - mn) + mp * prev + mn * nxt

    bd2 = _head_block_ones()

    r = shifted(R0, R0 + C_RWKV)
    k = shifted(K0, K0 + C_RWKV)
    v = shifted(V0, V0 + C_RWKV)
    r_o[...] = r.astype(r_o.dtype)
    if has_vres:
        low = _dot(v.astype(BF16), v1_ref[...])
        mix = _sigmoid(v0_ref[...] + _dot(low.astype(BF16), v2_ref[...]))
        v = v + (vf_ref[...].astype(F32) - v) * mix
    v_o[...] = v.astype(v_o.dtype)
    kkraw = k * kk_ref[...]
    kk = kkraw / jnp.maximum(jnp.sqrt(_head_sums(kkraw * kkraw, bd2)), 1e-12)
    kk_o[...] = kk.astype(kk_o.dtype)
    ka = ka_ref[...]
    lw_scale = -math.exp(-0.5)
    kd_sum = None
    for z, (wc, ac, lw_o, b_o, kd_o) in enumerate(
            ((WD0, AD0, lw0_o, b0_o, kd0_o), (WD1, AD1, lw1_o, b1_o, kd1_o))):
        wd = shifted(wc, wc + RANK_PAD)
        w_pre = w0_ref[z:z + 1, :] + _dot(jnp.tanh(wd).astype(BF16), w2_ref[z])
        lw_o[...] = lw_scale * _sigmoid(w_pre)
        ad = shifted(ac, ac + RANK_PAD)
        a = _sigmoid(a0_ref[z:z + 1, :] + _dot(ad.astype(BF16), a2_ref[z]))
        b_o[...] = (kk * a).astype(b_o.dtype)
        kd = k * (1.0 + (a - 1.0) * ka)
        kd_o[...] = kd.astype(kd_o.dtype)
        kd_sum = kd if kd_sum is None else kd_sum + kd
    gd = shifted(GD0, GD0 + GATE_RANK)
    g = _dot(_sigmoid(gd).astype(BF16), gw_ref[...])
    g_o[...] = g.astype(g_o.dtype)
    bonus = _head_sums(r * kd_sum * rk_ref[...], bd2) * v
    bg_o[...] = (bonus * g).astype(bg_o.dtype)


def rwkv_prep(u, seq_len, layer, params, vres, v_first, tm=256):
    n = u.shape[0]
    seq_tiles = seq_len // tm
    hb = tm // 8
    nblk8 = n // 8
    has_vres = vres is not None
    c = C_RWKV
    in_specs = [
        pl.BlockSpec((tm, RW_COLS), lambda i: (i, 0)),
        pl.BlockSpec((8, RW_COLS), lambda i: (jnp.maximum(i * hb - 1, 0), 0)),
        pl.BlockSpec((8, RW_COLS), lambda i: (jnp.minimum((i + 1) * hb, nblk8 - 1), 0)),
    ] + [_layer_block(p.shape[1:], layer) for p in params]
    args = [u, u, u] + list(params)
    if has_vres:
        in_specs += [pl.BlockSpec((tm, c), lambda i: (i, 0))] + [_layer_block(p.shape[1:], layer - 1) for p in vres]
        args += [v_first] + list(vres)
    out_spec = pl.BlockSpec((tm, c), lambda i: (i, 0))
    dtypes = [BF16] * 5 + [F32] * 2 + [BF16] * 4
    return pl.pallas_call(
        functools.partial(_prep_kernel, seq_tiles, has_vres),
        grid=(n // tm,),
        in_specs=in_specs,
        out_specs=[out_spec] * len(dtypes),
        out_shape=[jax.ShapeDtypeStruct((n, c), dt) for dt in dtypes],
        compiler_params=_cparams(("parallel",)),
        name="rwkv_prep",
    )(*args)


def _chunk_kernel(rf, vf, kkf, af, lwf, kdf, rb, vb, kkb, ab, lwb, kdb, yf_o, yb_o, s_ref):
    nbatch, L, width = rf.shape
    L2 = 2 * L
    npairs = width // LANES

    @pl.when(pl.program_id(0) == 0)
    def _():
        s_ref[...] = jnp.zeros_like(s_ref)

    lane = lax.broadcasted_iota(jnp.int32, (1, LANES), 1)
    m1 = lane < HEAD_DIM
    t_i = lax.broadcasted_iota(jnp.int32, (L, L2), 0)
    s_i = lax.broadcasted_iota(jnp.int32, (L, L2), 1) & (L - 1)
    ri = lax.broadcasted_iota(jnp.int32, (LANES, LANES), 0)
    ci = lax.broadcasted_iota(jnp.int32, (LANES, LANES), 1)
    same_head = (ri >> 6) == (ci >> 6)

    def stack(x):
        z = jnp.zeros_like(x)
        return jnp.concatenate([jnp.where(m1, x, z), jnp.where(m1, z, x)], axis=0)

    chains = []
    for rev, ins in ((False, (rf, vf, kkf, af, lwf, kdf)), (True, (rb, vb, kkb, ab, lwb, kdb))):
        strict = (s_i > t_i) if rev else (s_i < t_i)
        incl = (s_i >= t_i) if rev else (s_i <= t_i)
        incl4 = jnp.concatenate([incl, incl], axis=1)
        tri2 = incl.astype(BF16)
        for bi in range(nbatch):
            r, vb16, kk, bn, lw, kd = (x[bi] for x in ins)
            r, kk, bn, kd = (x.astype(F32) for x in (r, kk, bn, kd))
            c = _dot(tri2, jnp.concatenate(_split_bf16(lw), axis=0))
            c_last = c[0:1, :] if rev else c[L - 1:L, :]
            decay_last = jnp.exp(c_last)
            e_pos = jnp.exp(c)
            e_neg = jnp.exp(-c)
            e_rem = jnp.exp(c_last - c)
            a_t = (-kk * jnp.exp(c - lw)).astype(BF16)
            b_t = (bn * e_neg).astype(BF16)
            k_t = (kd * e_neg).astype(BF16)
            r_t = (r * e_pos).astype(BF16)
            b_h = (bn * e_rem).astype(BF16)
            k_h = (kd * e_rem).astype(BF16)
            for pr in range(npairs):
                sl = slice(pr * LANES, (pr + 1) * LANES)
                chains.append(dict(
                    strict=strict, incl4=incl4, s_ref=s_ref.at[1 if rev else 0, bi, pr],
                    decay_last=decay_last[:, sl],
                    ar=jnp.concatenate([a_t[:, sl], r_t[:, sl]], axis=0),
                    bk=jnp.concatenate([stack(b_t[:, sl]), stack(k_t[:, sl])], axis=0),
                    bkh=jnp.concatenate([b_h[:, sl], k_h[:, sl]], axis=0),
                    v=vb16[:, sl], v_s=stack(vb16[:, sl])))
    for ch in chains:
        ch["s0"] = ch["s_ref"][...]
        ch["sc"] = _dot_nt(ch["ar"], ch["bk"])
    for ch in chains:
        ch["from_state"] = _dot_nt(ch["ar"], ch["s0"].astype(BF16))
    for ch in chains:
        sc = ch["sc"]
        m_ak = jnp.where(ch["strict"], sc[0:L, L2:], 0.0).astype(BF16)
        ch["m_r"] = jnp.where(ch["incl4"], sc[L:, :], 0.0).astype(BF16)
        ch["p"] = jnp.where(ch["strict"], sc[0:L, 0:L2], 0.0)
        ch["x"] = ch["from_state"][0:L] + _dot(m_ak, ch["v_s"])
    n_steps = int(math.log2(L))
    for step in range(n_steps):
        for ch in chains:
            pb = ch["p"].astype(BF16)
            xs = stack(ch["x"].astype(BF16))
            if step < n_steps - 1:
                res = _dot(pb, jnp.concatenate([xs, stack(pb)], axis=1))
                ch["x"] = ch["x"] + res[:, 0:LANES]
                ch["p"] = res[:, LANES:]
            else:
                ch["x"] = ch["x"] + _dot(pb, xs)
    for ch in chains:
        ub = ch["x"].astype(BF16)
        ch["ub"] = ub
        ch["y"] = ch["from_state"][L:] + _dot(ch["m_r"], jnp.concatenate([stack(ub), ch["v_s"]], axis=0))
    for ch in chains:
        upd = _dot_tn(jnp.concatenate([ch["ub"], ch["v"]], axis=0), ch["bkh"])
        ch["s_ref"][...] = ch["s0"] * ch["decay_last"] + jnp.where(same_head, upd, 0.0)
    for di, y_o in enumerate((yf_o, yb_o)):
        for bi in range(nbatch):
            group = chains[(di * nbatch + bi) * npairs:(di * nbatch + bi + 1) * npairs]
            y_o[bi] = jnp.concatenate([ch["y"] for ch in group], axis=1).astype(y_o.dtype)


def rwkv_chunk(r, v, kk, b0, b1, lw0, lw1, kd0, kd1, batch, seq_len):
    c = C_RWKV
    L = CHUNK
    nc = seq_len // L

    def r3(x):
        return x.reshape(batch, seq_len, c)

    fwd = pl.BlockSpec((batch, L, c), lambda t: (0, t, 0))
    bwd = pl.BlockSpec((batch, L, c), lambda t: (0, nc - 1 - t, 0))
    sds = jax.ShapeDtypeStruct((batch, seq_len, c), BF16)
    yf, yb = pl.pallas_call(
        _chunk_kernel,
        grid=(nc,),
        in_specs=[fwd] * 6 + [bwd] * 6,
        out_specs=[fwd, bwd],
        out_shape=[sds, sds],
        scratch_shapes=[pltpu.VMEM((2, batch, c // LANES, LANES, LANES), F32)],
        compiler_params=_cparams(("arbitrary",)),
        name="rwkv_chunk",
    )(r3(r), r3(v), r3(kk), r3(b0), r3(lw0), r3(kd0), r3(r), r3(v), r3(kk), r3(b1), r3(lw1), r3(kd1))
    return yf.reshape(-1, c), yb.reshape(-1, c)


def _post_kernel(yf, yb, g, bg, lng, lnb, o_ref):
    bd2 = _head_block_ones()
    y = yf[...].astype(F32) + yb[...].astype(F32)
    mu = _head_sums(y, bd2) * (1.0 / HEAD_DIM)
    d = y - mu
    var = _head_sums(d * d, bd2) * (1.0 / HEAD_DIM)
    yn = d * lax.rsqrt(var + GN_EPS) * lng[...] + lnb[...]
    o_ref[...] = (yn * g[...].astype(F32) + bg[...].astype(F32)).astype(o_ref.dtype)


def rwkv_post(yf, yb, g, bg, ln_g, ln_b, layer, tm=512):
    n, c = yf.shape
    blk = pl.BlockSpec((tm, c), lambda i: (i, 0))
    vec = _layer_block((1, c), layer)
    return pl.pallas_call(
        _post_kernel,
        grid=(n // tm,),
        in_specs=[blk] * 4 + [vec] * 2,
        out_specs=blk,
        out_shape=jax.ShapeDtypeStruct((n, c), BF16),
        compiler_params=_cparams(("parallel",)),
        name="rwkv_post",
    )(yf, yb, g, bg, ln_g, ln_b)


def _attn_kernel(layer, q_ref, kp, kc, kn, vp, vc, vn, bias_ref, sink_ref, gain_ref, o_ref, bias_s):
    n = pl.program_id(1)
    nb = pl.num_programs(1)
    j = lax.broadcasted_iota(jnp.int32, (1, 1, 3 * BLOCK), 2)
    off_seq = ((j < BLOCK) & (n == 0)) | ((j >= 2 * BLOCK) & (n == nb - 1))

    @pl.when((n == 0) | (n == 1) | (n == nb - 1))
    def _():
        bias_s[...] = bias_ref[...] + jnp.where(off_seq, NEG_INF, 0.0)

    kb = jnp.concatenate([kp[...], kc[...], kn[...]], axis=0).astype(BF16)
    vb = jnp.concatenate([vp[...], vc[...], vn[...]], axis=0).astype(BF16)
    q = (q_ref[...] * (HEAD_DIM ** -0.5 * LOG2E)).astype(BF16)
    group = ATT_Q_HEADS // ATT_KV_HEADS
    row_head = lax.broadcasted_iota(jnp.int32, (group * BLOCK, 1), 0) >> 7

    def scores(kh):
        qg = jnp.concatenate([q[:, (kh * group + gi) * HEAD_DIM:(kh * group + gi + 1) * HEAD_DIM]
                              for gi in range(group)], axis=0)
        return _dot_nt(qg, kb[:, kh * HEAD_DIM:(kh + 1) * HEAD_DIM])

    def softmax(kh, s):
        s = s + bias_s[kh * group:(kh + 1) * group].reshape(group * BLOCK, 3 * BLOCK)
        sk = jnp.zeros((group * BLOCK, 1), F32)
        for gi in range(group):
            sk = jnp.where(row_head == gi, sink_ref[layer, kh * group + gi] * LOG2E, sk)
        m = jnp.maximum(jnp.max(s, axis=-1, keepdims=True), sk)
        p = jnp.exp2(s - m)
        denom = jnp.sum(p, axis=-1, keepdims=True) + jnp.exp2(sk - m)
        return p.astype(BF16), denom

    all_scores = [scores(kh) for kh in range(ATT_KV_HEADS)]
    probs = [softmax(kh, s) for kh, s in enumerate(all_scores)]
    outs = []
    for kh, (p, denom) in enumerate(probs):
        og = _dot(p, vb[:, kh * HEAD_DIM:(kh + 1) * HEAD_DIM]) / denom
        outs += [og[gi * BLOCK:(gi + 1) * BLOCK] for gi in range(group)]
    o = jnp.concatenate(outs, axis=-1)
    o_ref[...] = _rms(o, gain_ref[...]).astype(o_ref.dtype)


def window_attention(qkv, bias, sink, gain, layer, batch, seq_len):
    n = qkv.shape[0]
    nb = seq_len // BLOCK
    kcol = ATT_WIDTH // ATT_KV_WIDTH
    vcol = kcol + 1

    def rows(off):
        def f(b, t):
            return jnp.clip(t + off, 0, nb - 1) + b * nb
        return f

    def kv_spec(col, off):
        f = rows(off)
        return pl.BlockSpec((BLOCK, ATT_KV_WIDTH), lambda b, t: (f(b, t), col))

    return pl.pallas_call(
        functools.partial(_attn_kernel, layer),
        grid=(batch, nb),
        in_specs=[
            pl.BlockSpec((BLOCK, ATT_WIDTH), lambda b, t: (b * nb + t, 0)),
            kv_spec(kcol, -1), kv_spec(kcol, 0), kv_spec(kcol, 1),
            kv_spec(vcol, -1), kv_spec(vcol, 0), kv_spec(vcol, 1),
            pl.BlockSpec((ATT_Q_HEADS, BLOCK, 3 * BLOCK), lambda b, t: (0, 0, 0)),
            pl.BlockSpec(memory_space=pltpu.SMEM),
            _layer_block((1, ATT_WIDTH), layer),
        ],
        out_specs=pl.BlockSpec((BLOCK, ATT_WIDTH), lambda b, t: (b * nb + t, 0)),
        out_shape=jax.ShapeDtypeStruct((n, ATT_WIDTH), BF16),
        scratch_shapes=[pltpu.VMEM((ATT_Q_HEADS, BLOCK, 3 * BLOCK), F32)],
        compiler_params=_cparams(("arbitrary", "arbitrary")),
        name="window_attention",
    )(qkv, qkv, qkv, qkv, qkv, qkv, qkv, bias, sink, gain)


def _t5_bias(rel_bias_table):
    i = jnp.arange(BLOCK)[:, None]
    j = jnp.arange(3 * BLOCK)[None, :]
    rel = j - BLOCK - i
    nbk = N_BUCKETS // 2
    max_exact = nbk // 2
    ret = jnp.where(rel > 0, nbk, 0)
    nabs = jnp.abs(rel)
    large = max_exact + (jnp.log(jnp.maximum(nabs, max_exact).astype(jnp.float32) / max_exact)
                         / math.log(MAX_DISTANCE / max_exact) * (nbk - max_exact)).astype(jnp.int32)
    large = jnp.minimum(large, nbk - 1)
    bucket = ret + jnp.where(nabs < max_exact, nabs, large)
    onehot = (bucket[None, :, :] == jnp.arange(N_BUCKETS)[:, None, None]).astype(F32)
    bias = jnp.einsum("bh,bij->hij", rel_bias_table.astype(F32), onehot, precision=HI)
    return jnp.where((nabs <= WINDOW)[None], bias * LOG2E, NEG_INF)


def _outproj_kernel(yr, ya, w1, w2, h_ref, o_ref):
    o_ref[...] = h_ref[...] + _dot(yr[...], w1[...]) + _dot(ya[...], w2[...])


def out_proj(y_r, y_a, w, layer, h, tm=512):
    n, d = h.shape
    c = y_r.shape[1]
    return pl.pallas_call(
        _outproj_kernel,
        grid=(n // tm,),
        in_specs=[
            pl.BlockSpec((tm, c), lambda i: (i, 0)),
            pl.BlockSpec((tm, c), lambda i: (i, 0)),
            _layer_block((c, d), layer, (0, 0)),
            _layer_block((c, d), layer, (1, 0)),
            pl.BlockSpec((tm, d), lambda i: (i, 0)),
        ],
        out_specs=pl.BlockSpec((tm, d), lambda i: (i, 0)),
        out_shape=jax.ShapeDtypeStruct((n, d), F32),
        compiler_params=_cparams(("parallel",)),
        name="out_proj",
    )(y_r, y_a, w, w, h)


def _xatt_kernel(h_ref, g_ref, wq, kv_ref, wo, gm_ref, wr_ref, br_ref, o_ref, slab_ref):
    h = h_ref[...]
    hn = _rms(h, g_ref[...]).astype(BF16)
    q = (_dot(hn, wq[...]) * (XATT_HEAD_DIM ** -0.5 * LOG2E)).astype(BF16)
    kv = kv_ref[...]
    outs = []
    for hd in range(XATT_HEADS):
        sl = slice(hd * XATT_HEAD_DIM, (hd + 1) * XATT_HEAD_DIM)
        k_h = kv[:, sl].astype(BF16)
        v_h = kv[:, XATT_WIDTH + hd * XATT_HEAD_DIM: XATT_WIDTH + (hd + 1) * XATT_HEAD_DIM].astype(BF16)
        s = _dot_nt(q[:, sl], k_h)
        m = jnp.max(s, axis=-1, keepdims=True)
        p = jnp.exp2(s - m)
        p = p / jnp.sum(p, axis=-1, keepdims=True)
        outs.append(_dot(p.astype(BF16), v_h))
    o = jnp.concatenate(outs, axis=-1).astype(BF16)
    h_new = h + _dot(o, wo[...])
    o_ref[...] = h_new
    slab_ref[...] = _route(h_new, gm_ref[...], wr_ref[...], br_ref[...])


def cross_attention_router(h, g, wq, kv, wo, g_moe, w_r, b_r, layer, batch, seq_len, tm=512):
    n, d = h.shape
    m = kv.shape[0] // batch
    tiles = seq_len // tm
    return pl.pallas_call(
        _xatt_kernel,
        grid=(n // tm,),
        in_specs=[
            pl.BlockSpec((tm, d), lambda i: (i, 0)),
            _layer_block((1, d), layer),
            _layer_block((d, XATT_WIDTH), layer),
            pl.BlockSpec((m, 2 * XATT_WIDTH), lambda i: (i // tiles, 0)),
            _layer_block((XATT_WIDTH, d), layer),
            _layer_block((1, d), layer),
            _layer_block((d, 2 * LANES), layer),
            _layer_block((1, LANES), layer),
        ],
        out_specs=[pl.BlockSpec((tm, d), lambda i: (i, 0)), pl.BlockSpec((tm, LANES), lambda i: (i, 0))],
        out_shape=[jax.ShapeDtypeStruct((n, d), F32), jax.ShapeDtypeStruct((n, LANES), F32)],
        compiler_params=_cparams(("parallel",)),
        name="cross_attention_router",
    )(h, g, wq, kv, wo, g_moe, w_r, b_r)


def _route(h, g, w2, b):
    hn_hi, hn_lo = _split_bf16(_rms(h, g))
    both = _dot(hn_hi, w2) + _dot(hn_lo, w2)
    logits = both[:, 0:LANES] + both[:, LANES:] + b
    tm = logits.shape[0]
    lane = lax.broadcasted_iota(jnp.int32, (tm, LANES), 1)
    is_coarse = (lane >= N_EXPERTS) & (lane < N_EXPERTS + N_GROUPS)
    cl = jnp.where(is_coarse, logits, NEG_INF)
    cmax = jnp.max(cl, axis=-1, keepdims=True)
    csum = jnp.sum(jnp.where(is_coarse, jnp.exp(cl - cmax), 0.0), axis=-1, keepdims=True)
    p_g = 1.0 / csum
    lane_f = lane.astype(F32)
    grp_f = (lane >> 3).astype(F32)
    big = float(LANES)
    g_lane = jnp.min(jnp.where(is_coarse & (cl == cmax), lane_f, big), axis=-1, keepdims=True)
    g_idx = g_lane - float(N_EXPERTS)
    in_grp = (lane < N_EXPERTS) & (grp_f == g_idx)
    fl = jnp.where(in_grp, logits, NEG_INF)
    m1 = jnp.max(fl, axis=-1, keepdims=True)
    i1 = jnp.min(jnp.where(in_grp & (fl == m1), lane_f, big), axis=-1, keepdims=True)
    fl2 = jnp.where(lane_f == i1, NEG_INF, fl)
    m2 = jnp.max(fl2, axis=-1, keepdims=True)
    i2 = jnp.min(jnp.where(in_grp & (lane_f != i1) & (fl2 == m2), lane_f, big), axis=-1, keepdims=True)
    e2 = jnp.exp(m2 - m1)
    w1 = p_g / (1.0 + e2)
    w2 = p_g * e2 / (1.0 + e2)
    return jnp.where(lane == 0, i1,
                     jnp.where(lane == 1, i2,
                               jnp.where(lane == 2, w1, jnp.where(lane == 3, w2, 0.0))))


GATHER_UNROLL = 8


def _start_row_gather(src_hbm, dst, sem, idx_ref, first, stride, n_rows):
    def body(r, carry):
        row = idx_ref[first + r * stride]
        pltpu.make_async_copy(src_hbm.at[pl.ds(row, 1), :], dst.at[pl.ds(r, 1), :], sem).start()
        return carry

    lax.fori_loop(0, n_rows, body, 0, unroll=GATHER_UNROLL)


def _wait_row_gather(dst, sem):
    pltpu.make_async_copy(dst, dst, sem).wait()


def _expert_kernel(e0, te_ref, nxt_ref, par_ref, nused_ref, tok_ref, h_hbm, g_ref, wg_hbm, wu_hbm, wd_hbm, o_ref,
                   xbuf, wgf, wuf, wdf, wgb, wub, wdb, xsem, wsem):
    t = pl.program_id(0)
    nslots, tm = xbuf.shape[0], xbuf.shape[1]
    ahead = nslots - 1
    n_used = nused_ref[0]
    slot = lax.rem(t, nslots)

    def weight_copies(expert, ws):
        return [pltpu.make_async_copy(src.at[e0 + expert], dst.at[ws], wsem.at[ws, j])
                for j, (src, dst) in enumerate(((wg_hbm, wgf), (wu_hbm, wuf), (wd_hbm, wdf)))]

    @pl.when(t == 0)
    def _():
        for cp in weight_copies(te_ref[0], 0):
            cp.start()
        for j in range(ahead):
            @pl.when(j < n_used)
            def _():
                _start_row_gather(h_hbm, xbuf.at[j], xsem.at[j], tok_ref, j * tm, 1, tm)

    @pl.when(t + ahead < n_used)
    def _():
        nslot = lax.rem(t + ahead, nslots)
        _start_row_gather(h_hbm, xbuf.at[nslot], xsem.at[nslot], tok_ref, (t + ahead) * tm, 1, tm)

    @pl.when((t == 0) | (te_ref[t] != te_ref[jnp.maximum(t - 1, 0)]))
    def _():
        ws = par_ref[t]
        for cp in weight_copies(te_ref[t], ws):
            cp.wait()

        @pl.when(nxt_ref[t] >= 0)
        def _():
            for cp in weight_copies(nxt_ref[t], 1 - ws):
                cp.start()

        wgb[...] = wgf[ws].astype(BF16)
        wub[...] = wuf[ws].astype(BF16)
        wdb[...] = wdf[ws].astype(BF16)

    @pl.when(t < n_used)
    def _():
        _wait_row_gather(xbuf.at[slot], xsem.at[slot])
        xn = _rms(xbuf[slot], g_ref[...]).astype(BF16)
        hg = _dot(xn, wgb[...])
        hu = _dot(xn, wub[...])
        act = (hg * _sigmoid(hg) * hu).astype(BF16)
        o_ref[...] = _dot(act, wdb[...])

    @pl.when(t >= n_used)
    def _():
        o_ref[...] = jnp.zeros_like(o_ref)


def moe_experts(h, g, w_gate, w_up, w_down, layer, tables, n_tiles):
    n, d = h.shape
    tm = MOE_TILE
    tile_expert, next_expert, parity, n_used, row_tok = tables
    grid_spec = pltpu.PrefetchScalarGridSpec(
        num_scalar_prefetch=5,
        grid=(n_tiles,),
        in_specs=[
            pl.BlockSpec(memory_space=pl.ANY),
            _layer_block((1, d), layer),
            pl.BlockSpec(memory_space=pl.ANY),
            pl.BlockSpec(memory_space=pl.ANY),
            pl.BlockSpec(memory_space=pl.ANY),
        ],
        out_specs=pl.BlockSpec((tm, d), lambda t, *_: (t, 0)),
        scratch_shapes=[pltpu.VMEM((MOE_ROW_SLOTS, tm, d), F32),
                        pltpu.VMEM((2, d, D_EXPERT), F32), pltpu.VMEM((2, d, D_EXPERT), F32),
                        pltpu.VMEM((2, D_EXPERT, d), F32),
                        pltpu.VMEM((d, D_EXPERT), BF16), pltpu.VMEM((d, D_EXPERT), BF16),
                        pltpu.VMEM((D_EXPERT, d), BF16),
                        pltpu.SemaphoreType.DMA((MOE_ROW_SLOTS,)), pltpu.SemaphoreType.DMA((2, 3))],
    )
    return pl.pallas_call(
        functools.partial(_expert_kernel, layer * N_EXPERTS),
        grid_spec=grid_spec,
        out_shape=jax.ShapeDtypeStruct((n_tiles * tm, d), F32),
        compiler_params=_cparams(("arbitrary",)),
        name="moe_experts",
    )(tile_expert, next_expert, parity, n_used, row_tok, h, g, w_gate, w_up, w_down)


def _combine_kernel(final, pos_ref, y_hbm, h_ref, slab_ref, fg_ref, o_ref, buf, sem):
    i = pl.program_id(0)
    n_steps = pl.num_programs(0)
    tm = h_ref.shape[0]
    slot = i % 2

    def start(step, sl):
        for j in range(2):
            _start_row_gather(y_hbm, buf.at[sl, j], sem.at[sl, j], pos_ref, 2 * step * tm + j, 2, tm)

    @pl.when(i == 0)
    def _():
        start(0, 0)

    @pl.when(i + 1 < n_steps)
    def _():
        start(i + 1, 1 - slot)

    for j in range(2):
        _wait_row_gather(buf.at[slot, j], sem.at[slot, j])
    slab = slab_ref[...]
    out = h_ref[...] + slab[:, 2:3] * buf[slot, 0] + slab[:, 3:4] * buf[slot, 1]
    if final:
        out = _rms(out, fg_ref[...])
    o_ref[...] = out


def moe_combine(y_sorted, pos, h, slab, final_gain, final, tm=256):
    n, d = h.shape
    grid_spec = pltpu.PrefetchScalarGridSpec(
        num_scalar_prefetch=1,
        grid=(n // tm,),
        in_specs=[
            pl.BlockSpec(memory_space=pl.ANY),
            pl.BlockSpec((tm, d), lambda i, p: (i, 0)),
            pl.BlockSpec((tm, LANES), lambda i, p: (i, 0)),
            pl.BlockSpec((1, d), lambda i, p: (0, 0)),
        ],
        out_specs=pl.BlockSpec((tm, d), lambda i, p: (i, 0)),
        scratch_shapes=[pltpu.VMEM((2, 2, tm, d), F32), pltpu.SemaphoreType.DMA((2, 2))],
    )
    return pl.pallas_call(
        functools.partial(_combine_kernel, final),
        grid_spec=grid_spec,
        out_shape=jax.ShapeDtypeStruct((n, d), F32),
        compiler_params=_cparams(("arbitrary",)),
        name="moe_combine",
    )(pos, y_sorted, h, slab, final_gain.reshape(1, d))


def _routing_tables(slab, n_tiles):
    n = slab.shape[0]
    tm = MOE_TILE
    e = slab[:, 0:2].astype(jnp.int32).reshape(-1)
    onehot = (e[:, None] == jnp.arange(N_EXPERTS, dtype=jnp.int32)[None, :]).astype(jnp.int32)
    csum = jnp.cumsum(onehot, axis=0)
    counts = csum[-1]
    rank = jnp.take_along_axis(csum, e[:, None], axis=1)[:, 0] - 1
    tiles_per = (counts + tm - 1) // tm
    tile_end = jnp.cumsum(tiles_per)
    tile_start = tile_end - tiles_per
    pos = tile_start[e] * tm + rank
    row_tok = jnp.zeros((n_tiles * tm,), jnp.int32).at[pos].set(jnp.arange(2 * n, dtype=jnp.int32) // 2)
    n_used = tile_end[-1]
    t_idx = jnp.arange(n_tiles, dtype=jnp.int32)
    tile_expert = jnp.searchsorted(tile_end, jnp.minimum(t_idx, n_used - 1), side="right").astype(jnp.int32)
    tile_expert = jnp.minimum(tile_expert, N_EXPERTS - 1)
    after = tile_end[tile_expert]
    next_expert = jnp.where(after < n_used, tile_expert[jnp.minimum(after, n_tiles - 1)], -1).astype(jnp.int32)
    change = jnp.concatenate([jnp.zeros((1,), jnp.int32), (tile_expert[1:] != tile_expert[:-1]).astype(jnp.int32)])
    parity = (jnp.cumsum(change) % 2).astype(jnp.int32)
    tables = (tile_expert, next_expert, parity, n_used.reshape(1).astype(jnp.int32), row_tok)
    return tables, pos.astype(jnp.int32)


def _split_in_weights(w):
    c = C_RWKV
    o = 3 * c
    wd0 = w[..., o:o + DECAY_RANK]
    wd1 = w[..., o + DECAY_RANK:o + 2 * DECAY_RANK]
    o += 2 * DECAY_RANK
    ad0 = w[..., o:o + ICLR_RANK]
    ad1 = w[..., o + ICLR_RANK:o + 2 * ICLR_RANK]
    o += 2 * ICLR_RANK
    gd = w[..., o:o + GATE_RANK]
    o += GATE_RANK
    att = w[..., o:]

    def padl(x):
        pad = [(0, 0)] * (x.ndim - 1) + [(0, RANK_PAD - x.shape[-1])]
        return jnp.pad(x, pad)

    rw = jnp.concatenate([w[..., :3 * c], padl(wd0), padl(wd1), padl(ad0), padl(ad1), gd], axis=-1)
    return rw, att


def kernel(x, mem, w_in, shift_prev, shift_next, decay_w0, decay_w2, iclr_a0, iclr_a2, gate_w2, vres_v0, vres_w1, vres_w2, k_k, k_a, r_k, ln_x_gain, ln_x_bias, att_sink, att_out_gain, rel_bias_table, w_out, norm_mix, norm_xatt, mem_norm, xatt_wq, xatt_wk, xatt_wv, xatt_wo, norm_moe, router_coarse_w, router_coarse_b, router_fine_w, router_fine_b, expert_w_gate, expert_w_up, expert_w_down, final_norm):
    batch, seq_len, d = x.shape
    depth = w_in.shape[0]
    n = batch * seq_len
    mem_len = mem.shape[1]
    h = x.reshape(n, d)
    memf = mem.reshape(batch * mem_len, d)
    bias = _t5_bias(rel_bias_table)
    n_tiles = (2 * n) // MOE_TILE + N_EXPERTS

    def row(p):
        return p[:, None, :]

    def pad_axis(p, axis, size):
        pad = [(0, 0)] * p.ndim
        pad[axis] = (0, size - p.shape[axis])
        return jnp.pad(p, pad)

    w_rw, w_at = _split_in_weights(w_in)
    w_in_all = jnp.concatenate([w_rw, w_at], axis=-1).astype(BF16)
    prep_params = (
        row(_split_in_weights(shift_prev)[0]), row(_split_in_weights(shift_next)[0]),
        decay_w0, pad_axis(decay_w2, 2, RANK_PAD).astype(BF16),
        iclr_a0, pad_axis(iclr_a2, 2, RANK_PAD).astype(BF16),
        gate_w2.astype(BF16), row(k_k), row(k_a), row(r_k))
    vres_params = (row(vres_v0), pad_axis(vres_w1, 2, RANK_PAD).astype(BF16),
                   pad_axis(vres_w2, 1, RANK_PAD).astype(BF16))
    ln_g, ln_b = row(ln_x_gain), row(ln_x_bias)
    att_gain = row(att_out_gain)
    w_out_all = w_out.astype(BF16)
    g_mix, g_xatt, g_mem, g_moe = row(norm_mix), row(norm_xatt), row(mem_norm), row(norm_moe)
    wq_all = xatt_wq.astype(BF16)
    wkv_all = jnp.concatenate([xatt_wk, xatt_wv], axis=-1).astype(BF16)
    wo_all = xatt_wo.astype(BF16)
    w_r = pad_axis(jnp.concatenate([router_fine_w, router_coarse_w], axis=-1), 2, LANES)
    w_r2_all = jnp.concatenate(_split_bf16(w_r), axis=-1)
    b_r_all = row(pad_axis(jnp.concatenate([router_fine_b.reshape(depth, -1), router_coarse_b], axis=-1), 1, LANES))
    wg_all = expert_w_gate.reshape(depth * N_EXPERTS, d, D_EXPERT)
    wu_all = expert_w_up.reshape(depth * N_EXPERTS, d, D_EXPERT)
    wd_all = expert_w_down.reshape(depth * N_EXPERTS, D_EXPERT, d)

    v_first = None
    for l in range(depth):
        u, qkv = in_proj(h, g_mix, w_in_all, l, RW_COLS)
        r, v, kk, b0, b1, lw0, lw1, kd0, kd1, g, bg = rwkv_prep(
            u, seq_len, l, prep_params, vres_params if l > 0 else None, v_first)
        if l == 0:
            v_first = v
        yf, yb = rwkv_chunk(r, v, kk, b0, b1, lw0, lw1, kd0, kd1, batch, seq_len)
        y_r = rwkv_post(yf, yb, g, bg, ln_g, ln_b, l)
        y_a = window_attention(qkv, bias, att_sink, att_gain, l, batch, seq_len)
        h = out_proj(y_r, y_a, w_out_all, l, h)
        kv = norm_matmul(memf, g_mem, wkv_all, l, memf.shape[0], 2 * XATT_WIDTH)
        h, slab = cross_attention_router(h, g_xatt, wq_all, kv, wo_all, g_moe, w_r2_all, b_r_all, l, batch, seq_len)
        tables, pos = _routing_tables(slab, n_tiles)
        y_sorted = moe_experts(h, g_moe, wg_all, wu_all, wd_all, l, tables, n_tiles)
        h = moe_combine(y_sorted, pos, h, slab, final_norm, l == depth - 1)
    return h.reshape(batch, seq_len, d)
```

```python
import functools
import math

import jax
import jax.numpy as jnp
from jax import lax
from jax.experimental import pallas as pl
from jax.experimental.pallas import tpu as pltpu

F32 = jnp.float32
BF16 = jnp.bfloat16
HI = lax.Precision.HIGHEST

D_MODEL = 2048
RWKV_HEADS = 16
HEAD_DIM = 64
C_RWKV = RWKV_HEADS * HEAD_DIM
ATT_Q_HEADS = 16
ATT_KV_HEADS = 4
ATT_WIDTH = ATT_Q_HEADS * HEAD_DIM
ATT_KV_WIDTH = ATT_KV_HEADS * HEAD_DIM
WINDOW = 128
BLOCK = 128
DECAY_RANK = 96
ICLR_RANK = 96
VRES_RANK = 64
GATE_RANK = 256
GN_EPS = 64e-5
N_BUCKETS = 32
MAX_DISTANCE = 128
XATT_HEADS = 4
XATT_HEAD_DIM = 128
XATT_WIDTH = XATT_HEADS * XATT_HEAD_DIM
N_GROUPS = 4
EXPERTS_PER_GROUP = 8
N_EXPERTS = N_GROUPS * EXPERTS_PER_GROUP
D_EXPERT = 512
EPS = 1e-6
NEG_INF = -1e30
LOG2E = math.log2(math.e)

LANES = 128
RANK_PAD = 128
R0, K0, V0 = 0, C_RWKV, 2 * C_RWKV
WD0 = 3 * C_RWKV
WD1 = WD0 + RANK_PAD
AD0 = WD1 + RANK_PAD
AD1 = AD0 + RANK_PAD
GD0 = AD1 + RANK_PAD
RW_COLS = GD0 + GATE_RANK
AT_COLS = ATT_WIDTH + 2 * ATT_KV_WIDTH

CHUNK = 64
MOE_TILE = 256
MOE_ROW_SLOTS = 4
VMEM_LIMIT = 56 * 1024 * 1024


def _cparams(sem, vmem=VMEM_LIMIT):
    return pltpu.CompilerParams(dimension_semantics=sem, vmem_limit_bytes=vmem)


def _sigmoid(x):
    return 1.0 / (1.0 + jnp.exp(-x))


def _dot(a, b, prec=None):
    return jnp.dot(a, b, preferred_element_type=F32, precision=prec)


def _dot_nt(a, b, prec=None):
    return lax.dot_general(a, b, (((1,), (1,)), ((), ())), preferred_element_type=F32, precision=prec)


def _dot_tn(a, b, prec=None):
    return lax.dot_general(a, b, (((0,), (0,)), ((), ())), preferred_element_type=F32, precision=prec)


def _rms(x, g):
    ms = jnp.mean(x * x, axis=-1, keepdims=True)
    return x * lax.rsqrt(ms + EPS) * g


def _split_bf16(x):
    hi = x.astype(BF16)
    lo = (x - hi.astype(F32)).astype(BF16)
    return hi, lo


def _head_block_ones():
    ri = lax.broadcasted_iota(jnp.int32, (LANES, LANES), 0)
    ci = lax.broadcasted_iota(jnp.int32, (LANES, LANES), 1)
    return ((ri >> 6) == (ci >> 6)).astype(BF16)


def _head_sums(x, bd2):
    rows = x.shape[0]
    hi, lo = _split_bf16(x)
    outs = []
    for gi in range(x.shape[1] // LANES):
        sl = slice(gi * LANES, (gi + 1) * LANES)
        both = _dot(jnp.concatenate([hi[:, sl], lo[:, sl]], axis=0), bd2)
        outs.append(both[0:rows] + both[rows:])
    return outs[0] if len(outs) == 1 else jnp.concatenate(outs, axis=1)


def _layer_block(shape, layer, index=None):
    index = (0,) * len(shape) if index is None else index
    return pl.BlockSpec((None,) + tuple(shape), lambda *_: (layer,) + tuple(index), pipeline_mode=pl.Buffered(1))


def _norm_mm_kernel(x_ref, g_ref, w_ref, o_ref, xn_ref):
    @pl.when(pl.program_id(1) == 0)
    def _():
        xn_ref[...] = _rms(x_ref[...], g_ref[...]).astype(BF16)

    o_ref[...] = _dot(xn_ref[...], w_ref[...]).astype(o_ref.dtype)


def norm_matmul(x, g, w, layer, tm, tn, out_dtype=F32):
    n, d = x.shape
    nc = w.shape[2]
    return pl.pallas_call(
        _norm_mm_kernel,
        grid=(n // tm, nc // tn),
        in_specs=[
            pl.BlockSpec((tm, d), lambda i, j: (i, 0)),
            pl.BlockSpec((None, 1, d), lambda i, j: (layer, 0, 0)),
            pl.BlockSpec((None, d, tn), lambda i, j: (layer, 0, j)),
        ],
        out_specs=pl.BlockSpec((tm, tn), lambda i, j: (i, j)),
        out_shape=jax.ShapeDtypeStruct((n, nc), out_dtype),
        scratch_shapes=[pltpu.VMEM((tm, d), BF16)],
        compiler_params=_cparams(("parallel", "arbitrary")),
        name="norm_matmul",
    )(x, g, w)


def _in_proj_kernel(x_ref, g_ref, w_ref, u_ref, qkv_ref):
    xn = _rms(x_ref[...], g_ref[...]).astype(BF16)
    nu = u_ref.shape[1]
    u_ref[...] = _dot(xn, w_ref[:, 0:nu])
    qkv_ref[...] = _dot(xn, w_ref[:, nu:])


def in_proj(x, g, w, layer, n_rw, tm=512):
    n, d = x.shape
    nc = w.shape[2]
    return pl.pallas_call(
        _in_proj_kernel,
        grid=(n // tm,),
        in_specs=[
            pl.BlockSpec((tm, d), lambda i: (i, 0)),
            _layer_block((1, d), layer),
            _layer_block((d, nc), layer),
        ],
        out_specs=[pl.BlockSpec((tm, n_rw), lambda i: (i, 0)), pl.BlockSpec((tm, nc - n_rw), lambda i: (i, 0))],
        out_shape=[jax.ShapeDtypeStruct((n, n_rw), F32), jax.ShapeDtypeStruct((n, nc - n_rw), F32)],
        compiler_params=_cparams(("parallel",)),
        name="in_proj",
    )(x, g, w)


def _prep_kernel(seq_tiles, has_vres, *refs):
    if has_vres:
        (u_ref, hp_ref, hn_ref, mup_ref, mun_ref, w0_ref, w2_ref, a0_ref, a2_ref, gw_ref, kk_ref, ka_ref, rk_ref,
         vf_ref, v0_ref, v1_ref, v2_ref,
         r_o, v_o, kk_o, b0_o, b1_o, lw0_o, lw1_o, kd0_o, kd1_o, g_o, bg_o) = refs
    else:
        (u_ref, hp_ref, hn_ref, mup_ref, mun_ref, w0_ref, w2_ref, a0_ref, a2_ref, gw_ref, kk_ref, ka_ref, rk_ref,
         r_o, v_o, kk_o, b0_o, b1_o, lw0_o, lw1_o, kd0_o, kd1_o, g_o, bg_o) = refs
    tm = u_ref.shape[0]
    i = pl.program_id(0)
    it = i % seq_tiles
    has_prev = jnp.where(it != 0, 1.0, 0.0)
    has_next = jnp.where(it != seq_tiles - 1, 1.0, 0.0)
    row8 = lax.broadcasted_iota(jnp.int32, (8, 1), 0)

    def shifted(c0, c1):
        u = u_ref[:, c0:c1]
        pr = hp_ref[7:8, c0:c1] * has_prev
        nx = hn_ref[0:1, c0:c1] * has_next
        prev = pltpu.roll(u, 1, 0)
        nxt = pltpu.roll(u, tm - 1, 0)
        prev = jnp.concatenate([jnp.where(row8 == 0, pr, prev[0:8]), prev[8:]], axis=0)
        nxt = jnp.concatenate([nxt[:tm - 8], jnp.where(row8 == 7, nx, nxt[tm - 8:])], axis=0)
        mp = mup_ref[:, c0:c1]
        mn = mun_ref[:, c0:c1]
        return u * (1.0 - mp - mn) + mp * prev + mn * nxt

    bd2 = _head_block_ones()

    r = shifted(R0, R0 + C_RWKV)
    k = shifted(K0, K0 + C_RWKV)
    v = shifted(V0, V0 + C_RWKV)
    r_o[...] = r.astype(r_o.dtype)
    if has_vres:
        low = _dot(v.astype(BF16), v1_ref[...])
        mix = _sigmoid(v0_ref[...] + _dot(low.astype(BF16), v2_ref[...]))
        v = v + (vf_ref[...].astype(F32) - v) * mix
    v_o[...] = v.astype(v_o.dtype)
    kkraw = k * kk_ref[...]
    kk = kkraw / jnp.maximum(jnp.sqrt(_head_sums(kkraw * kkraw, bd2)), 1e-12)
    kk_o[...] = kk.astype(kk_o.dtype)
    ka = ka_ref[...]
    lw_scale = -math.exp(-0.5)
    kd_sum = None
    for z, (wc, ac, lw_o, b_o, kd_o) in enumerate(
            ((WD0, AD0, lw0_o, b0_o, kd0_o), (WD1, AD1, lw1_o, b1_o, kd1_o))):
        wd = shifted(wc, wc + RANK_PAD)
        w_pre = w0_ref[z:z + 1, :] + _dot(jnp.tanh(wd).astype(BF16), w2_ref[z])
        lw_o[...] = lw_scale * _sigmoid(w_pre)
        ad = shifted(ac, ac + RANK_PAD)
        a = _sigmoid(a0_ref[z:z + 1, :] + _dot(ad.astype(BF16), a2_ref[z]))
        b_o[...] = (kk * a).astype(b_o.dtype)
        kd = k * (1.0 + (a - 1.0) * ka)
        kd_o[...] = kd.astype(kd_o.dtype)
        kd_sum = kd if kd_sum is None else kd_sum + kd
    gd = shifted(GD0, GD0 + GATE_RANK)
    g = _dot(_sigmoid(gd).astype(BF16), gw_ref[...])
    g_o[...] = g.astype(g_o.dtype)
    bonus = _head_sums(r * kd_sum * rk_ref[...], bd2) * v
    bg_o[...] = (bonus * g).astype(bg_o.dtype)


def rwkv_prep(u, seq_len, layer, params, vres, v_first, tm=256):
    n = u.shape[0]
    seq_tiles = seq_len // tm
    hb = tm // 8
    nblk8 = n // 8
    has_vres = vres is not None
    c = C_RWKV
    in_specs = [
        pl.BlockSpec((tm, RW_COLS), lambda i: (i, 0)),
        pl.BlockSpec((8, RW_COLS), lambda i: (jnp.maximum(i * hb - 1, 0), 0)),
        pl.BlockSpec((8, RW_COLS), lambda i: (jnp.minimum((i + 1) * hb, nblk8 - 1), 0)),
    ] + [_layer_block(p.shape[1:], layer) for p in params]
    args = [u, u, u] + list(params)
    if has_vres:
        in_specs += [pl.BlockSpec((tm, c), lambda i: (i, 0))] + [_layer_block(p.shape[1:], layer - 1) for p in vres]
        args += [v_first] + list(vres)
    out_spec = pl.BlockSpec((tm, c), lambda i: (i, 0))
    dtypes = [BF16] * 5 + [F32] * 2 + [BF16] * 4
    return pl.pallas_call(
        functools.partial(_prep_kernel, seq_tiles, has_vres),
        grid=(n // tm,),
        in_specs=in_specs,
        out_specs=[out_spec] * len(dtypes),
        out_shape=[jax.ShapeDtypeStruct((n, c), dt) for dt in dtypes],
        compiler_params=_cparams(("parallel",)),
        name="rwkv_prep",
    )(*args)


def _chunk_kernel(rf, vf, kkf, af, lwf, kdf, rb, vb, kkb, ab, lwb, kdb, yf_o, yb_o, s_ref):
    nbatch, L, width = rf.shape
    L2 = 2 * L
    npairs = width // LANES

    @pl.when(pl.program_id(0) == 0)
    def _():
        s_ref[...] = jnp.zeros_like(s_ref)

    lane = lax.broadcasted_iota(jnp.int32, (1, LANES), 1)
    m1 = lane < HEAD_DIM
    t_i = lax.broadcasted_iota(jnp.int32, (L, L2), 0)
    s_i = lax.broadcasted_iota(jnp.int32, (L, L2), 1) & (L - 1)
    ri = lax.broadcasted_iota(jnp.int32, (LANES, LANES), 0)
    ci = lax.broadcasted_iota(jnp.int32, (LANES, LANES), 1)
    same_head = (ri >> 6) == (ci >> 6)

    def stack(x):
        z = jnp.zeros_like(x)
        return jnp.concatenate([jnp.where(m1, x, z), jnp.where(m1, z, x)], axis=0)

    chains = []
    for rev, ins in ((False, (rf, vf, kkf, af, lwf, kdf)), (True, (rb, vb, kkb, ab, lwb, kdb))):
        strict = (s_i > t_i) if rev else (s_i < t_i)
        incl = (s_i >= t_i) if rev else (s_i <= t_i)
        incl4 = jnp.concatenate([incl, incl], axis=1)
        tri2 = incl.astype(BF16)
        for bi in range(nbatch):
            r, vb16, kk, bn, lw, kd = (x[bi] for x in ins)
            r, kk, bn, kd = (x.astype(F32) for x in (r, kk, bn, kd))
            c = _dot(tri2, jnp.concatenate(_split_bf16(lw), axis=0))
            c_last = c[0:1, :] if rev else c[L - 1:L, :]
            decay_last = jnp.exp(c_last)
            e_pos = jnp.exp(c)
            e_neg = jnp.exp(-c)
            e_rem = jnp.exp(c_last - c)
            a_t = (-kk * jnp.exp(c - lw)).astype(BF16)
            b_t = (bn * e_neg).astype(BF16)
            k_t = (kd * e_neg).astype(BF16)
            r_t = (r * e_pos).astype(BF16)
            b_h = (bn * e_rem).astype(BF16)
            k_h = (kd * e_rem).astype(BF16)
            for pr in range(npairs):
                sl = slice(pr * LANES, (pr + 1) * LANES)
                chains.append(dict(
                    strict=strict, incl4=incl4, s_ref=s_ref.at[1 if rev else 0, bi, pr],
                    decay_last=decay_last[:, sl],
                    ar=jnp.concatenate([a_t[:, sl], r_t[:, sl]], axis=0),
                    bk=jnp.concatenate([stack(b_t[:, sl]), stack(k_t[:, sl])], axis=0),
                    bkh=jnp.concatenate([b_h[:, sl], k_h[:, sl]], axis=0),
                    v=vb16[:, sl], v_s=stack(vb16[:, sl])))
    for ch in chains:
        ch["s0"] = ch["s_ref"][...]
        ch["sc"] = _dot_nt(ch["ar"], ch["bk"])
    for ch in chains:
        ch["from_state"] = _dot_nt(ch["ar"], ch["s0"].astype(BF16))
    for ch in chains:
        sc = ch["sc"]
        m_ak = jnp.where(ch["strict"], sc[0:L, L2:], 0.0).astype(BF16)
        ch["m_r"] = jnp.where(ch["incl4"], sc[L:, :], 0.0).astype(BF16)
        ch["p"] = jnp.where(ch["strict"], sc[0:L, 0:L2], 0.0)
        ch["x"] = ch["from_state"][0:L] + _dot(m_ak, ch["v_s"])
    n_steps = int(math.log2(L))
    for step in range(n_steps):
        for ch in chains:
            pb = ch["p"].astype(BF16)
            xs = stack(ch["x"].astype(BF16))
            if step < n_steps - 1:
                res = _dot(pb, jnp.concatenate([xs, stack(pb)], axis=1))
                ch["x"] = ch["x"] + res[:, 0:LANES]
                ch["p"] = res[:, LANES:]
            else:
                ch["x"] = ch["x"] + _dot(pb, xs)
    for ch in chains:
        ub = ch["x"].astype(BF16)
        ch["ub"] = ub
        ch["y"] = ch["from_state"][L:] + _dot(ch["m_r"], jnp.concatenate([stack(ub), ch["v_s"]], axis=0))
    for ch in chains:
        upd = _dot_tn(jnp.concatenate([ch["ub"], ch["v"]], axis=0), ch["bkh"])
        ch["s_ref"][...] = ch["s0"] * ch["decay_last"] + jnp.where(same_head, upd, 0.0)
    for di, y_o in enumerate((yf_o, yb_o)):
        for bi in range(nbatch):
            group = chains[(di * nbatch + bi) * npairs:(di * nbatch + bi + 1) * npairs]
            y_o[bi] = jnp.concatenate([ch["y"] for ch in group], axis=1).astype(y_o.dtype)


def rwkv_chunk(r, v, kk, b0, b1, lw0, lw1, kd0, kd1, batch, seq_len):
    c = C_RWKV
    L = CHUNK
    nc = seq_len // L

    def r3(x):
        return x.reshape(batch, seq_len, c)

    fwd = pl.BlockSpec((batch, L, c), lambda t: (0, t, 0))
    bwd = pl.BlockSpec((batch, L, c), lambda t: (0, nc - 1 - t, 0))
    sds = jax.ShapeDtypeStruct((batch, seq_len, c), BF16)
    yf, yb = pl.pallas_call(
        _chunk_kernel,
        grid=(nc,),
        in_specs=[fwd] * 6 + [bwd] * 6,
        out_specs=[fwd, bwd],
        out_shape=[sds, sds],
        scratch_shapes=[pltpu.VMEM((2, batch, c // LANES, LANES, LANES), F32)],
        compiler_params=_cparams(("arbitrary",)),
        name="rwkv_chunk",
    )(r3(r), r3(v), r3(kk), r3(b0), r3(lw0), r3(kd0), r3(r), r3(v), r3(kk), r3(b1), r3(lw1), r3(kd1))
    return yf.reshape(-1, c), yb.reshape(-1, c)


def _post_kernel(yf, yb, g, bg, lng, lnb, o_ref):
    bd2 = _head_block_ones()
    y = yf[...].astype(F32) + yb[...].astype(F32)
    mu = _head_sums(y, bd2) * (1.0 / HEAD_DIM)
    d = y - mu
    var = _head_sums(d * d, bd2) * (1.0 / HEAD_DIM)
    yn = d * lax.rsqrt(var + GN_EPS) * lng[...] + lnb[...]
    o_ref[...] = (yn * g[...].astype(F32) + bg[...].astype(F32)).astype(o_ref.dtype)


def rwkv_post(yf, yb, g, bg, ln_g, ln_b, layer, tm=512):
    n, c = yf.shape
    blk = pl.BlockSpec((tm, c), lambda i: (i, 0))
    vec = _layer_block((1, c), layer)
    return pl.pallas_call(
        _post_kernel,
        grid=(n // tm,),
        in_specs=[blk] * 4 + [vec] * 2,
        out_specs=blk,
        out_shape=jax.ShapeDtypeStruct((n, c), BF16),
        compiler_params=_cparams(("parallel",)),
        name="rwkv_post",
    )(yf, yb, g, bg, ln_g, ln_b)


def _attn_kernel(layer, q_ref, kp, kc, kn, vp, vc, vn, bias_ref, sink_ref, gain_ref, o_ref, bias_s):
    n = pl.program_id(1)
    nb = pl.num_programs(1)
    j = lax.broadcasted_iota(jnp.int32, (1, 1, 3 * BLOCK), 2)
    off_seq = ((j < BLOCK) & (n == 0)) | ((j >= 2 * BLOCK) & (n == nb - 1))

    @pl.when((n == 0) | (n == 1) | (n == nb - 1))
    def _():
        bias_s[...] = bias_ref[...] + jnp.where(off_seq, NEG_INF, 0.0)

    kb = jnp.concatenate([kp[...], kc[...], kn[...]], axis=0).astype(BF16)
    vb = jnp.concatenate([vp[...], vc[...], vn[...]], axis=0).astype(BF16)
    q = (q_ref[...] * (HEAD_DIM ** -0.5 * LOG2E)).astype(BF16)
    group = ATT_Q_HEADS // ATT_KV_HEADS
    row_head = lax.broadcasted_iota(jnp.int32, (group * BLOCK, 1), 0) >> 7

    def scores(kh):
        qg = jnp.concatenate([q[:, (kh * group + gi) * HEAD_DIM:(kh * group + gi + 1) * HEAD_DIM]
                              for gi in range(group)], axis=0)
        return _dot_nt(qg, kb[:, kh * HEAD_DIM:(kh + 1) * HEAD_DIM])

    def softmax(kh, s):
        s = s + bias_s[kh * group:(kh + 1) * group].reshape(group * BLOCK, 3 * BLOCK)
        sk = jnp.zeros((group * BLOCK, 1), F32)
        for gi in range(group):
            sk = jnp.where(row_head == gi, sink_ref[layer, kh * group + gi] * LOG2E, sk)
        m = jnp.maximum(jnp.max(s, axis=-1, keepdims=True), sk)
        p = jnp.exp2(s - m)
        denom = jnp.sum(p, axis=-1, keepdims=True) + jnp.exp2(sk - m)
        return p.astype(BF16), denom

    all_scores = [scores(kh) for kh in range(ATT_KV_HEADS)]
    probs = [softmax(kh, s) for kh, s in enumerate(all_scores)]
    outs = []
    for kh, (p, denom) in enumerate(probs):
        og = _dot(p, vb[:, kh * HEAD_DIM:(kh + 1) * HEAD_DIM]) / denom
        outs += [og[gi * BLOCK:(gi + 1) * BLOCK] for gi in range(group)]
    o = jnp.concatenate(outs, axis=-1)
    o_ref[...] = _rms(o, gain_ref[...]).astype(o_ref.dtype)


def window_attention(qkv, bias, sink, gain, layer, batch, seq_len):
    n = qkv.shape[0]
    nb = seq_len // BLOCK
    kcol = ATT_WIDTH // ATT_KV_WIDTH
    vcol = kcol + 1

    def rows(off):
        def f(b, t):
            return jnp.clip(t + off, 0, nb - 1) + b * nb
        return f

    def kv_spec(col, off):
        f = rows(off)
        return pl.BlockSpec((BLOCK, ATT_KV_WIDTH), lambda b, t: (f(b, t), col))

    return pl.pallas_call(
        functools.partial(_attn_kernel, layer),
        grid=(batch, nb),
        in_specs=[
            pl.BlockSpec((BLOCK, ATT_WIDTH), lambda b, t: (b * nb + t, 0)),
            kv_spec(kcol, -1), kv_spec(kcol, 0), kv_spec(kcol, 1),
            kv_spec(vcol, -1), kv_spec(vcol, 0), kv_spec(vcol, 1),
            pl.BlockSpec((ATT_Q_HEADS, BLOCK, 3 * BLOCK), lambda b, t: (0, 0, 0)),
            pl.BlockSpec(memory_space=pltpu.SMEM),
            _layer_block((1, ATT_WIDTH), layer),
        ],
        out_specs=pl.BlockSpec((BLOCK, ATT_WIDTH), lambda b, t: (b * nb + t, 0)),
        out_shape=jax.ShapeDtypeStruct((n, ATT_WIDTH), BF16),
        scratch_shapes=[pltpu.VMEM((ATT_Q_HEADS, BLOCK, 3 * BLOCK), F32)],
        compiler_params=_cparams(("arbitrary", "arbitrary")),
        name="window_attention",
    )(qkv, qkv, qkv, qkv, qkv, qkv, qkv, bias, sink, gain)


def _t5_bias(rel_bias_table):
    i = jnp.arange(BLOCK)[:, None]
    j = jnp.arange(3 * BLOCK)[None, :]
    rel = j - BLOCK - i
    nbk = N_BUCKETS // 2
    max_exact = nbk // 2
    ret = jnp.where(rel > 0, nbk, 0)
    nabs = jnp.abs(rel)
    large = max_exact + (jnp.log(jnp.maximum(nabs, max_exact).astype(jnp.float32) / max_exact)
                         / math.log(MAX_DISTANCE / max_exact) * (nbk - max_exact)).astype(jnp.int32)
    large = jnp.minimum(large, nbk - 1)
    bucket = ret + jnp.where(nabs < max_exact, nabs, large)
    onehot = (bucket[None, :, :] == jnp.arange(N_BUCKETS)[:, None, None]).astype(F32)
    bias = jnp.einsum("bh,bij->hij", rel_bias_table.astype(F32), onehot, precision=HI)
    return jnp.where((nabs <= WINDOW)[None], bias * LOG2E, NEG_INF)


def _outproj_kernel(yr, ya, w1, w2, h_ref, o_ref):
    o_ref[...] = h_ref[...] + _dot(yr[...], w1[...]) + _dot(ya[...], w2[...])


def out_proj(y_r, y_a, w, layer, h, tm=512):
    n, d = h.shape
    c = y_r.shape[1]
    return pl.pallas_call(
        _outproj_kernel,
        grid=(n // tm,),
        in_specs=[
            pl.BlockSpec((tm, c), lambda i: (i, 0)),
            pl.BlockSpec((tm, c), lambda i: (i, 0)),
            _layer_block((c, d), layer, (0, 0)),
            _layer_block((c, d), layer, (1, 0)),
            pl.BlockSpec((tm, d), lambda i: (i, 0)),
        ],
        out_specs=pl.BlockSpec((tm, d), lambda i: (i, 0)),
        out_shape=jax.ShapeDtypeStruct((n, d), F32),
        compiler_params=_cparams(("parallel",)),
        name="out_proj",
    )(y_r, y_a, w, w, h)


def _xatt_kernel(h_ref, g_ref, wq, kv_ref, wo, gm_ref, wr_ref, br_ref, o_ref, slab_ref):
    h = h_ref[...]
    hn = _rms(h, g_ref[...]).astype(BF16)
    q = (_dot(hn, wq[...]) * (XATT_HEAD_DIM ** -0.5 * LOG2E)).astype(BF16)
    kv = kv_ref[...]
    outs = []
    for hd in range(XATT_HEADS):
        sl = slice(hd * XATT_HEAD_DIM, (hd + 1) * XATT_HEAD_DIM)
        k_h = kv[:, sl].astype(BF16)
        v_h = kv[:, XATT_WIDTH + hd * XATT_HEAD_DIM: XATT_WIDTH + (hd + 1) * XATT_HEAD_DIM].astype(BF16)
        s = _dot_nt(q[:, sl], k_h)
        m = jnp.max(s, axis=-1, keepdims=True)
        p = jnp.exp2(s - m)
        p = p / jnp.sum(p, axis=-1, keepdims=True)
        outs.append(_dot(p.astype(BF16), v_h))
    o = jnp.concatenate(outs, axis=-1).astype(BF16)
    h_new = h + _dot(o, wo[...])
    o_ref[...] = h_new
    slab_ref[...] = _route(h_new, gm_ref[...], wr_ref[...], br_ref[...])


def cross_attention_router(h, g, wq, kv, wo, g_moe, w_r, b_r, layer, batch, seq_len, tm=512):
    n, d = h.shape
    m = kv.shape[0] // batch
    tiles = seq_len // tm
    return pl.pallas_call(
        _xatt_kernel,
        grid=(n // tm,),
        in_specs=[
            pl.BlockSpec((tm, d), lambda i: (i, 0)),
            _layer_block((1, d), layer),
            _layer_block((d, XATT_WIDTH), layer),
            pl.BlockSpec((m, 2 * XATT_WIDTH), lambda i: (i // tiles, 0)),
            _layer_block((XATT_WIDTH, d), layer),
            _layer_block((1, d), layer),
            _layer_block((d, 2 * LANES), layer),
            _layer_block((1, LANES), layer),
        ],
        out_specs=[pl.BlockSpec((tm, d), lambda i: (i, 0)), pl.BlockSpec((tm, LANES), lambda i: (i, 0))],
        out_shape=[jax.ShapeDtypeStruct((n, d), F32), jax.ShapeDtypeStruct((n, LANES), F32)],
        compiler_params=_cparams(("parallel",)),
        name="cross_attention_router",
    )(h, g, wq, kv, wo, g_moe, w_r, b_r)


def _route(h, g, w2, b):
    hn_hi, hn_lo = _split_bf16(_rms(h, g))
    both = _dot(hn_hi, w2) + _dot(hn_lo, w2)
    logits = both[:, 0:LANES] + both[:, LANES:] + b
    tm = logits.shape[0]
    lane = lax.broadcasted_iota(jnp.int32, (tm, LANES), 1)
    is_coarse = (lane >= N_EXPERTS) & (lane < N_EXPERTS + N_GROUPS)
    cl = jnp.where(is_coarse, logits, NEG_INF)
    cmax = jnp.max(cl, axis=-1, keepdims=True)
    csum = jnp.sum(jnp.where(is_coarse, jnp.exp(cl - cmax), 0.0), axis=-1, keepdims=True)
    p_g = 1.0 / csum
    lane_f = lane.astype(F32)
    grp_f = (lane >> 3).astype(F32)
    big = float(LANES)
    g_lane = jnp.min(jnp.where(is_coarse & (cl == cmax), lane_f, big), axis=-1, keepdims=True)
    g_idx = g_lane - float(N_EXPERTS)
    in_grp = (lane < N_EXPERTS) & (grp_f == g_idx)
    fl = jnp.where(in_grp, logits, NEG_INF)
    m1 = jnp.max(fl, axis=-1, keepdims=True)
    i1 = jnp.min(jnp.where(in_grp & (fl == m1), lane_f, big), axis=-1, keepdims=True)
    fl2 = jnp.where(lane_f == i1, NEG_INF, fl)
    m2 = jnp.max(fl2, axis=-1, keepdims=True)
    i2 = jnp.min(jnp.where(in_grp & (lane_f != i1) & (fl2 == m2), lane_f, big), axis=-1, keepdims=True)
    e2 = jnp.exp(m2 - m1)
    w1 = p_g / (1.0 + e2)
    w2 = p_g * e2 / (1.0 + e2)
    return jnp.where(lane == 0, i1,
                     jnp.where(lane == 1, i2,
                               jnp.where(lane == 2, w1, jnp.where(lane == 3, w2, 0.0))))


GATHER_UNROLL = 8


def _start_row_gather(src_hbm, dst, sem, idx_ref, first, stride, n_rows):
    def body(r, carry):
        row = idx_ref[first + r * stride]
        pltpu.make_async_copy(src_hbm.at[pl.ds(row, 1), :], dst.at[pl.ds(r, 1), :], sem).start()
        return carry

    lax.fori_loop(0, n_rows, body, 0, unroll=GATHER_UNROLL)


def _wait_row_gather(dst, sem):
    pltpu.make_async_copy(dst, dst, sem).wait()


def _start_row_groups(src_hbm, dst, sem, idx_ref, first, n_groups):
    def body(grp, carry):
        for j in range(GATHER_UNROLL):
            r = grp * GATHER_UNROLL + j
            row = idx_ref[first + r]
            pltpu.make_async_copy(src_hbm.at[pl.ds(row, 1), :], dst.at[pl.ds(r, 1), :], sem).start()
        return carry

    lax.fori_loop(0, n_groups, body, 0)


def _wait_row_groups(dst, sem, n_groups):
    def body(grp, carry):
        blk = dst.at[pl.ds(0, GATHER_UNROLL), :]
        pltpu.make_async_copy(blk, blk, sem).wait()
        return carry

    lax.fori_loop(0, n_groups, body, 0)


def _expert_kernel(e0, te_ref, nxt_ref, par_ref, grp_ref, nused_ref, tok_ref, h_hbm, g_ref, wg_hbm, wu_hbm, wd_hbm,
                   o_ref, xbuf, wgf, wuf, wdf, wgb, wub, wdb, xsem, wsem):
    t = pl.program_id(0)
    nslots, tm = xbuf.shape[0], xbuf.shape[1]
    ahead = nslots - 1
    n_used = nused_ref[0]
    slot = lax.rem(t, nslots)

    def weight_copies(expert, ws):
        return [pltpu.make_async_copy(src.at[e0 + expert], dst.at[ws], wsem.at[ws, j])
                for j, (src, dst) in enumerate(((wg_hbm, wgf), (wu_hbm, wuf), (wd_hbm, wdf)))]

    @pl.when(t == 0)
    def _():
        for cp in weight_copies(te_ref[0], 0):
            cp.start()
        xbuf[...] = jnp.zeros_like(xbuf)
        for j in range(ahead):
            @pl.when(j < n_used)
            def _():
                _start_row_groups(h_hbm, xbuf.at[j], xsem.at[j], tok_ref, j * tm, grp_ref[j])

    @pl.when(t + ahead < n_used)
    def _():
        nslot = lax.rem(t + ahead, nslots)
        _start_row_groups(h_hbm, xbuf.at[nslot], xsem.at[nslot], tok_ref, (t + ahead) * tm, grp_ref[t + ahead])

    @pl.when((t == 0) | (te_ref[t] != te_ref[jnp.maximum(t - 1, 0)]))
    def _():
        ws = par_ref[t]
        for cp in weight_copies(te_ref[t], ws):
            cp.wait()

        @pl.when(nxt_ref[t] >= 0)
        def _():
            for cp in weight_copies(nxt_ref[t], 1 - ws):
                cp.start()

        wgb[...] = wgf[ws].astype(BF16)
        wub[...] = wuf[ws].astype(BF16)
        wdb[...] = wdf[ws].astype(BF16)

    @pl.when(t < n_used)
    def _():
        _wait_row_groups(xbuf.at[slot], xsem.at[slot], grp_ref[t])
        xn = _rms(xbuf[slot], g_ref[...]).astype(BF16)
        hg = _dot(xn, wgb[...])
        hu = _dot(xn, wub[...])
        act = (hg * _sigmoid(hg) * hu).astype(BF16)
        o_ref[...] = _dot(act, wdb[...])

    @pl.when(t >= n_used)
    def _():
        o_ref[...] = jnp.zeros_like(o_ref)


def moe_experts(h, g, w_gate, w_up, w_down, layer, tables, n_tiles):
    n, d = h.shape
    tm = MOE_TILE
    tile_expert, next_expert, parity, tile_groups, n_used, row_tok = tables
    grid_spec = pltpu.PrefetchScalarGridSpec(
        num_scalar_prefetch=6,
        grid=(n_tiles,),
        in_specs=[
            pl.BlockSpec(memory_space=pl.ANY),
            _layer_block((1, d), layer),
            pl.BlockSpec(memory_space=pl.ANY),
            pl.BlockSpec(memory_space=pl.ANY),
            pl.BlockSpec(memory_space=pl.ANY),
        ],
        out_specs=pl.BlockSpec((tm, d), lambda t, *_: (t, 0)),
        scratch_shapes=[pltpu.VMEM((MOE_ROW_SLOTS, tm, d), F32),
                        pltpu.VMEM((2, d, D_EXPERT), F32), pltpu.VMEM((2, d, D_EXPERT), F32),
                        pltpu.VMEM((2, D_EXPERT, d), F32),
                        pltpu.VMEM((d, D_EXPERT), BF16), pltpu.VMEM((d, D_EXPERT), BF16),
                        pltpu.VMEM((D_EXPERT, d), BF16),
                        pltpu.SemaphoreType.DMA((MOE_ROW_SLOTS,)), pltpu.SemaphoreType.DMA((2, 3))],
    )
    return pl.pallas_call(
        functools.partial(_expert_kernel, layer * N_EXPERTS),
        grid_spec=grid_spec,
        out_shape=jax.ShapeDtypeStruct((n_tiles * tm, d), F32),
        compiler_params=_cparams(("arbitrary",)),
        name="moe_experts",
    )(tile_expert, next_expert, parity, tile_groups, n_used, row_tok, h, g, w_gate, w_up, w_down)


def _combine_kernel(final, pos_ref, y_hbm, h_ref, slab_ref, fg_ref, o_ref, buf, sem):
    i = pl.program_id(0)
    n_steps = pl.num_programs(0)
    tm = h_ref.shape[0]
    slot = i % 2

    def start(step, sl):
        for j in range(2):
            _start_row_gather(y_hbm, buf.at[sl, j], sem.at[sl, j], pos_ref, 2 * step * tm + j, 2, tm)

    @pl.when(i == 0)
    def _():
        start(0, 0)

    @pl.when(i + 1 < n_steps)
    def _():
        start(i + 1, 1 - slot)

    for j in range(2):
        _wait_row_gather(buf.at[slot, j], sem.at[slot, j])
    slab = slab_ref[...]
    out = h_ref[...] + slab[:, 2:3] * buf[slot, 0] + slab[:, 3:4] * buf[slot, 1]
    if final:
        out = _rms(out, fg_ref[...])
    o_ref[...] = out


def moe_combine(y_sorted, pos, h, slab, final_gain, final, tm=256):
    n, d = h.shape
    grid_spec = pltpu.PrefetchScalarGridSpec(
        num_scalar_prefetch=1,
        grid=(n // tm,),
        in_specs=[
            pl.BlockSpec(memory_space=pl.ANY),
            pl.BlockSpec((tm, d), lambda i, p: (i, 0)),
            pl.BlockSpec((tm, LANES), lambda i, p: (i, 0)),
            pl.BlockSpec((1, d), lambda i, p: (0, 0)),
        ],
        out_specs=pl.BlockSpec((tm, d), lambda i, p: (i, 0)),
        scratch_shapes=[pltpu.VMEM((2, 2, tm, d), F32), pltpu.SemaphoreType.DMA((2, 2))],
    )
    return pl.pallas_call(
        functools.partial(_combine_kernel, final),
        grid_spec=grid_spec,
        out_shape=jax.ShapeDtypeStruct((n, d), F32),
        compiler_params=_cparams(("arbitrary",)),
        name="moe_combine",
    )(pos, y_sorted, h, slab, final_gain.reshape(1, d))


def _routing_tables(slab, n_tiles):
    n = slab.shape[0]
    tm = MOE_TILE
    e = slab[:, 0:2].astype(jnp.int32).reshape(-1)
    onehot = (e[:, None] == jnp.arange(N_EXPERTS, dtype=jnp.int32)[None, :]).astype(jnp.int32)
    csum = jnp.cumsum(onehot, axis=0)
    counts = csum[-1]
    rank = jnp.take_along_axis(csum, e[:, None], axis=1)[:, 0] - 1
    tiles_per = (counts + tm - 1) // tm
    tile_end = jnp.cumsum(tiles_per)
    tile_start = tile_end - tiles_per
    pos = tile_start[e] * tm + rank
    row_tok = jnp.zeros((n_tiles * tm,), jnp.int32).at[pos].set(jnp.arange(2 * n, dtype=jnp.int32) // 2)
    n_used = tile_end[-1]
    t_idx = jnp.arange(n_tiles, dtype=jnp.int32)
    tile_expert = jnp.searchsorted(tile_end, jnp.minimum(t_idx, n_used - 1), side="right").astype(jnp.int32)
    tile_expert = jnp.minimum(tile_expert, N_EXPERTS - 1)
    after = tile_end[tile_expert]
    next_expert = jnp.where(after < n_used, tile_expert[jnp.minimum(after, n_tiles - 1)], -1).astype(jnp.int32)
    change = jnp.concatenate([jnp.zeros((1,), jnp.int32), (tile_expert[1:] != tile_expert[:-1]).astype(jnp.int32)])
    parity = (jnp.cumsum(change) % 2).astype(jnp.int32)
    rows_left = counts[tile_expert] - (t_idx - tile_start[tile_expert]) * tm
    tile_groups = jnp.where(t_idx < n_used, (jnp.clip(rows_left, 0, tm) + GATHER_UNROLL - 1) // GATHER_UNROLL, 0)
    tables = (tile_expert, next_expert, parity, tile_groups.astype(jnp.int32),
              n_used.reshape(1).astype(jnp.int32), row_tok)
    return tables, pos.astype(jnp.int32)


def _split_in_weights(w):
    c = C_RWKV
    o = 3 * c
    wd0 = w[..., o:o + DECAY_RANK]
    wd1 = w[..., o + DECAY_RANK:o + 2 * DECAY_RANK]
    o += 2 * DECAY_RANK
    ad0 = w[..., o:o + ICLR_RANK]
    ad1 = w[..., o + ICLR_RANK:o + 2 * ICLR_RANK]
    o += 2 * ICLR_RANK
    gd = w[..., o:o + GATE_RANK]
    o += GATE_RANK
    att = w[..., o:]

    def padl(x):
        pad = [(0, 0)] * (x.ndim - 1) + [(0, RANK_PAD - x.shape[-1])]
        return jnp.pad(x, pad)

    rw = jnp.concatenate([w[..., :3 * c], padl(wd0), padl(wd1), padl(ad0), padl(ad1), gd], axis=-1)
    return rw, att


def kernel(x, mem, w_in, shift_prev, shift_next, decay_w0, decay_w2, iclr_a0, iclr_a2, gate_w2, vres_v0, vres_w1, vres_w2, k_k, k_a, r_k, ln_x_gain, ln_x_bias, att_sink, att_out_gain, rel_bias_table, w_out, norm_mix, norm_xatt, mem_norm, xatt_wq, xatt_wk, xatt_wv, xatt_wo, norm_moe, router_coarse_w, router_coarse_b, router_fine_w, router_fine_b, expert_w_gate, expert_w_up, expert_w_down, final_norm):
    batch, seq_len, d = x.shape
    depth = w_in.shape[0]
    n = batch * seq_len
    mem_len = mem.shape[1]
    h = x.reshape(n, d)
    memf = mem.reshape(batch * mem_len, d)
    bias = _t5_bias(rel_bias_table)
    n_tiles = (2 * n) // MOE_TILE + N_EXPERTS

    def row(p):
        return p[:, None, :]

    def pad_axis(p, axis, size):
        pad = [(0, 0)] * p.ndim
        pad[axis] = (0, size - p.shape[axis])
        return jnp.pad(p, pad)

    w_rw, w_at = _split_in_weights(w_in)
    w_in_all = jnp.concatenate([w_rw, w_at], axis=-1).astype(BF16)
    prep_params = (
        row(_split_in_weights(shift_prev)[0]), row(_split_in_weights(shift_next)[0]),
        decay_w0, pad_axis(decay_w2, 2, RANK_PAD).astype(BF16),
        iclr_a0, pad_axis(iclr_a2, 2, RANK_PAD).astype(BF16),
        gate_w2.astype(BF16), row(k_k), row(k_a), row(r_k))
    vres_params = (row(vres_v0), pad_axis(vres_w1, 2, RANK_PAD).astype(BF16),
                   pad_axis(vres_w2, 1, RANK_PAD).astype(BF16))
    ln_g, ln_b = row(ln_x_gain), row(ln_x_bias)
    att_gain = row(att_out_gain)
    w_out_all = w_out.astype(BF16)
    g_mix, g_xatt, g_mem, g_moe = row(norm_mix), row(norm_xatt), row(mem_norm), row(norm_moe)
    wq_all = xatt_wq.astype(BF16)
    wkv_all = jnp.concatenate([xatt_wk, xatt_wv], axis=-1).astype(BF16)
    wo_all = xatt_wo.astype(BF16)
    w_r = pad_axis(jnp.concatenate([router_fine_w, router_coarse_w], axis=-1), 2, LANES)
    w_r2_all = jnp.concatenate(_split_bf16(w_r), axis=-1)
    b_r_all = row(pad_axis(jnp.concatenate([router_fine_b.reshape(depth, -1), router_coarse_b], axis=-1), 1, LANES))
    wg_all = expert_w_gate.reshape(depth * N_EXPERTS, d, D_EXPERT)
    wu_all = expert_w_up.reshape(depth * N_EXPERTS, d, D_EXPERT)
    wd_all = expert_w_down.reshape(depth * N_EXPERTS, D_EXPERT, d)

    v_first = None
    for l in range(depth):
        u, qkv = in_proj(h, g_mix, w_in_all, l, RW_COLS)
        r, v, kk, b0, b1, lw0, lw1, kd0, kd1, g, bg = rwkv_prep(
            u, seq_len, l, prep_params, vres_params if l > 0 else None, v_first)
        if l == 0:
            v_first = v
        yf, yb = rwkv_chunk(r, v, kk, b0, b1, lw0, lw1, kd0, kd1, batch, seq_len)
        y_r = rwkv_post(yf, yb, g, bg, ln_g, ln_b, l)
        y_a = window_attention(qkv, bias, att_sink, att_gain, l, batch, seq_len)
        h = out_proj(y_r, y_a, w_out_all, l, h)
        kv = norm_matmul(memf, g_mem, wkv_all, l, memf.shape[0], 2 * XATT_WIDTH)
        h, slab = cross_attention_router(h, g_xatt, wq_all, kv, wo_all, g_moe, w_r2_all, b_r_all, l, batch, seq_len)
        tables, pos = _routing_tables(slab, n_tiles)
        y_sorted = moe_experts(h, g_moe, wg_all, wu_all, wd_all, l, tables, n_tiles)
        h = moe_combine(y_sorted, pos, h, slab, final_norm, l == depth - 1)
    return h.reshape(batch, seq_len, d)
```

```python
import functools
import math

import jax
import jax.numpy as jnp
from jax import lax
from jax.experimental import pallas as pl
from jax.experimental.pallas import tpu as pltpu

F32 = jnp.float32
BF16 = jnp.bfloat16
HI = lax.Precision.HIGHEST

D_MODEL = 2048
RWKV_HEADS = 16
HEAD_DIM = 64
C_RWKV = RWKV_HEADS * HEAD_DIM
ATT_Q_HEADS = 16
ATT_KV_HEADS = 4
ATT_WIDTH = ATT_Q_HEADS * HEAD_DIM
ATT_KV_WIDTH = ATT_KV_HEADS * HEAD_DIM
WINDOW = 128
BLOCK = 128
DECAY_RANK = 96
ICLR_RANK = 96
VRES_RANK = 64
GATE_RANK = 256
GN_EPS = 64e-5
N_BUCKETS = 32
MAX_DISTANCE = 128
XATT_HEADS = 4
XATT_HEAD_DIM = 128
XATT_WIDTH = XATT_HEADS * XATT_HEAD_DIM
N_GROUPS = 4
EXPERTS_PER_GROUP = 8
N_EXPERTS = N_GROUPS * EXPERTS_PER_GROUP
D_EXPERT = 512
EPS = 1e-6
NEG_INF = -1e30
LOG2E = math.log2(math.e)
HEAD_SHIFT = HEAD_DIM.bit_length() - 1
BLOCK_SHIFT = BLOCK.bit_length() - 1
GROUP_SHIFT = EXPERTS_PER_GROUP.bit_length() - 1

LANES = 128
RANK_PAD = 128
R0, K0, V0 = 0, C_RWKV, 2 * C_RWKV
WD0 = 3 * C_RWKV
WD1 = WD0 + RANK_PAD
AD0 = WD1 + RANK_PAD
AD1 = AD0 + RANK_PAD
GD0 = AD1 + RANK_PAD
RW_COLS = GD0 + GATE_RANK
AT_COLS = ATT_WIDTH + 2 * ATT_KV_WIDTH

CHUNK = 64
MOE_TILE = 256
MOE_ROW_SLOTS = 4
VMEM_LIMIT = 56 * 1024 * 1024


def _cparams(sem, vmem=VMEM_LIMIT):
    return pltpu.CompilerParams(dimension_semantics=sem, vmem_limit_bytes=vmem)


def _sigmoid(x):
    return 1.0 / (1.0 + jnp.exp(-x))


def _dot(a, b, prec=None):
    return jnp.dot(a, b, preferred_element_type=F32, precision=prec)


def _dot_nt(a, b, prec=None):
    return lax.dot_general(a, b, (((1,), (1,)), ((), ())), preferred_element_type=F32, precision=prec)


def _dot_tn(a, b, prec=None):
    return lax.dot_general(a, b, (((0,), (0,)), ((), ())), preferred_element_type=F32, precision=prec)


def _rms(x, g):
    ms = jnp.mean(x * x, axis=-1, keepdims=True)
    return x * lax.rsqrt(ms + EPS) * g


def _split_bf16(x):
    hi = x.astype(BF16)
    lo = (x - hi.astype(F32)).astype(BF16)
    return hi, lo


def _head_block_ones():
    ri = lax.broadcasted_iota(jnp.int32, (LANES, LANES), 0)
    ci = lax.broadcasted_iota(jnp.int32, (LANES, LANES), 1)
    return ((ri >> HEAD_SHIFT) == (ci >> HEAD_SHIFT)).astype(BF16)


def _head_sums(x, bd2):
    rows = x.shape[0]
    hi, lo = _split_bf16(x)
    outs = []
    for gi in range(x.shape[1] // LANES):
        sl = slice(gi * LANES, (gi + 1) * LANES)
        both = _dot(jnp.concatenate([hi[:, sl], lo[:, sl]], axis=0), bd2)
        outs.append(both[0:rows] + both[rows:])
    return outs[0] if len(outs) == 1 else jnp.concatenate(outs, axis=1)


def _layer_block(shape, layer, index=None):
    index = (0,) * len(shape) if index is None else index
    return pl.BlockSpec((None,) + tuple(shape), lambda *_: (layer,) + tuple(index), pipeline_mode=pl.Buffered(1))


def _norm_mm_kernel(x_ref, g_ref, w_ref, o_ref, xn_ref):
    @pl.when(pl.program_id(1) == 0)
    def _():
        xn_ref[...] = _rms(x_ref[...], g_ref[...]).astype(BF16)

    o_ref[...] = _dot(xn_ref[...], w_ref[...]).astype(o_ref.dtype)


def norm_matmul(x, g, w, layer, tm, tn, out_dtype=F32):
    n, d = x.shape
    nc = w.shape[2]
    return pl.pallas_call(
        _norm_mm_kernel,
        grid=(n // tm, nc // tn),
        in_specs=[
            pl.BlockSpec((tm, d), lambda i, j: (i, 0)),
            pl.BlockSpec((None, 1, d), lambda i, j: (layer, 0, 0)),
            pl.BlockSpec((None, d, tn), lambda i, j: (layer, 0, j)),
        ],
        out_specs=pl.BlockSpec((tm, tn), lambda i, j: (i, j)),
        out_shape=jax.ShapeDtypeStruct((n, nc), out_dtype),
        scratch_shapes=[pltpu.VMEM((tm, d), BF16)],
        compiler_params=_cparams(("parallel", "arbitrary")),
        name="norm_matmul",
    )(x, g, w)


def _in_proj_kernel(x_ref, g_ref, w_ref, u_ref, qkv_ref):
    xn = _rms(x_ref[...], g_ref[...]).astype(BF16)
    nu = u_ref.shape[1]
    u_ref[...] = _dot(xn, w_ref[:, 0:nu])
    qkv_ref[...] = _dot(xn, w_ref[:, nu:])


def in_proj(x, g, w, layer, n_rw, tm=512):
    n, d = x.shape
    nc = w.shape[2]
    return pl.pallas_call(
        _in_proj_kernel,
        grid=(n // tm,),
        in_specs=[
            pl.BlockSpec((tm, d), lambda i: (i, 0)),
            _layer_block((1, d), layer),
            _layer_block((d, nc), layer),
        ],
        out_specs=[pl.BlockSpec((tm, n_rw), lambda i: (i, 0)), pl.BlockSpec((tm, nc - n_rw), lambda i: (i, 0))],
        out_shape=[jax.ShapeDtypeStruct((n, n_rw), F32), jax.ShapeDtypeStruct((n, nc - n_rw), F32)],
        compiler_params=_cparams(("parallel",)),
        name="in_proj",
    )(x, g, w)


def _prep_kernel(seq_tiles, has_vres, *refs):
    if has_vres:
        (u_ref, hp_ref, hn_ref, mup_ref, mun_ref, w0_ref, w2_ref, a0_ref, a2_ref, gw_ref, kk_ref, ka_ref, rk_ref,
         vf_ref, v0_ref, v1_ref, v2_ref,
         r_o, v_o, kk_o, b0_o, b1_o, lw0_o, lw1_o, kd0_o, kd1_o, g_o, bg_o) = refs
    else:
        (u_ref, hp_ref, hn_ref, mup_ref, mun_ref, w0_ref, w2_ref, a0_ref, a2_ref, gw_ref, kk_ref, ka_ref, rk_ref,
         r_o, v_o, kk_o, b0_o, b1_o, lw0_o, lw1_o, kd0_o, kd1_o, g_o, bg_o) = refs
    tm = u_ref.shape[0]
    i = pl.program_id(0)
    it = i % seq_tiles
    has_prev = jnp.where(it != 0, 1.0, 0.0)
    has_next = jnp.where(it != seq_tiles - 1, 1.0, 0.0)
    row8 = lax.broadcasted_iota(jnp.int32, (8, 1), 0)

    def shifted(c0, c1):
        u = u_ref[:, c0:c1]
        pr = hp_ref[7:8, c0:c1] * has_prev
        nx = hn_ref[0:1, c0:c1] * has_next
        prev = pltpu.roll(u, 1, 0)
        nxt = pltpu.roll(u, tm - 1, 0)
        prev = jnp.concatenate([jnp.where(row8 == 0, pr, prev[0:8]), prev[8:]], axis=0)
        nxt = jnp.concatenate([nxt[:tm - 8], jnp.where(row8 == 7, nx, nxt[tm - 8:])], axis=0)
        mp = mup_ref[:, c0:c1]
        mn = mun_ref[:, c0:c1]
        return u * (1.0 - mp - mn) + mp * prev + mn * nxt

    bd2 = _head_block_ones()

    r = shifted(R0, R0 + C_RWKV)
    k = shifted(K0, K0 + C_RWKV)
    v = shifted(V0, V0 + C_RWKV)
    r_o[...] = r.astype(r_o.dtype)
    if has_vres:
        low = _dot(v.astype(BF16), v1_ref[...])
        mix = _sigmoid(v0_ref[...] + _dot(low.astype(BF16), v2_ref[...]))
        v = v + (vf_ref[...].astype(F32) - v) * mix
    v_o[...] = v.astype(v_o.dtype)
    kkraw = k * kk_ref[...]
    kk = kkraw / jnp.maximum(jnp.sqrt(_head_sums(kkraw * kkraw, bd2)), 1e-12)
    kk_o[...] = kk.astype(kk_o.dtype)
    ka = ka_ref[...]
    lw_scale = -math.exp(-0.5)
    kd_sum = None
    for z, (wc, ac, lw_o, b_o, kd_o) in enumerate(
            ((WD0, AD0, lw0_o, b0_o, kd0_o), (WD1, AD1, lw1_o, b1_o, kd1_o))):
        wd = shifted(wc, wc + RANK_PAD)
        w_pre = w0_ref[z:z + 1, :] + _dot(jnp.tanh(wd).astype(BF16), w2_ref[z])
        lw_o[...] = lw_scale * _sigmoid(w_pre)
        ad = shifted(ac, ac + RANK_PAD)
        a = _sigmoid(a0_ref[z:z + 1, :] + _dot(ad.astype(BF16), a2_ref[z]))
        b_o[...] = (kk * a).astype(b_o.dtype)
        kd = k * (1.0 + (a - 1.0) * ka)
        kd_o[...] = kd.astype(kd_o.dtype)
        kd_sum = kd if kd_sum is None else kd_sum + kd
    gd = shifted(GD0, GD0 + GATE_RANK)
    g = _dot(_sigmoid(gd).astype(BF16), gw_ref[...])
    g_o[...] = g.astype(g_o.dtype)
    bonus = _head_sums(r * kd_sum * rk_ref[...], bd2) * v
    bg_o[...] = (bonus * g).astype(bg_o.dtype)


def rwkv_prep(u, seq_len, layer, params, vres, v_first, tm=256):
    n = u.shape[0]
    seq_tiles = seq_len // tm
    hb = tm // 8
    nblk8 = n // 8
    has_vres = vres is not None
    c = C_RWKV
    in_specs = [
        pl.BlockSpec((tm, RW_COLS), lambda i: (i, 0)),
        pl.BlockSpec((8, RW_COLS), lambda i: (jnp.maximum(i * hb - 1, 0), 0)),
        pl.BlockSpec((8, RW_COLS), lambda i: (jnp.minimum((i + 1) * hb, nblk8 - 1), 0)),
    ] + [_layer_block(p.shape[1:], layer) for p in params]
    args = [u, u, u] + list(params)
    if has_vres:
        in_specs += [pl.BlockSpec((tm, c), lambda i: (i, 0))] + [_layer_block(p.shape[1:], layer - 1) for p in vres]
        args += [v_first] + list(vres)
    out_spec = pl.BlockSpec((tm, c), lambda i: (i, 0))
    dtypes = [BF16] * 5 + [F32] * 2 + [BF16] * 4
    return pl.pallas_call(
        functools.partial(_prep_kernel, seq_tiles, has_vres),
        grid=(n // tm,),
        in_specs=in_specs,
        out_specs=[out_spec] * len(dtypes),
        out_shape=[jax.ShapeDtypeStruct((n, c), dt) for dt in dtypes],
        compiler_params=_cparams(("parallel",)),
        name="rwkv_prep",
    )(*args)


def _chunk_kernel(rf, vf, kkf, af, lwf, kdf, rb, vb, kkb, ab, lwb, kdb, yf_o, yb_o, s_ref):
    nbatch, L, width = rf.shape
    L2 = 2 * L
    npairs = width // LANES

    @pl.when(pl.program_id(0) == 0)
    def _():
        s_ref[...] = jnp.zeros_like(s_ref)

    lane = lax.broadcasted_iota(jnp.int32, (1, LANES), 1)
    m1 = lane < HEAD_DIM
    t_i = lax.broadcasted_iota(jnp.int32, (L, L2), 0)
    s_i = lax.broadcasted_iota(jnp.int32, (L, L2), 1) & (L - 1)
    ri = lax.broadcasted_iota(jnp.int32, (LANES, LANES), 0)
    ci = lax.broadcasted_iota(jnp.int32, (LANES, LANES), 1)
    same_head = (ri >> HEAD_SHIFT) == (ci >> HEAD_SHIFT)

    def stack(x):
        z = jnp.zeros_like(x)
        return jnp.concatenate([jnp.where(m1, x, z), jnp.where(m1, z, x)], axis=0)

    chains = []
    for rev, ins in ((False, (rf, vf, kkf, af, lwf, kdf)), (True, (rb, vb, kkb, ab, lwb, kdb))):
        strict = (s_i > t_i) if rev else (s_i < t_i)
        incl = (s_i >= t_i) if rev else (s_i <= t_i)
        incl4 = jnp.concatenate([incl, incl], axis=1)
        tri2 = incl.astype(BF16)
        for bi in range(nbatch):
            r, vb16, kk, bn, lw, kd = (x[bi] for x in ins)
            r, kk, bn, kd = (x.astype(F32) for x in (r, kk, bn, kd))
            c = _dot(tri2, jnp.concatenate(_split_bf16(lw), axis=0))
            c_last = c[0:1, :] if rev else c[L - 1:L, :]
            decay_last = jnp.exp(c_last)
            e_pos = jnp.exp(c)
            e_neg = jnp.exp(-c)
            e_rem = jnp.exp(c_last - c)
            a_t = (-kk * jnp.exp(c - lw)).astype(BF16)
            b_t = (bn * e_neg).astype(BF16)
            k_t = (kd * e_neg).astype(BF16)
            r_t = (r * e_pos).astype(BF16)
            b_h = (bn * e_rem).astype(BF16)
            k_h = (kd * e_rem).astype(BF16)
            for pr in range(npairs):
                sl = slice(pr * LANES, (pr + 1) * LANES)
                chains.append(dict(
                    strict=strict, incl4=incl4, s_ref=s_ref.at[1 if rev else 0, bi, pr],
                    decay_last=decay_last[:, sl],
                    ar=jnp.concatenate([a_t[:, sl], r_t[:, sl]], axis=0),
                    bk=jnp.concatenate([stack(b_t[:, sl]), stack(k_t[:, sl])], axis=0),
                    bkh=jnp.concatenate([b_h[:, sl], k_h[:, sl]], axis=0),
                    v=vb16[:, sl], v_s=stack(vb16[:, sl])))
    for ch in chains:
        ch["s0"] = ch["s_ref"][...]
        ch["sc"] = _dot_nt(ch["ar"], ch["bk"])
    for ch in chains:
        ch["from_state"] = _dot_nt(ch["ar"], ch["s0"].astype(BF16))
    for ch in chains:
        sc = ch["sc"]
        m_ak = jnp.where(ch["strict"], sc[0:L, L2:], 0.0).astype(BF16)
        ch["m_r"] = jnp.where(ch["incl4"], sc[L:, :], 0.0).astype(BF16)
        ch["p"] = jnp.where(ch["strict"], sc[0:L, 0:L2], 0.0)
        ch["x"] = ch["from_state"][0:L] + _dot(m_ak, ch["v_s"])
    n_steps = int(math.log2(L))
    for step in range(n_steps):
        for ch in chains:
            pb = ch["p"].astype(BF16)
            xs = stack(ch["x"].astype(BF16))
            if step < n_steps - 1:
                res = _dot(pb, jnp.concatenate([xs, stack(pb)], axis=1))
                ch["x"] = ch["x"] + res[:, 0:LANES]
                ch["p"] = res[:, LANES:]
            else:
                ch["x"] = ch["x"] + _dot(pb, xs)
    for ch in chains:
        ub = ch["x"].astype(BF16)
        ch["ub"] = ub
        ch["y"] = ch["from_state"][L:] + _dot(ch["m_r"], jnp.concatenate([stack(ub), ch["v_s"]], axis=0))
    for ch in chains:
        upd = _dot_tn(jnp.concatenate([ch["ub"], ch["v"]], axis=0), ch["bkh"])
        ch["s_ref"][...] = ch["s0"] * ch["decay_last"] + jnp.where(same_head, upd, 0.0)
    for di, y_o in enumerate((yf_o, yb_o)):
        for bi in range(nbatch):
            group = chains[(di * nbatch + bi) * npairs:(di * nbatch + bi + 1) * npairs]
            y_o[bi] = jnp.concatenate([ch["y"] for ch in group], axis=1).astype(y_o.dtype)


def rwkv_chunk(r, v, kk, b0, b1, lw0, lw1, kd0, kd1, batch, seq_len):
    c = C_RWKV
    L = CHUNK
    nc = seq_len // L

    def r3(x):
        return x.reshape(batch, seq_len, c)

    fwd = pl.BlockSpec((batch, L, c), lambda t: (0, t, 0))
    bwd = pl.BlockSpec((batch, L, c), lambda t: (0, nc - 1 - t, 0))
    sds = jax.ShapeDtypeStruct((batch, seq_len, c), BF16)
    yf, yb = pl.pallas_call(
        _chunk_kernel,
        grid=(nc,),
        in_specs=[fwd] * 6 + [bwd] * 6,
        out_specs=[fwd, bwd],
        out_shape=[sds, sds],
        scratch_shapes=[pltpu.VMEM((2, batch, c // LANES, LANES, LANES), F32)],
        compiler_params=_cparams(("arbitrary",)),
        name="rwkv_chunk",
    )(r3(r), r3(v), r3(kk), r3(b0), r3(lw0), r3(kd0), r3(r), r3(v), r3(kk), r3(b1), r3(lw1), r3(kd1))
    return yf.reshape(-1, c), yb.reshape(-1, c)


def _post_kernel(yf, yb, g, bg, lng, lnb, o_ref):
    bd2 = _head_block_ones()
    y = yf[...].astype(F32) + yb[...].astype(F32)
    mu = _head_sums(y, bd2) * (1.0 / HEAD_DIM)
    d = y - mu
    var = _head_sums(d * d, bd2) * (1.0 / HEAD_DIM)
    yn = d * lax.rsqrt(var + GN_EPS) * lng[...] + lnb[...]
    o_ref[...] = (yn * g[...].astype(F32) + bg[...].astype(F32)).astype(o_ref.dtype)


def rwkv_post(yf, yb, g, bg, ln_g, ln_b, layer, tm=512):
    n, c = yf.shape
    blk = pl.BlockSpec((tm, c), lambda i: (i, 0))
    vec = _layer_block((1, c), layer)
    return pl.pallas_call(
        _post_kernel,
        grid=(n // tm,),
        in_specs=[blk] * 4 + [vec] * 2,
        out_specs=blk,
        out_shape=jax.ShapeDtypeStruct((n, c), BF16),
        compiler_params=_cparams(("parallel",)),
        name="rwkv_post",
    )(yf, yb, g, bg, ln_g, ln_b)


def _attn_kernel(layer, q_ref, kp, kc, kn, vp, vc, vn, bias_ref, sink_ref, gain_ref, o_ref, bias_s):
    n = pl.program_id(1)
    nb = pl.num_programs(1)
    j = lax.broadcasted_iota(jnp.int32, (1, 1, 3 * BLOCK), 2)
    off_seq = ((j < BLOCK) & (n == 0)) | ((j >= 2 * BLOCK) & (n == nb - 1))

    @pl.when((n == 0) | (n == 1) | (n == nb - 1))
    def _():
        bias_s[...] = bias_ref[...] + jnp.where(off_seq, NEG_INF, 0.0)

    kb = jnp.concatenate([kp[...], kc[...], kn[...]], axis=0).astype(BF16)
    vb = jnp.concatenate([vp[...], vc[...], vn[...]], axis=0).astype(BF16)
    q = (q_ref[...] * (HEAD_DIM ** -0.5 * LOG2E)).astype(BF16)
    group = ATT_Q_HEADS // ATT_KV_HEADS
    row_head = lax.broadcasted_iota(jnp.int32, (group * BLOCK, 1), 0) >> BLOCK_SHIFT

    def scores(kh):
        qg = jnp.concatenate([q[:, (kh * group + gi) * HEAD_DIM:(kh * group + gi + 1) * HEAD_DIM]
                              for gi in range(group)], axis=0)
        return _dot_nt(qg, kb[:, kh * HEAD_DIM:(kh + 1) * HEAD_DIM])

    def softmax(kh, s):
        s = s + bias_s[kh * group:(kh + 1) * group].reshape(group * BLOCK, 3 * BLOCK)
        sk = jnp.zeros((group * BLOCK, 1), F32)
        for gi in range(group):
            sk = jnp.where(row_head == gi, sink_ref[layer, kh * group + gi] * LOG2E, sk)
        m = jnp.maximum(jnp.max(s, axis=-1, keepdims=True), sk)
        p = jnp.exp2(s - m)
        denom = jnp.sum(p, axis=-1, keepdims=True) + jnp.exp2(sk - m)
        return p.astype(BF16), denom

    all_scores = [scores(kh) for kh in range(ATT_KV_HEADS)]
    probs = [softmax(kh, s) for kh, s in enumerate(all_scores)]
    outs = []
    for kh, (p, denom) in enumerate(probs):
        og = _dot(p, vb[:, kh * HEAD_DIM:(kh + 1) * HEAD_DIM]) / denom
        outs += [og[gi * BLOCK:(gi + 1) * BLOCK] for gi in range(group)]
    o = jnp.concatenate(outs, axis=-1)
    o_ref[...] = _rms(o, gain_ref[...]).astype(o_ref.dtype)


def window_attention(qkv, bias, sink, gain, layer, batch, seq_len):
    n = qkv.shape[0]
    nb = seq_len // BLOCK
    kcol = ATT_WIDTH // ATT_KV_WIDTH
    vcol = kcol + 1

    def rows(off):
        def f(b, t):
            return jnp.clip(t + off, 0, nb - 1) + b * nb
        return f

    def kv_spec(col, off):
        f = rows(off)
        return pl.BlockSpec((BLOCK, ATT_KV_WIDTH), lambda b, t: (f(b, t), col))

    return pl.pallas_call(
        functools.partial(_attn_kernel, layer),
        grid=(batch, nb),
        in_specs=[
            pl.BlockSpec((BLOCK, ATT_WIDTH), lambda b, t: (b * nb + t, 0)),
            kv_spec(kcol, -1), kv_spec(kcol, 0), kv_spec(kcol, 1),
            kv_spec(vcol, -1), kv_spec(vcol, 0), kv_spec(vcol, 1),
            pl.BlockSpec((ATT_Q_HEADS, BLOCK, 3 * BLOCK), lambda b, t: (0, 0, 0)),
            pl.BlockSpec(memory_space=pltpu.SMEM),
            _layer_block((1, ATT_WIDTH), layer),
        ],
        out_specs=pl.BlockSpec((BLOCK, ATT_WIDTH), lambda b, t: (b * nb + t, 0)),
        out_shape=jax.ShapeDtypeStruct((n, ATT_WIDTH), BF16),
        scratch_shapes=[pltpu.VMEM((ATT_Q_HEADS, BLOCK, 3 * BLOCK), F32)],
        compiler_params=_cparams(("arbitrary", "arbitrary")),
        name="window_attention",
    )(qkv, qkv, qkv, qkv, qkv, qkv, qkv, bias, sink, gain)


def _t5_bias(rel_bias_table):
    i = jnp.arange(BLOCK)[:, None]
    j = jnp.arange(3 * BLOCK)[None, :]
    rel = j - BLOCK - i
    nbk = N_BUCKETS // 2
    max_exact = nbk // 2
    ret = jnp.where(rel > 0, nbk, 0)
    nabs = jnp.abs(rel)
    large = max_exact + (jnp.log(jnp.maximum(nabs, max_exact).astype(jnp.float32) / max_exact)
                         / math.log(MAX_DISTANCE / max_exact) * (nbk - max_exact)).astype(jnp.int32)
    large = jnp.minimum(large, nbk - 1)
    bucket = ret + jnp.where(nabs < max_exact, nabs, large)
    onehot = (bucket[None, :, :] == jnp.arange(N_BUCKETS)[:, None, None]).astype(F32)
    bias = jnp.einsum("bh,bij->hij", rel_bias_table.astype(F32), onehot, precision=HI)
    return jnp.where((nabs <= WINDOW)[None], bias * LOG2E, NEG_INF)


def _outproj_kernel(yr, ya, w1, w2, h_ref, o_ref):
    o_ref[...] = h_ref[...] + _dot(yr[...], w1[...]) + _dot(ya[...], w2[...])


def out_proj(y_r, y_a, w, layer, h, tm=512):
    n, d = h.shape
    c = y_r.shape[1]
    return pl.pallas_call(
        _outproj_kernel,
        grid=(n // tm,),
        in_specs=[
            pl.BlockSpec((tm, c), lambda i: (i, 0)),
            pl.BlockSpec((tm, c), lambda i: (i, 0)),
            _layer_block((c, d), layer, (0, 0)),
            _layer_block((c, d), layer, (1, 0)),
            pl.BlockSpec((tm, d), lambda i: (i, 0)),
        ],
        out_specs=pl.BlockSpec((tm, d), lambda i: (i, 0)),
        out_shape=jax.ShapeDtypeStruct((n, d), F32),
        compiler_params=_cparams(("parallel",)),
        name="out_proj",
    )(y_r, y_a, w, w, h)


def _xatt_kernel(h_ref, g_ref, wq, kv_ref, wo, gm_ref, wr_ref, br_ref, o_ref, slab_ref):
    h = h_ref[...]
    hn = _rms(h, g_ref[...]).astype(BF16)
    q = (_dot(hn, wq[...]) * (XATT_HEAD_DIM ** -0.5 * LOG2E)).astype(BF16)
    kv = kv_ref[...]
    outs = []
    for hd in range(XATT_HEADS):
        sl = slice(hd * XATT_HEAD_DIM, (hd + 1) * XATT_HEAD_DIM)
        k_h = kv[:, sl].astype(BF16)
        v_h = kv[:, XATT_WIDTH + hd * XATT_HEAD_DIM: XATT_WIDTH + (hd + 1) * XATT_HEAD_DIM].astype(BF16)
        s = _dot_nt(q[:, sl], k_h)
        m = jnp.max(s, axis=-1, keepdims=True)
        p = jnp.exp2(s - m)
        p = p / jnp.sum(p, axis=-1, keepdims=True)
        outs.append(_dot(p.astype(BF16), v_h))
    o = jnp.concatenate(outs, axis=-1).astype(BF16)
    h_new = h + _dot(o, wo[...])
    o_ref[...] = h_new
    slab_ref[...] = _route(h_new, gm_ref[...], wr_ref[...], br_ref[...])


def cross_attention_router(h, g, wq, kv, wo, g_moe, w_r, b_r, layer, batch, seq_len, tm=512):
    n, d = h.shape
    m = kv.shape[0] // batch
    tiles = seq_len // tm
    return pl.pallas_call(
        _xatt_kernel,
        grid=(n // tm,),
        in_specs=[
            pl.BlockSpec((tm, d), lambda i: (i, 0)),
            _layer_block((1, d), layer),
            _layer_block((d, XATT_WIDTH), layer),
            pl.BlockSpec((m, 2 * XATT_WIDTH), lambda i: (i // tiles, 0)),
            _layer_block((XATT_WIDTH, d), layer),
            _layer_block((1, d), layer),
            _layer_block((d, 2 * LANES), layer),
            _layer_block((1, LANES), layer),
        ],
        out_specs=[pl.BlockSpec((tm, d), lambda i: (i, 0)), pl.BlockSpec((tm, LANES), lambda i: (i, 0))],
        out_shape=[jax.ShapeDtypeStruct((n, d), F32), jax.ShapeDtypeStruct((n, LANES), F32)],
        compiler_params=_cparams(("parallel",)),
        name="cross_attention_router",
    )(h, g, wq, kv, wo, g_moe, w_r, b_r)


def _route(h, g, w2, b):
    hn_hi, hn_lo = _split_bf16(_rms(h, g))
    both = _dot(hn_hi, w2) + _dot(hn_lo, w2)
    logits = both[:, 0:LANES] + both[:, LANES:] + b
    tm = logits.shape[0]
    lane = lax.broadcasted_iota(jnp.int32, (tm, LANES), 1)
    is_coarse = (lane >= N_EXPERTS) & (lane < N_EXPERTS + N_GROUPS)
    cl = jnp.where(is_coarse, logits, NEG_INF)
    cmax = jnp.max(cl, axis=-1, keepdims=True)
    csum = jnp.sum(jnp.where(is_coarse, jnp.exp(cl - cmax), 0.0), axis=-1, keepdims=True)
    p_g = 1.0 / csum
    lane_f = lane.astype(F32)
    grp_f = (lane >> GROUP_SHIFT).astype(F32)
    big = float(LANES)
    g_lane = jnp.min(jnp.where(is_coarse & (cl == cmax), lane_f, big), axis=-1, keepdims=True)
    g_idx = g_lane - float(N_EXPERTS)
    in_grp = (lane < N_EXPERTS) & (grp_f == g_idx)
    fl = jnp.where(in_grp, logits, NEG_INF)
    m1 = jnp.max(fl, axis=-1, keepdims=True)
    i1 = jnp.min(jnp.where(in_grp & (fl == m1), lane_f, big), axis=-1, keepdims=True)
    fl2 = jnp.where(lane_f == i1, NEG_INF, fl)
    m2 = jnp.max(fl2, axis=-1, keepdims=True)
    i2 = jnp.min(jnp.where(in_grp & (lane_f != i1) & (fl2 == m2), lane_f, big), axis=-1, keepdims=True)
    e2 = jnp.exp(m2 - m1)
    w1 = p_g / (1.0 + e2)
    w2 = p_g * e2 / (1.0 + e2)
    return jnp.where(lane == 0, i1,
                     jnp.where(lane == 1, i2,
                               jnp.where(lane == 2, w1, jnp.where(lane == 3, w2, 0.0))))


GATHER_UNROLL = 8


def _start_row_gather(src_hbm, dst, sem, idx_ref, first, stride, n_rows):
    def body(r, carry):
        row = idx_ref[first + r * stride]
        pltpu.make_async_copy(src_hbm.at[pl.ds(row, 1), :], dst.at[pl.ds(r, 1), :], sem).start()
        return carry

    lax.fori_loop(0, n_rows, body, 0, unroll=GATHER_UNROLL)


def _wait_row_gather(dst, sem):
    pltpu.make_async_copy(dst, dst, sem).wait()


def _start_row_groups(src_hbm, dst, sem, idx_ref, first, n_groups):
    def body(grp, carry):
        for j in range(GATHER_UNROLL):
            r = grp * GATHER_UNROLL + j
            row = idx_ref[first + r]
            pltpu.make_async_copy(src_hbm.at[pl.ds(row, 1), :], dst.at[pl.ds(r, 1), :], sem).start()
        return carry

    lax.fori_loop(0, n_groups, body, 0)


def _wait_row_groups(dst, sem, n_groups):
    def body(grp, carry):
        blk = dst.at[pl.ds(0, GATHER_UNROLL), :]
        pltpu.make_async_copy(blk, blk, sem).wait()
        return carry

    lax.fori_loop(0, n_groups, body, 0)


def _expert_kernel(e0, te_ref, nxt_ref, par_ref, grp_ref, nused_ref, tok_ref, h_hbm, g_ref, wg_hbm, wu_hbm, wd_hbm,
                   o_ref, xbuf, wgf, wuf, wdf, wgb, wub, wdb, xsem, wsem):
    t = pl.program_id(0)
    nslots, tm = xbuf.shape[0], xbuf.shape[1]
    ahead = nslots - 1
    n_used = nused_ref[0]
    slot = lax.rem(t, nslots)

    def weight_copies(expert, ws):
        return [pltpu.make_async_copy(src.at[e0 + expert], dst.at[ws], wsem.at[ws, j])
                for j, (src, dst) in enumerate(((wg_hbm, wgf), (wu_hbm, wuf), (wd_hbm, wdf)))]

    @pl.when(t == 0)
    def _():
        for cp in weight_copies(te_ref[0], 0):
            cp.start()
        xbuf[...] = jnp.zeros_like(xbuf)
        for j in range(ahead):
            @pl.when(j < n_used)
            def _():
                _start_row_groups(h_hbm, xbuf.at[j], xsem.at[j], tok_ref, j * tm, grp_ref[j])

    @pl.when(t + ahead < n_used)
    def _():
        nslot = lax.rem(t + ahead, nslots)
        _start_row_groups(h_hbm, xbuf.at[nslot], xsem.at[nslot], tok_ref, (t + ahead) * tm, grp_ref[t + ahead])

    @pl.when((t == 0) | (te_ref[t] != te_ref[jnp.maximum(t - 1, 0)]))
    def _():
        ws = par_ref[t]
        for cp in weight_copies(te_ref[t], ws):
            cp.wait()

        @pl.when(nxt_ref[t] >= 0)
        def _():
            for cp in weight_copies(nxt_ref[t], 1 - ws):
                cp.start()

        wgb[...] = wgf[ws].astype(BF16)
        wub[...] = wuf[ws].astype(BF16)
        wdb[...] = wdf[ws].astype(BF16)

    @pl.when(t < n_used)
    def _():
        _wait_row_groups(xbuf.at[slot], xsem.at[slot], grp_ref[t])
        xn = _rms(xbuf[slot], g_ref[...]).astype(BF16)
        hg = _dot(xn, wgb[...])
        hu = _dot(xn, wub[...])
        act = (hg * _sigmoid(hg) * hu).astype(BF16)
        o_ref[...] = _dot(act, wdb[...])

    @pl.when(t >= n_used)
    def _():
        o_ref[...] = jnp.zeros_like(o_ref)


def moe_experts(h, g, w_gate, w_up, w_down, layer, tables, n_tiles):
    n, d = h.shape
    tm = MOE_TILE
    tile_expert, next_expert, parity, tile_groups, n_used, row_tok = tables
    grid_spec = pltpu.PrefetchScalarGridSpec(
        num_scalar_prefetch=6,
        grid=(n_tiles,),
        in_specs=[
            pl.BlockSpec(memory_space=pl.ANY),
            _layer_block((1, d), layer),
            pl.BlockSpec(memory_space=pl.ANY),
            pl.BlockSpec(memory_space=pl.ANY),
            pl.BlockSpec(memory_space=pl.ANY),
        ],
        out_specs=pl.BlockSpec((tm, d), lambda t, *_: (t, 0)),
        scratch_shapes=[pltpu.VMEM((MOE_ROW_SLOTS, tm, d), F32),
                        pltpu.VMEM((2, d, D_EXPERT), F32), pltpu.VMEM((2, d, D_EXPERT), F32),
                        pltpu.VMEM((2, D_EXPERT, d), F32),
                        pltpu.VMEM((d, D_EXPERT), BF16), pltpu.VMEM((d, D_EXPERT), BF16),
                        pltpu.VMEM((D_EXPERT, d), BF16),
                        pltpu.SemaphoreType.DMA((MOE_ROW_SLOTS,)), pltpu.SemaphoreType.DMA((2, 3))],
    )
    return pl.pallas_call(
        functools.partial(_expert_kernel, layer * N_EXPERTS),
        grid_spec=grid_spec,
        out_shape=jax.ShapeDtypeStruct((n_tiles * tm, d), F32),
        compiler_params=_cparams(("arbitrary",)),
        name="moe_experts",
    )(tile_expert, next_expert, parity, tile_groups, n_used, row_tok, h, g, w_gate, w_up, w_down)


def _combine_kernel(final, pos_ref, y_hbm, h_ref, slab_ref, fg_ref, o_ref, buf, sem):
    i = pl.program_id(0)
    n_steps = pl.num_programs(0)
    tm = h_ref.shape[0]
    slot = i % 2

    def start(step, sl):
        for j in range(2):
            _start_row_gather(y_hbm, buf.at[sl, j], sem.at[sl, j], pos_ref, 2 * step * tm + j, 2, tm)

    @pl.when(i == 0)
    def _():
        start(0, 0)

    @pl.when(i + 1 < n_steps)
    def _():
        start(i + 1, 1 - slot)

    for j in range(2):
        _wait_row_gather(buf.at[slot, j], sem.at[slot, j])
    slab = slab_ref[...]
    out = h_ref[...] + slab[:, 2:3] * buf[slot, 0] + slab[:, 3:4] * buf[slot, 1]
    if final:
        out = _rms(out, fg_ref[...])
    o_ref[...] = out


def moe_combine(y_sorted, pos, h, slab, final_gain, final, tm=256):
    n, d = h.shape
    grid_spec = pltpu.PrefetchScalarGridSpec(
        num_scalar_prefetch=1,
        grid=(n // tm,),
        in_specs=[
            pl.BlockSpec(memory_space=pl.ANY),
            pl.BlockSpec((tm, d), lambda i, p: (i, 0)),
            pl.BlockSpec((tm, LANES), lambda i, p: (i, 0)),
            pl.BlockSpec((1, d), lambda i, p: (0, 0)),
        ],
        out_specs=pl.BlockSpec((tm, d), lambda i, p: (i, 0)),
        scratch_shapes=[pltpu.VMEM((2, 2, tm, d), F32), pltpu.SemaphoreType.DMA((2, 2))],
    )
    return pl.pallas_call(
        functools.partial(_combine_kernel, final),
        grid_spec=grid_spec,
        out_shape=jax.ShapeDtypeStruct((n, d), F32),
        compiler_params=_cparams(("arbitrary",)),
        name="moe_combine",
    )(pos, y_sorted, h, slab, final_gain.reshape(1, d))


def _routing_tables(slab, n_tiles):
    n = slab.shape[0]
    tm = MOE_TILE
    e = slab[:, 0:2].astype(jnp.int32).reshape(-1)
    onehot = (e[:, None] == jnp.arange(N_EXPERTS, dtype=jnp.int32)[None, :]).astype(jnp.int32)
    csum = jnp.cumsum(onehot, axis=0)
    counts = csum[-1]
    rank = jnp.take_along_axis(csum, e[:, None], axis=1)[:, 0] - 1
    tiles_per = (counts + tm - 1) // tm
    tile_end = jnp.cumsum(tiles_per)
    tile_start = tile_end - tiles_per
    pos = tile_start[e] * tm + rank
    row_tok = jnp.zeros((n_tiles * tm,), jnp.int32).at[pos].set(jnp.arange(2 * n, dtype=jnp.int32) // 2)
    n_used = tile_end[-1]
    t_idx = jnp.arange(n_tiles, dtype=jnp.int32)
    tile_expert = jnp.searchsorted(tile_end, jnp.minimum(t_idx, n_used - 1), side="right").astype(jnp.int32)
    tile_expert = jnp.minimum(tile_expert, N_EXPERTS - 1)
    after = tile_end[tile_expert]
    next_expert = jnp.where(after < n_used, tile_expert[jnp.minimum(after, n_tiles - 1)], -1).astype(jnp.int32)
    change = jnp.concatenate([jnp.zeros((1,), jnp.int32), (tile_expert[1:] != tile_expert[:-1]).astype(jnp.int32)])
    parity = (jnp.cumsum(change) % 2).astype(jnp.int32)
    rows_left = counts[tile_expert] - (t_idx - tile_start[tile_expert]) * tm
    tile_groups = jnp.where(t_idx < n_used, (jnp.clip(rows_left, 0, tm) + GATHER_UNROLL - 1) // GATHER_UNROLL, 0)
    tables = (tile_expert, next_expert, parity, tile_groups.astype(jnp.int32),
              n_used.reshape(1).astype(jnp.int32), row_tok)
    return tables, pos.astype(jnp.int32)


def _split_in_weights(w):
    c = C_RWKV
    o = 3 * c
    wd0 = w[..., o:o + DECAY_RANK]
    wd1 = w[..., o + DECAY_RANK:o + 2 * DECAY_RANK]
    o += 2 * DECAY_RANK
    ad0 = w[..., o:o + ICLR_RANK]
    ad1 = w[..., o + ICLR_RANK:o + 2 * ICLR_RANK]
    o += 2 * ICLR_RANK
    gd = w[..., o:o + GATE_RANK]
    o += GATE_RANK
    att = w[..., o:]

    def padl(x):
        pad = [(0, 0)] * (x.ndim - 1) + [(0, RANK_PAD - x.shape[-1])]
        return jnp.pad(x, pad)

    rw = jnp.concatenate([w[..., :3 * c], padl(wd0), padl(wd1), padl(ad0), padl(ad1), gd], axis=-1)
    return rw, att


def kernel(x, mem, w_in, shift_prev, shift_next, decay_w0, decay_w2, iclr_a0, iclr_a2, gate_w2, vres_v0, vres_w1, vres_w2, k_k, k_a, r_k, ln_x_gain, ln_x_bias, att_sink, att_out_gain, rel_bias_table, w_out, norm_mix, norm_xatt, mem_norm, xatt_wq, xatt_wk, xatt_wv, xatt_wo, norm_moe, router_coarse_w, router_coarse_b, router_fine_w, router_fine_b, expert_w_gate, expert_w_up, expert_w_down, final_norm):
    batch, seq_len, d = x.shape
    depth = w_in.shape[0]
    n = batch * seq_len
    mem_len = mem.shape[1]
    h = x.reshape(n, d)
    memf = mem.reshape(batch * mem_len, d)
    bias = _t5_bias(rel_bias_table)
    n_tiles = (2 * n) // MOE_TILE + N_EXPERTS

    def row(p):
        return p[:, None, :]

    def pad_axis(p, axis, size):
        pad = [(0, 0)] * p.ndim
        pad[axis] = (0, size - p.shape[axis])
        return jnp.pad(p, pad)

    w_rw, w_at = _split_in_weights(w_in.astype(BF16))
    w_in_all = jnp.concatenate([w_rw, w_at], axis=-1)
    prep_params = (
        row(_split_in_weights(shift_prev)[0]), row(_split_in_weights(shift_next)[0]),
        decay_w0, pad_axis(decay_w2, 2, RANK_PAD).astype(BF16),
        iclr_a0, pad_axis(iclr_a2, 2, RANK_PAD).astype(BF16),
        gate_w2.astype(BF16), row(k_k), row(k_a), row(r_k))
    vres_params = (row(vres_v0), pad_axis(vres_w1, 2, RANK_PAD).astype(BF16),
                   pad_axis(vres_w2, 1, RANK_PAD).astype(BF16))
    ln_g, ln_b = row(ln_x_gain), row(ln_x_bias)
    att_gain = row(att_out_gain)
    w_out_all = w_out.astype(BF16)
    g_mix, g_xatt, g_mem, g_moe = row(norm_mix), row(norm_xatt), row(mem_norm), row(norm_moe)
    wq_all = xatt_wq.astype(BF16)
    wkv_all = jnp.concatenate([xatt_wk, xatt_wv], axis=-1).astype(BF16)
    wo_all = xatt_wo.astype(BF16)
    w_r = pad_axis(jnp.concatenate([router_fine_w, router_coarse_w], axis=-1), 2, LANES)
    w_r2_all = jnp.concatenate(_split_bf16(w_r), axis=-1)
    b_r_all = row(pad_axis(jnp.concatenate([router_fine_b.reshape(depth, -1), router_coarse_b], axis=-1), 1, LANES))
    wg_all = expert_w_gate.reshape(depth * N_EXPERTS, d, D_EXPERT)
    wu_all = expert_w_up.reshape(depth * N_EXPERTS, d, D_EXPERT)
    wd_all = expert_w_down.reshape(depth * N_EXPERTS, D_EXPERT, d)

    v_first = None
    for l in range(depth):
        u, qkv = in_proj(h, g_mix, w_in_all, l, RW_COLS)
        r, v, kk, b0, b1, lw0, lw1, kd0, kd1, g, bg = rwkv_prep(
            u, seq_len, l, prep_params, vres_params if l > 0 else None, v_first)
        if l == 0:
            v_first = v
        yf, yb = rwkv_chunk(r, v, kk, b0, b1, lw0, lw1, kd0, kd1, batch, seq_len)
        y_r = rwkv_post(yf, yb, g, bg, ln_g, ln_b, l)
        y_a = window_attention(qkv, bias, att_sink, att_gain, l, batch, seq_len)
        h = out_proj(y_r, y_a, w_out_all, l, h)
        kv = norm_matmul(memf, g_mem, wkv_all, l, memf.shape[0], 2 * XATT_WIDTH)
        h, slab = cross_attention_router(h, g_xatt, wq_all, kv, wo_all, g_moe, w_r2_all, b_r_all, l, batch, seq_len)
        tables, pos = _routing_tables(slab, n_tiles)
        y_sorted = moe_experts(h, g_moe, wg_all, wu_all, wd_all, l, tables, n_tiles)
        h = moe_combine(y_sorted, pos, h, slab, final_norm, l == depth - 1)
    return h.reshape(batch, seq_len, d)
```

```python
import functools
import math

import jax
import jax.numpy as jnp
from jax import lax
from jax.experimental import pallas as pl
from jax.experimental.pallas import tpu as pltpu

F32 = jnp.float32
BF16 = jnp.bfloat16
HI = lax.Precision.HIGHEST

D_MODEL = 2048
RWKV_HEADS = 16
HEAD_DIM = 64
C_RWKV = RWKV_HEADS * HEAD_DIM
ATT_Q_HEADS = 16
ATT_KV_HEADS = 4
ATT_WIDTH = ATT_Q_HEADS * HEAD_DIM
ATT_KV_WIDTH = ATT_KV_HEADS * HEAD_DIM
WINDOW = 128
BLOCK = 128
DECAY_RANK = 96
ICLR_RANK = 96
VRES_RANK = 64
GATE_RANK = 256
GN_EPS = 64e-5
N_BUCKETS = 32
MAX_DISTANCE = 128
XATT_HEADS = 4
XATT_HEAD_DIM = 128
XATT_WIDTH = XATT_HEADS * XATT_HEAD_DIM
N_GROUPS = 4
EXPERTS_PER_GROUP = 8
N_EXPERTS = N_GROUPS * EXPERTS_PER_GROUP
D_EXPERT = 512
EPS = 1e-6
NEG_INF = -1e30
LOG2E = math.log2(math.e)
HEAD_SHIFT = HEAD_DIM.bit_length() - 1
BLOCK_SHIFT = BLOCK.bit_length() - 1
GROUP_SHIFT = EXPERTS_PER_GROUP.bit_length() - 1

LANES = 128
RANK_PAD = 128
R0, K0, V0 = 0, C_RWKV, 2 * C_RWKV
WD0 = 3 * C_RWKV
WD1 = WD0 + RANK_PAD
AD0 = WD1 + RANK_PAD
AD1 = AD0 + RANK_PAD
GD0 = AD1 + RANK_PAD
RW_COLS = GD0 + GATE_RANK
AT_COLS = ATT_WIDTH + 2 * ATT_KV_WIDTH

CHUNK = 64
MOE_TILE = 256
MOE_ROW_SLOTS = 4
VMEM_LIMIT = 56 * 1024 * 1024


def _cparams(sem, vmem=VMEM_LIMIT):
    return pltpu.CompilerParams(dimension_semantics=sem, vmem_limit_bytes=vmem)


def _sigmoid(x):
    return 1.0 / (1.0 + jnp.exp(-x))


def _dot(a, b, prec=None):
    return jnp.dot(a, b, preferred_element_type=F32, precision=prec)


def _dot_nt(a, b, prec=None):
    return lax.dot_general(a, b, (((1,), (1,)), ((), ())), preferred_element_type=F32, precision=prec)


def _dot_tn(a, b, prec=None):
    return lax.dot_general(a, b, (((0,), (0,)), ((), ())), preferred_element_type=F32, precision=prec)


def _rms(x, g):
    ms = jnp.mean(x * x, axis=-1, keepdims=True)
    return x * lax.rsqrt(ms + EPS) * g


def _split_bf16(x):
    hi = x.astype(BF16)
    lo = (x - hi.astype(F32)).astype(BF16)
    return hi, lo


def _head_block_ones():
    ri = lax.broadcasted_iota(jnp.int32, (LANES, LANES), 0)
    ci = lax.broadcasted_iota(jnp.int32, (LANES, LANES), 1)
    return ((ri >> HEAD_SHIFT) == (ci >> HEAD_SHIFT)).astype(BF16)


def _head_sums(x, bd2):
    rows = x.shape[0]
    hi, lo = _split_bf16(x)
    outs = []
    for gi in range(x.shape[1] // LANES):
        sl = slice(gi * LANES, (gi + 1) * LANES)
        both = _dot(jnp.concatenate([hi[:, sl], lo[:, sl]], axis=0), bd2)
        outs.append(both[0:rows] + both[rows:])
    return outs[0] if len(outs) == 1 else jnp.concatenate(outs, axis=1)


def _layer_block(shape, layer, index=None):
    index = (0,) * len(shape) if index is None else index
    return pl.BlockSpec((None,) + tuple(shape), lambda *_: (layer,) + tuple(index), pipeline_mode=pl.Buffered(1))


def _norm_mm_kernel(x_ref, g_ref, w_ref, o_ref, xn_ref):
    @pl.when(pl.program_id(1) == 0)
    def _():
        xn_ref[...] = _rms(x_ref[...], g_ref[...]).astype(BF16)

    o_ref[...] = _dot(xn_ref[...], w_ref[...]).astype(o_ref.dtype)


def norm_matmul(x, g, w, layer, tm, tn, out_dtype=F32):
    n, d = x.shape
    nc = w.shape[2]
    return pl.pallas_call(
        _norm_mm_kernel,
        grid=(n // tm, nc // tn),
        in_specs=[
            pl.BlockSpec((tm, d), lambda i, j: (i, 0)),
            pl.BlockSpec((None, 1, d), lambda i, j: (layer, 0, 0)),
            pl.BlockSpec((None, d, tn), lambda i, j: (layer, 0, j)),
        ],
        out_specs=pl.BlockSpec((tm, tn), lambda i, j: (i, j)),
        out_shape=jax.ShapeDtypeStruct((n, nc), out_dtype),
        scratch_shapes=[pltpu.VMEM((tm, d), BF16)],
        compiler_params=_cparams(("parallel", "arbitrary")),
        name="norm_matmul",
    )(x, g, w)


def _in_proj_kernel(x_ref, g_ref, w_ref, u_ref, qkv_ref):
    xn = _rms(x_ref[...], g_ref[...]).astype(BF16)
    nu = u_ref.shape[1]
    u_ref[...] = _dot(xn, w_ref[:, 0:nu])
    qkv_ref[...] = _dot(xn, w_ref[:, nu:])


def in_proj(x, g, w, layer, n_rw, tm=512):
    n, d = x.shape
    nc = w.shape[2]
    return pl.pallas_call(
        _in_proj_kernel,
        grid=(n // tm,),
        in_specs=[
            pl.BlockSpec((tm, d), lambda i: (i, 0)),
            _layer_block((1, d), layer),
            _layer_block((d, nc), layer),
        ],
        out_specs=[pl.BlockSpec((tm, n_rw), lambda i: (i, 0)), pl.BlockSpec((tm, nc - n_rw), lambda i: (i, 0))],
        out_shape=[jax.ShapeDtypeStruct((n, n_rw), F32), jax.ShapeDtypeStruct((n, nc - n_rw), F32)],
        compiler_params=_cparams(("parallel",)),
        name="in_proj",
    )(x, g, w)


def _prep_kernel(seq_tiles, has_vres, *refs):
    if has_vres:
        (u_ref, hp_ref, hn_ref, mup_ref, mun_ref, w0_ref, w2_ref, a0_ref, a2_ref, gw_ref, kk_ref, ka_ref, rk_ref,
         vf_ref, v0_ref, v1_ref, v2_ref,
         r_o, v_o, kk_o, b0_o, b1_o, lw0_o, lw1_o, kd0_o, kd1_o, g_o, bg_o) = refs
    else:
        (u_ref, hp_ref, hn_ref, mup_ref, mun_ref, w0_ref, w2_ref, a0_ref, a2_ref, gw_ref, kk_ref, ka_ref, rk_ref,
         r_o, v_o, kk_o, b0_o, b1_o, lw0_o, lw1_o, kd0_o, kd1_o, g_o, bg_o) = refs
    tm = u_ref.shape[0]
    i = pl.program_id(0)
    it = i % seq_tiles
    has_prev = jnp.where(it != 0, 1.0, 0.0)
    has_next = jnp.where(it != seq_tiles - 1, 1.0, 0.0)
    row8 = lax.broadcasted_iota(jnp.int32, (8, 1), 0)

    def shifted(c0, c1):
        u = u_ref[:, c0:c1]
        pr = hp_ref[7:8, c0:c1] * has_prev
        nx = hn_ref[0:1, c0:c1] * has_next
        prev = pltpu.roll(u, 1, 0)
        nxt = pltpu.roll(u, tm - 1, 0)
        prev = jnp.concatenate([jnp.where(row8 == 0, pr, prev[0:8]), prev[8:]], axis=0)
        nxt = jnp.concatenate([nxt[:tm - 8], jnp.where(row8 == 7, nx, nxt[tm - 8:])], axis=0)
        mp = mup_ref[:, c0:c1]
        mn = mun_ref[:, c0:c1]
        return u * (1.0 - mp - mn) + mp * prev + mn * nxt

    bd2 = _head_block_ones()

    r = shifted(R0, R0 + C_RWKV)
    k = shifted(K0, K0 + C_RWKV)
    v = shifted(V0, V0 + C_RWKV)
    r_o[...] = r.astype(r_o.dtype)
    if has_vres:
        low = _dot(v.astype(BF16), v1_ref[...])
        mix = _sigmoid(v0_ref[...] + _dot(low.astype(BF16), v2_ref[...]))
        v = v + (vf_ref[...].astype(F32) - v) * mix
    v_o[...] = v.astype(v_o.dtype)
    kkraw = k * kk_ref[...]
    kk = kkraw / jnp.maximum(jnp.sqrt(_head_sums(kkraw * kkraw, bd2)), 1e-12)
    kk_o[...] = kk.astype(kk_o.dtype)
    ka = ka_ref[...]
    lw_scale = -math.exp(-0.5)
    kd_sum = None
    for z, (wc, ac, lw_o, b_o, kd_o) in enumerate(
            ((WD0, AD0, lw0_o, b0_o, kd0_o), (WD1, AD1, lw1_o, b1_o, kd1_o))):
        wd = shifted(wc, wc + RANK_PAD)
        w_pre = w0_ref[z:z + 1, :] + _dot(jnp.tanh(wd).astype(BF16), w2_ref[z])
        lw_o[...] = lw_scale * _sigmoid(w_pre)
        ad = shifted(ac, ac + RANK_PAD)
        a = _sigmoid(a0_ref[z:z + 1, :] + _dot(ad.astype(BF16), a2_ref[z]))
        b_o[...] = (kk * a).astype(b_o.dtype)
        kd = k * (1.0 + (a - 1.0) * ka)
        kd_o[...] = kd.astype(kd_o.dtype)
        kd_sum = kd if kd_sum is None else kd_sum + kd
    gd = shifted(GD0, GD0 + GATE_RANK)
    g = _dot(_sigmoid(gd).astype(BF16), gw_ref[...])
    g_o[...] = g.astype(g_o.dtype)
    bonus = _head_sums(r * kd_sum * rk_ref[...], bd2) * v
    bg_o[...] = (bonus * g).astype(bg_o.dtype)


def rwkv_prep(u, seq_len, layer, params, vres, v_first, tm=256):
    n = u.shape[0]
    seq_tiles = seq_len // tm
    hb = tm // 8
    nblk8 = n // 8
    has_vres = vres is not None
    c = C_RWKV
    in_specs = [
        pl.BlockSpec((tm, RW_COLS), lambda i: (i, 0)),
        pl.BlockSpec((8, RW_COLS), lambda i: (jnp.maximum(i * hb - 1, 0), 0)),
        pl.BlockSpec((8, RW_COLS), lambda i: (jnp.minimum((i + 1) * hb, nblk8 - 1), 0)),
    ] + [_layer_block(p.shape[1:], layer) for p in params]
    args = [u, u, u] + list(params)
    if has_vres:
        in_specs += [pl.BlockSpec((tm, c), lambda i: (i, 0))] + [_layer_block(p.shape[1:], layer - 1) for p in vres]
        args += [v_first] + list(vres)
    out_spec = pl.BlockSpec((tm, c), lambda i: (i, 0))
    dtypes = [BF16] * 5 + [F32] * 2 + [BF16] * 4
    return pl.pallas_call(
        functools.partial(_prep_kernel, seq_tiles, has_vres),
        grid=(n // tm,),
        in_specs=in_specs,
        out_specs=[out_spec] * len(dtypes),
        out_shape=[jax.ShapeDtypeStruct((n, c), dt) for dt in dtypes],
        compiler_params=_cparams(("parallel",)),
        name="rwkv_prep",
    )(*args)


def _chunk_kernel(rf, vf, kkf, af, lwf, kdf, rb, vb, kkb, ab, lwb, kdb, yf_o, yb_o, s_ref):
    nbatch, L, width = rf.shape
    L2 = 2 * L
    npairs = width // LANES

    @pl.when(pl.program_id(0) == 0)
    def _():
        s_ref[...] = jnp.zeros_like(s_ref)

    lane = lax.broadcasted_iota(jnp.int32, (1, LANES), 1)
    m1 = lane < HEAD_DIM
    t_i = lax.broadcasted_iota(jnp.int32, (L, L2), 0)
    s_i = lax.broadcasted_iota(jnp.int32, (L, L2), 1) & (L - 1)
    ri = lax.broadcasted_iota(jnp.int32, (LANES, LANES), 0)
    ci = lax.broadcasted_iota(jnp.int32, (LANES, LANES), 1)
    same_head = (ri >> HEAD_SHIFT) == (ci >> HEAD_SHIFT)

    def stack(x):
        z = jnp.zeros_like(x)
        return jnp.concatenate([jnp.where(m1, x, z), jnp.where(m1, z, x)], axis=0)

    chains = []
    for rev, ins in ((False, (rf, vf, kkf, af, lwf, kdf)), (True, (rb, vb, kkb, ab, lwb, kdb))):
        strict = (s_i > t_i) if rev else (s_i < t_i)
        incl = (s_i >= t_i) if rev else (s_i <= t_i)
        incl4 = jnp.concatenate([incl, incl], axis=1)
        tri2 = incl.astype(BF16)
        for bi in range(nbatch):
            r, vb16, kk, bn, lw, kd = (x[bi] for x in ins)
            r, kk, bn, kd = (x.astype(F32) for x in (r, kk, bn, kd))
            c = _dot(tri2, jnp.concatenate(_split_bf16(lw), axis=0))
            c_last = c[0:1, :] if rev else c[L - 1:L, :]
            decay_last = jnp.exp(c_last)
            e_pos = jnp.exp(c)
            e_neg = jnp.exp(-c)
            e_rem = jnp.exp(c_last - c)
            a_t = (-kk * jnp.exp(c - lw)).astype(BF16)
            b_t = (bn * e_neg).astype(BF16)
            k_t = (kd * e_neg).astype(BF16)
            r_t = (r * e_pos).astype(BF16)
            b_h = (bn * e_rem).astype(BF16)
            k_h = (kd * e_rem).astype(BF16)
            for pr in range(npairs):
                sl = slice(pr * LANES, (pr + 1) * LANES)
                chains.append(dict(
                    strict=strict, incl4=incl4, s_ref=s_ref.at[1 if rev else 0, bi, pr],
                    decay_last=decay_last[:, sl],
                    ar=jnp.concatenate([a_t[:, sl], r_t[:, sl]], axis=0),
                    bk=jnp.concatenate([stack(b_t[:, sl]), stack(k_t[:, sl])], axis=0),
                    bkh=jnp.concatenate([b_h[:, sl], k_h[:, sl]], axis=0),
                    v=vb16[:, sl], v_s=stack(vb16[:, sl])))
    for ch in chains:
        ch["s0"] = ch["s_ref"][...]
        ch["sc"] = _dot_nt(ch["ar"], ch["bk"])
    for ch in chains:
        ch["from_state"] = _dot_nt(ch["ar"], ch["s0"].astype(BF16))
    for ch in chains:
        sc = ch["sc"]
        m_ak = jnp.where(ch["strict"], sc[0:L, L2:], 0.0).astype(BF16)
        ch["m_r"] = jnp.where(ch["incl4"], sc[L:, :], 0.0).astype(BF16)
        ch["p"] = jnp.where(ch["strict"], sc[0:L, 0:L2], 0.0)
        ch["x"] = ch["from_state"][0:L] + _dot(m_ak, ch["v_s"])
    n_steps = int(math.log2(L))
    for step in range(n_steps):
        for ch in chains:
            pb = ch["p"].astype(BF16)
            xs = stack(ch["x"].astype(BF16))
            if step < n_steps - 1:
                res = _dot(pb, jnp.concatenate([xs, stack(pb)], axis=1))
                ch["x"] = ch["x"] + res[:, 0:LANES]
                ch["p"] = res[:, LANES:]
            else:
                ch["x"] = ch["x"] + _dot(pb, xs)
    for ch in chains:
        ub = ch["x"].astype(BF16)
        ch["ub"] = ub
        ch["y"] = ch["from_state"][L:] + _dot(ch["m_r"], jnp.concatenate([stack(ub), ch["v_s"]], axis=0))
    for ch in chains:
        upd = _dot_tn(jnp.concatenate([ch["ub"], ch["v"]], axis=0), ch["bkh"])
        ch["s_ref"][...] = ch["s0"] * ch["decay_last"] + jnp.where(same_head, upd, 0.0)
    for di, y_o in enumerate((yf_o, yb_o)):
        for bi in range(nbatch):
            group = chains[(di * nbatch + bi) * npairs:(di * nbatch + bi + 1) * npairs]
            y_o[bi] = jnp.concatenate([ch["y"] for ch in group], axis=1).astype(y_o.dtype)


def rwkv_chunk(r, v, kk, b0, b1, lw0, lw1, kd0, kd1, batch, seq_len):
    c = C_RWKV
    L = CHUNK
    nc = seq_len // L

    def r3(x):
        return x.reshape(batch, seq_len, c)

    fwd = pl.BlockSpec((batch, L, c), lambda t: (0, t, 0))
    bwd = pl.BlockSpec((batch, L, c), lambda t: (0, nc - 1 - t, 0))
    sds = jax.ShapeDtypeStruct((batch, seq_len, c), BF16)
    yf, yb = pl.pallas_call(
        _chunk_kernel,
        grid=(nc,),
        in_specs=[fwd] * 6 + [bwd] * 6,
        out_specs=[fwd, bwd],
        out_shape=[sds, sds],
        scratch_shapes=[pltpu.VMEM((2, batch, c // LANES, LANES, LANES), F32)],
        compiler_params=_cparams(("arbitrary",)),
        name="rwkv_chunk",
    )(r3(r), r3(v), r3(kk), r3(b0), r3(lw0), r3(kd0), r3(r), r3(v), r3(kk), r3(b1), r3(lw1), r3(kd1))
    return yf.reshape(-1, c), yb.reshape(-1, c)


def _post_kernel(yf, yb, g, bg, lng, lnb, o_ref):
    bd2 = _head_block_ones()
    y = yf[...].astype(F32) + yb[...].astype(F32)
    mu = _head_sums(y, bd2) * (1.0 / HEAD_DIM)
    d = y - mu
    var = _head_sums(d * d, bd2) * (1.0 / HEAD_DIM)
    yn = d * lax.rsqrt(var + GN_EPS) * lng[...] + lnb[...]
    o_ref[...] = (yn * g[...].astype(F32) + bg[...].astype(F32)).astype(o_ref.dtype)


def rwkv_post(yf, yb, g, bg, ln_g, ln_b, layer, tm=512):
    n, c = yf.shape
    blk = pl.BlockSpec((tm, c), lambda i: (i, 0))
    vec = _layer_block((1, c), layer)
    return pl.pallas_call(
        _post_kernel,
        grid=(n // tm,),
        in_specs=[blk] * 4 + [vec] * 2,
        out_specs=blk,
        out_shape=jax.ShapeDtypeStruct((n, c), BF16),
        compiler_params=_cparams(("parallel",)),
        name="rwkv_post",
    )(yf, yb, g, bg, ln_g, ln_b)


def _attn_kernel(layer, q_ref, kp, kc, kn, vp, vc, vn, bias_ref, sink_ref, gain_ref, o_ref, bias_s):
    n = pl.program_id(1)
    nb = pl.num_programs(1)
    j = lax.broadcasted_iota(jnp.int32, (1, 1, 3 * BLOCK), 2)
    off_seq = ((j < BLOCK) & (n == 0)) | ((j >= 2 * BLOCK) & (n == nb - 1))

    @pl.when((n == 0) | (n == 1) | (n == nb - 1))
    def _():
        bias_s[...] = bias_ref[...] + jnp.where(off_seq, NEG_INF, 0.0)

    kb = jnp.concatenate([kp[...], kc[...], kn[...]], axis=0).astype(BF16)
    vb = jnp.concatenate([vp[...], vc[...], vn[...]], axis=0).astype(BF16)
    q = (q_ref[...] * (HEAD_DIM ** -0.5 * LOG2E)).astype(BF16)
    group = ATT_Q_HEADS // ATT_KV_HEADS
    row_head = lax.broadcasted_iota(jnp.int32, (group * BLOCK, 1), 0) >> BLOCK_SHIFT

    def scores(kh):
        qg = jnp.concatenate([q[:, (kh * group + gi) * HEAD_DIM:(kh * group + gi + 1) * HEAD_DIM]
                              for gi in range(group)], axis=0)
        return _dot_nt(qg, kb[:, kh * HEAD_DIM:(kh + 1) * HEAD_DIM])

    def softmax(kh, s):
        s = s + bias_s[kh * group:(kh + 1) * group].reshape(group * BLOCK, 3 * BLOCK)
        sk = jnp.zeros((group * BLOCK, 1), F32)
        for gi in range(group):
            sk = jnp.where(row_head == gi, sink_ref[layer, kh * group + gi] * LOG2E, sk)
        m = jnp.maximum(jnp.max(s, axis=-1, keepdims=True), sk)
        p = jnp.exp2(s - m)
        denom = jnp.sum(p, axis=-1, keepdims=True) + jnp.exp2(sk - m)
        return p.astype(BF16), denom

    all_scores = [scores(kh) for kh in range(ATT_KV_HEADS)]
    probs = [softmax(kh, s) for kh, s in enumerate(all_scores)]
    outs = []
    for kh, (p, denom) in enumerate(probs):
        og = _dot(p, vb[:, kh * HEAD_DIM:(kh + 1) * HEAD_DIM]) / denom
        outs += [og[gi * BLOCK:(gi + 1) * BLOCK] for gi in range(group)]
    o = jnp.concatenate(outs, axis=-1)
    o_ref[...] = _rms(o, gain_ref[...]).astype(o_ref.dtype)


def window_attention(qkv, bias, sink, gain, layer, batch, seq_len):
    n = qkv.shape[0]
    nb = seq_len // BLOCK
    kcol = ATT_WIDTH // ATT_KV_WIDTH
    vcol = kcol + 1

    def rows(off):
        def f(b, t):
            return jnp.clip(t + off, 0, nb - 1) + b * nb
        return f

    def kv_spec(col, off):
        f = rows(off)
        return pl.BlockSpec((BLOCK, ATT_KV_WIDTH), lambda b, t: (f(b, t), col))

    return pl.pallas_call(
        functools.partial(_attn_kernel, layer),
        grid=(batch, nb),
        in_specs=[
            pl.BlockSpec((BLOCK, ATT_WIDTH), lambda b, t: (b * nb + t, 0)),
            kv_spec(kcol, -1), kv_spec(kcol, 0), kv_spec(kcol, 1),
            kv_spec(vcol, -1), kv_spec(vcol, 0), kv_spec(vcol, 1),
            pl.BlockSpec((ATT_Q_HEADS, BLOCK, 3 * BLOCK), lambda b, t: (0, 0, 0)),
            pl.BlockSpec(memory_space=pltpu.SMEM),
            _layer_block((1, ATT_WIDTH), layer),
        ],
        out_specs=pl.BlockSpec((BLOCK, ATT_WIDTH), lambda b, t: (b * nb + t, 0)),
        out_shape=jax.ShapeDtypeStruct((n, ATT_WIDTH), BF16),
        scratch_shapes=[pltpu.VMEM((ATT_Q_HEADS, BLOCK, 3 * BLOCK), F32)],
        compiler_params=_cparams(("arbitrary", "arbitrary")),
        name="window_attention",
    )(qkv, qkv, qkv, qkv, qkv, qkv, qkv, bias, sink, gain)


def _t5_bias(rel_bias_table):
    i = jnp.arange(BLOCK)[:, None]
    j = jnp.arange(3 * BLOCK)[None, :]
    rel = j - BLOCK - i
    nbk = N_BUCKETS // 2
    max_exact = nbk // 2
    ret = jnp.where(rel > 0, nbk, 0)
    nabs = jnp.abs(rel)
    large = max_exact + (jnp.log(jnp.maximum(nabs, max_exact).astype(jnp.float32) / max_exact)
                         / math.log(MAX_DISTANCE / max_exact) * (nbk - max_exact)).astype(jnp.int32)
    large = jnp.minimum(large, nbk - 1)
    bucket = ret + jnp.where(nabs < max_exact, nabs, large)
    onehot = (bucket[None, :, :] == jnp.arange(N_BUCKETS)[:, None, None]).astype(F32)
    bias = jnp.einsum("bh,bij->hij", rel_bias_table.astype(F32), onehot, precision=HI)
    return jnp.where((nabs <= WINDOW)[None], bias * LOG2E, NEG_INF)


def _outproj_kernel(yr, ya, w1, w2, h_ref, o_ref):
    o_ref[...] = h_ref[...] + _dot(yr[...], w1[...]) + _dot(ya[...], w2[...])


def out_proj(y_r, y_a, w, layer, h, tm=512):
    n, d = h.shape
    c = y_r.shape[1]
    return pl.pallas_call(
        _outproj_kernel,
        grid=(n // tm,),
        in_specs=[
            pl.BlockSpec((tm, c), lambda i: (i, 0)),
            pl.BlockSpec((tm, c), lambda i: (i, 0)),
            _layer_block((c, d), layer, (0, 0)),
            _layer_block((c, d), layer, (1, 0)),
            pl.BlockSpec((tm, d), lambda i: (i, 0)),
        ],
        out_specs=pl.BlockSpec((tm, d), lambda i: (i, 0)),
        out_shape=jax.ShapeDtypeStruct((n, d), F32),
        compiler_params=_cparams(("parallel",)),
        name="out_proj",
    )(y_r, y_a, w, w, h)


def _xatt_kernel(h_ref, g_ref, wq, kv_ref, wo, gm_ref, wr_ref, br_ref, o_ref, slab_ref):
    h = h_ref[...]
    hn = _rms(h, g_ref[...]).astype(BF16)
    q = (_dot(hn, wq[...]) * (XATT_HEAD_DIM ** -0.5 * LOG2E)).astype(BF16)
    kv = kv_ref[...]
    outs = []
    for hd in range(XATT_HEADS):
        sl = slice(hd * XATT_HEAD_DIM, (hd + 1) * XATT_HEAD_DIM)
        k_h = kv[:, sl].astype(BF16)
        v_h = kv[:, XATT_WIDTH + hd * XATT_HEAD_DIM: XATT_WIDTH + (hd + 1) * XATT_HEAD_DIM].astype(BF16)
        s = _dot_nt(q[:, sl], k_h)
        m = jnp.max(s, axis=-1, keepdims=True)
        p = jnp.exp2(s - m)
        p = p / jnp.sum(p, axis=-1, keepdims=True)
        outs.append(_dot(p.astype(BF16), v_h))
    o = jnp.concatenate(outs, axis=-1).astype(BF16)
    h_new = h + _dot(o, wo[...])
    o_ref[...] = h_new
    slab_ref[...] = _route(h_new, gm_ref[...], wr_ref[...], br_ref[...])


def cross_attention_router(h, g, wq, kv, wo, g_moe, w_r, b_r, layer, batch, seq_len, tm=512):
    n, d = h.shape
    m = kv.shape[0] // batch
    tiles = seq_len // tm
    return pl.pallas_call(
        _xatt_kernel,
        grid=(n // tm,),
        in_specs=[
            pl.BlockSpec((tm, d), lambda i: (i, 0)),
            _layer_block((1, d), layer),
            _layer_block((d, XATT_WIDTH), layer),
            pl.BlockSpec((m, 2 * XATT_WIDTH), lambda i: (i // tiles, 0)),
            _layer_block((XATT_WIDTH, d), layer),
            _layer_block((1, d), layer),
            _layer_block((d, 2 * LANES), layer),
            _layer_block((1, LANES), layer),
        ],
        out_specs=[pl.BlockSpec((tm, d), lambda i: (i, 0)), pl.BlockSpec((tm, LANES), lambda i: (i, 0))],
        out_shape=[jax.ShapeDtypeStruct((n, d), F32), jax.ShapeDtypeStruct((n, LANES), F32)],
        compiler_params=_cparams(("parallel",)),
        name="cross_attention_router",
    )(h, g, wq, kv, wo, g_moe, w_r, b_r)


def _route(h, g, w2, b):
    hn_hi, hn_lo = _split_bf16(_rms(h, g))
    both = _dot(hn_hi, w2) + _dot(hn_lo, w2)
    logits = both[:, 0:LANES] + both[:, LANES:] + b
    tm = logits.shape[0]
    lane = lax.broadcasted_iota(jnp.int32, (tm, LANES), 1)
    is_coarse = (lane >= N_EXPERTS) & (lane < N_EXPERTS + N_GROUPS)
    cl = jnp.where(is_coarse, logits, NEG_INF)
    cmax = jnp.max(cl, axis=-1, keepdims=True)
    csum = jnp.sum(jnp.where(is_coarse, jnp.exp(cl - cmax), 0.0), axis=-1, keepdims=True)
    p_g = 1.0 / csum
    lane_f = lane.astype(F32)
    grp_f = (lane >> GROUP_SHIFT).astype(F32)
    big = float(LANES)
    g_lane = jnp.min(jnp.where(is_coarse & (cl == cmax), lane_f, big), axis=-1, keepdims=True)
    g_idx = g_lane - float(N_EXPERTS)
    in_grp = (lane < N_EXPERTS) & (grp_f == g_idx)
    fl = jnp.where(in_grp, logits, NEG_INF)
    m1 = jnp.max(fl, axis=-1, keepdims=True)
    i1 = jnp.min(jnp.where(in_grp & (fl == m1), lane_f, big), axis=-1, keepdims=True)
    fl2 = jnp.where(lane_f == i1, NEG_INF, fl)
    m2 = jnp.max(fl2, axis=-1, keepdims=True)
    i2 = jnp.min(jnp.where(in_grp & (lane_f != i1) & (fl2 == m2), lane_f, big), axis=-1, keepdims=True)
    e2 = jnp.exp(m2 - m1)
    w1 = p_g / (1.0 + e2)
    w2 = p_g * e2 / (1.0 + e2)
    return jnp.where(lane == 0, i1,
                     jnp.where(lane == 1, i2,
                               jnp.where(lane == 2, w1, jnp.where(lane == 3, w2, 0.0))))


GATHER_UNROLL = 8


def _start_row_gather(src_hbm, dst, sem, idx_ref, first, stride, n_rows):
    def body(r, carry):
        row = idx_ref[first + r * stride]
        pltpu.make_async_copy(src_hbm.at[pl.ds(row, 1), :], dst.at[pl.ds(r, 1), :], sem).start()
        return carry

    lax.fori_loop(0, n_rows, body, 0, unroll=GATHER_UNROLL)


def _wait_row_gather(dst, sem):
    pltpu.make_async_copy(dst, dst, sem).wait()


def _start_row_groups(src_hbm, dst, sem, idx_ref, first, n_groups):
    def body(grp, carry):
        for j in range(GATHER_UNROLL):
            r = grp * GATHER_UNROLL + j
            row = idx_ref[first + r]
            pltpu.make_async_copy(src_hbm.at[pl.ds(row, 1), :], dst.at[pl.ds(r, 1), :], sem).start()
        return carry

    lax.fori_loop(0, n_groups, body, 0)


def _wait_row_groups(dst, sem, n_groups):
    def body(grp, carry):
        blk = dst.at[pl.ds(0, GATHER_UNROLL), :]
        pltpu.make_async_copy(blk, blk, sem).wait()
        return carry

    lax.fori_loop(0, n_groups, body, 0)


def _expert_kernel(e0, te_ref, nxt_ref, par_ref, grp_ref, nused_ref, tok_ref, h_hbm, g_ref, wg_hbm, wu_hbm, wd_hbm,
                   o_ref, xbuf, wgf, wuf, wdf, wgb, wub, wdb, xsem, wsem):
    t = pl.program_id(0)
    nslots, tm = xbuf.shape[0], xbuf.shape[1]
    ahead = nslots - 1
    n_used = nused_ref[0]
    slot = lax.rem(t, nslots)

    def weight_copies(expert, ws):
        return [pltpu.make_async_copy(src.at[e0 + expert], dst.at[ws], wsem.at[ws, j])
                for j, (src, dst) in enumerate(((wg_hbm, wgf), (wu_hbm, wuf), (wd_hbm, wdf)))]

    @pl.when(t == 0)
    def _():
        for cp in weight_copies(te_ref[0], 0):
            cp.start()
        xbuf[...] = jnp.zeros_like(xbuf)
        for j in range(ahead):
            @pl.when(j < n_used)
            def _():
                _start_row_groups(h_hbm, xbuf.at[j], xsem.at[j], tok_ref, j * tm, grp_ref[j])

    @pl.when(t + ahead < n_used)
    def _():
        nslot = lax.rem(t + ahead, nslots)
        _start_row_groups(h_hbm, xbuf.at[nslot], xsem.at[nslot], tok_ref, (t + ahead) * tm, grp_ref[t + ahead])

    @pl.when((t == 0) | (te_ref[t] != te_ref[jnp.maximum(t - 1, 0)]))
    def _():
        ws = par_ref[t]
        for cp in weight_copies(te_ref[t], ws):
            cp.wait()

        @pl.when(nxt_ref[t] >= 0)
        def _():
            for cp in weight_copies(nxt_ref[t], 1 - ws):
                cp.start()

        wgb[...] = wgf[ws].astype(BF16)
        wub[...] = wuf[ws].astype(BF16)
        wdb[...] = wdf[ws].astype(BF16)

    @pl.when(t < n_used)
    def _():
        _wait_row_groups(xbuf.at[slot], xsem.at[slot], grp_ref[t])
        xn = _rms(xbuf[slot], g_ref[...]).astype(BF16)
        hg = _dot(xn, wgb[...])
        hu = _dot(xn, wub[...])
        act = (hg * _sigmoid(hg) * hu).astype(BF16)
        o_ref[...] = _dot(act, wdb[...])

    @pl.when(t >= n_used)
    def _():
        o_ref[...] = jnp.zeros_like(o_ref)


def moe_experts(h, g, w_gate, w_up, w_down, layer, tables, n_tiles):
    n, d = h.shape
    tm = MOE_TILE
    tile_expert, next_expert, parity, tile_groups, n_used, row_tok = tables
    grid_spec = pltpu.PrefetchScalarGridSpec(
        num_scalar_prefetch=6,
        grid=(n_tiles,),
        in_specs=[
            pl.BlockSpec(memory_space=pl.ANY),
            _layer_block((1, d), layer),
            pl.BlockSpec(memory_space=pl.ANY),
            pl.BlockSpec(memory_space=pl.ANY),
            pl.BlockSpec(memory_space=pl.ANY),
        ],
        out_specs=pl.BlockSpec((tm, d), lambda t, *_: (t, 0)),
        scratch_shapes=[pltpu.VMEM((MOE_ROW_SLOTS, tm, d), F32),
                        pltpu.VMEM((2, d, D_EXPERT), F32), pltpu.VMEM((2, d, D_EXPERT), F32),
                        pltpu.VMEM((2, D_EXPERT, d), F32),
                        pltpu.VMEM((d, D_EXPERT), BF16), pltpu.VMEM((d, D_EXPERT), BF16),
                        pltpu.VMEM((D_EXPERT, d), BF16),
                        pltpu.SemaphoreType.DMA((MOE_ROW_SLOTS,)), pltpu.SemaphoreType.DMA((2, 3))],
    )
    return pl.pallas_call(
        functools.partial(_expert_kernel, layer * N_EXPERTS),
        grid_spec=grid_spec,
        out_shape=jax.ShapeDtypeStruct((n_tiles * tm, d), F32),
        compiler_params=_cparams(("arbitrary",)),
        name="moe_experts",
    )(tile_expert, next_expert, parity, tile_groups, n_used, row_tok, h, g, w_gate, w_up, w_down)


def _combine_kernel(final, pos_ref, y_hbm, h_ref, slab_ref, fg_ref, o_ref, buf, sem):
    i = pl.program_id(0)
    n_steps = pl.num_programs(0)
    tm = h_ref.shape[0]
    slot = i % 2

    def start(step, sl):
        for j in range(2):
            _start_row_gather(y_hbm, buf.at[sl, j], sem.at[sl, j], pos_ref, 2 * step * tm + j, 2, tm)

    @pl.when(i == 0)
    def _():
        start(0, 0)

    @pl.when(i + 1 < n_steps)
    def _():
        start(i + 1, 1 - slot)

    for j in range(2):
        _wait_row_gather(buf.at[slot, j], sem.at[slot, j])
    slab = slab_ref[...]
    out = h_ref[...] + slab[:, 2:3] * buf[slot, 0] + slab[:, 3:4] * buf[slot, 1]
    if final:
        out = _rms(out, fg_ref[...])
    o_ref[...] = out


def moe_combine(y_sorted, pos, h, slab, final_gain, final, tm=256):
    n, d = h.shape
    grid_spec = pltpu.PrefetchScalarGridSpec(
        num_scalar_prefetch=1,
        grid=(n // tm,),
        in_specs=[
            pl.BlockSpec(memory_space=pl.ANY),
            pl.BlockSpec((tm, d), lambda i, p: (i, 0)),
            pl.BlockSpec((tm, LANES), lambda i, p: (i, 0)),
            pl.BlockSpec((1, d), lambda i, p: (0, 0)),
        ],
        out_specs=pl.BlockSpec((tm, d), lambda i, p: (i, 0)),
        scratch_shapes=[pltpu.VMEM((2, 2, tm, d), F32), pltpu.SemaphoreType.DMA((2, 2))],
    )
    return pl.pallas_call(
        functools.partial(_combine_kernel, final),
        grid_spec=grid_spec,
        out_shape=jax.ShapeDtypeStruct((n, d), F32),
        compiler_params=_cparams(("arbitrary",)),
        name="moe_combine",
    )(pos, y_sorted, h, slab, final_gain.reshape(1, d))


def _invert_kernel(pos_ref, zeros_hbm, out_ref, sem):
    fill = pltpu.make_async_copy(zeros_hbm, out_ref, sem)
    fill.start()
    fill.wait()
    n_pairs = pos_ref.shape[0]

    def place(grp, carry):
        for j in range(GATHER_UNROLL):
            i = grp * GATHER_UNROLL + j
            out_ref[pos_ref[i]] = i >> 1
        return carry

    lax.fori_loop(0, n_pairs // GATHER_UNROLL, place, 0)


def invert_rows(pos, n_rows):
    return pl.pallas_call(
        _invert_kernel,
        in_specs=[pl.BlockSpec(memory_space=pltpu.SMEM), pl.BlockSpec(memory_space=pl.ANY)],
        out_specs=pl.BlockSpec(memory_space=pltpu.SMEM),
        out_shape=jax.ShapeDtypeStruct((n_rows,), jnp.int32),
        scratch_shapes=[pltpu.SemaphoreType.DMA(())],
        name="invert_rows",
    )(pos, jnp.zeros((n_rows,), jnp.int32))


def _routing_tables(slab, n_tiles):
    n = slab.shape[0]
    tm = MOE_TILE
    e = slab[:, 0:2].astype(jnp.int32).reshape(-1)
    onehot = (e[:, None] == jnp.arange(N_EXPERTS, dtype=jnp.int32)[None, :]).astype(jnp.int32)
    csum = jnp.cumsum(onehot, axis=0)
    counts = csum[-1]
    rank = jnp.take_along_axis(csum, e[:, None], axis=1)[:, 0] - 1
    tiles_per = (counts + tm - 1) // tm
    tile_end = jnp.cumsum(tiles_per)
    tile_start = tile_end - tiles_per
    pos = tile_start[e] * tm + rank
    row_tok = invert_rows(pos.astype(jnp.int32), n_tiles * tm)
    n_used = tile_end[-1]
    t_idx = jnp.arange(n_tiles, dtype=jnp.int32)
    tile_expert = jnp.searchsorted(tile_end, jnp.minimum(t_idx, n_used - 1), side="right").astype(jnp.int32)
    tile_expert = jnp.minimum(tile_expert, N_EXPERTS - 1)
    after = tile_end[tile_expert]
    next_expert = jnp.where(after < n_used, tile_expert[jnp.minimum(after, n_tiles - 1)], -1).astype(jnp.int32)
    change = jnp.concatenate([jnp.zeros((1,), jnp.int32), (tile_expert[1:] != tile_expert[:-1]).astype(jnp.int32)])
    parity = (jnp.cumsum(change) % 2).astype(jnp.int32)
    rows_left = counts[tile_expert] - (t_idx - tile_start[tile_expert]) * tm
    tile_groups = jnp.where(t_idx < n_used, (jnp.clip(rows_left, 0, tm) + GATHER_UNROLL - 1) // GATHER_UNROLL, 0)
    tables = (tile_expert, next_expert, parity, tile_groups.astype(jnp.int32),
              n_used.reshape(1).astype(jnp.int32), row_tok)
    return tables, pos.astype(jnp.int32)


def _split_in_weights(w):
    c = C_RWKV
    o = 3 * c
    wd0 = w[..., o:o + DECAY_RANK]
    wd1 = w[..., o + DECAY_RANK:o + 2 * DECAY_RANK]
    o += 2 * DECAY_RANK
    ad0 = w[..., o:o + ICLR_RANK]
    ad1 = w[..., o + ICLR_RANK:o + 2 * ICLR_RANK]
    o += 2 * ICLR_RANK
    gd = w[..., o:o + GATE_RANK]
    o += GATE_RANK
    att = w[..., o:]

    def padl(x):
        pad = [(0, 0)] * (x.ndim - 1) + [(0, RANK_PAD - x.shape[-1])]
        return jnp.pad(x, pad)

    rw = jnp.concatenate([w[..., :3 * c], padl(wd0), padl(wd1), padl(ad0), padl(ad1), gd], axis=-1)
    return rw, att


def kernel(x, mem, w_in, shift_prev, shift_next, decay_w0, decay_w2, iclr_a0, iclr_a2, gate_w2, vres_v0, vres_w1, vres_w2, k_k, k_a, r_k, ln_x_gain, ln_x_bias, att_sink, att_out_gain, rel_bias_table, w_out, norm_mix, norm_xatt, mem_norm, xatt_wq, xatt_wk, xatt_wv, xatt_wo, norm_moe, router_coarse_w, router_coarse_b, router_fine_w, router_fine_b, expert_w_gate, expert_w_up, expert_w_down, final_norm):
    batch, seq_len, d = x.shape
    depth = w_in.shape[0]
    n = batch * seq_len
    mem_len = mem.shape[1]
    h = x.reshape(n, d)
    memf = mem.reshape(batch * mem_len, d)
    bias = _t5_bias(rel_bias_table)
    n_tiles = (2 * n) // MOE_TILE + N_EXPERTS

    def row(p):
        return p[:, None, :]

    def pad_axis(p, axis, size):
        pad = [(0, 0)] * p.ndim
        pad[axis] = (0, size - p.shape[axis])
        return jnp.pad(p, pad)

    w_rw, w_at = _split_in_weights(w_in.astype(BF16))
    w_in_all = jnp.concatenate([w_rw, w_at], axis=-1)
    prep_params = (
        row(_split_in_weights(shift_prev)[0]), row(_split_in_weights(shift_next)[0]),
        decay_w0, pad_axis(decay_w2, 2, RANK_PAD).astype(BF16),
        iclr_a0, pad_axis(iclr_a2, 2, RANK_PAD).astype(BF16),
        gate_w2.astype(BF16), row(k_k), row(k_a), row(r_k))
    vres_params = (row(vres_v0), pad_axis(vres_w1, 2, RANK_PAD).astype(BF16),
                   pad_axis(vres_w2, 1, RANK_PAD).astype(BF16))
    ln_g, ln_b = row(ln_x_gain), row(ln_x_bias)
    att_gain = row(att_out_gain)
    w_out_all = w_out.astype(BF16)
    g_mix, g_xatt, g_mem, g_moe = row(norm_mix), row(norm_xatt), row(mem_norm), row(norm_moe)
    wq_all = xatt_wq.astype(BF16)
    wkv_all = jnp.concatenate([xatt_wk, xatt_wv], axis=-1).astype(BF16)
    wo_all = xatt_wo.astype(BF16)
    w_r = pad_axis(jnp.concatenate([router_fine_w, router_coarse_w], axis=-1), 2, LANES)
    w_r2_all = jnp.concatenate(_split_bf16(w_r), axis=-1)
    b_r_all = row(pad_axis(jnp.concatenate([router_fine_b.reshape(depth, -1), router_coarse_b], axis=-1), 1, LANES))
    wg_all = expert_w_gate.reshape(depth * N_EXPERTS, d, D_EXPERT)
    wu_all = expert_w_up.reshape(depth * N_EXPERTS, d, D_EXPERT)
    wd_all = expert_w_down.reshape(depth * N_EXPERTS, D_EXPERT, d)

    v_first = None
    for l in range(depth):
        u, qkv = in_proj(h, g_mix, w_in_all, l, RW_COLS)
        r, v, kk, b0, b1, lw0, lw1, kd0, kd1, g, bg = rwkv_prep(
            u, seq_len, l, prep_params, vres_params if l > 0 else None, v_first)
        if l == 0:
            v_first = v
        yf, yb = rwkv_chunk(r, v, kk, b0, b1, lw0, lw1, kd0, kd1, batch, seq_len)
        y_r = rwkv_post(yf, yb, g, bg, ln_g, ln_b, l)
        y_a = window_attention(qkv, bias, att_sink, att_gain, l, batch, seq_len)
        h = out_proj(y_r, y_a, w_out_all, l, h)
        kv = norm_matmul(memf, g_mem, wkv_all, l, memf.shape[0], 2 * XATT_WIDTH)
        h, slab = cross_attention_router(h, g_xatt, wq_all, kv, wo_all, g_moe, w_r2_all, b_r_all, l, batch, seq_len)
        tables, pos = _routing_tables(slab, n_tiles)
        y_sorted = moe_experts(h, g_moe, wg_all, wu_all, wd_all, l, tables, n_tiles)
        h = moe_combine(y_sorted, pos, h, slab, final_norm, l == depth - 1)
    return h.reshape(batch, seq_len, d)
```

```python
import functools
import math

import jax
import jax.numpy as jnp
from jax import lax
from jax.experimental import pallas as pl
from jax.experimental.pallas import tpu as pltpu

F32 = jnp.float32
BF16 = jnp.bfloat16
HI = lax.Precision.HIGHEST

D_MODEL = 2048
RWKV_HEADS = 16
HEAD_DIM = 64
C_RWKV = RWKV_HEADS * HEAD_DIM
ATT_Q_HEADS = 16
ATT_KV_HEADS = 4
ATT_WIDTH = ATT_Q_HEADS * HEAD_DIM
ATT_KV_WIDTH = ATT_KV_HEADS * HEAD_DIM
WINDOW = 128
BLOCK = 128
DECAY_RANK = 96
ICLR_RANK = 96
VRES_RANK = 64
GATE_RANK = 256
GN_EPS = 64e-5
N_BUCKETS = 32
MAX_DISTANCE = 128
XATT_HEADS = 4
XATT_HEAD_DIM = 128
XATT_WIDTH = XATT_HEADS * XATT_HEAD_DIM
N_GROUPS = 4
EXPERTS_PER_GROUP = 8
N_EXPERTS = N_GROUPS * EXPERTS_PER_GROUP
D_EXPERT = 512
EPS = 1e-6
NEG_INF = -1e30
LOG2E = math.log2(math.e)
HEAD_SHIFT = HEAD_DIM.bit_length() - 1
BLOCK_SHIFT = BLOCK.bit_length() - 1
GROUP_SHIFT = EXPERTS_PER_GROUP.bit_length() - 1

LANES = 128
RANK_PAD = 128
R0, K0, V0 = 0, C_RWKV, 2 * C_RWKV
WD0 = 3 * C_RWKV
WD1 = WD0 + RANK_PAD
AD0 = WD1 + RANK_PAD
AD1 = AD0 + RANK_PAD
GD0 = AD1 + RANK_PAD
RW_COLS = GD0 + GATE_RANK
AT_COLS = ATT_WIDTH + 2 * ATT_KV_WIDTH

CHUNK = 64
MOE_TILE = 256
MOE_ROW_SLOTS = 4
VMEM_LIMIT = 56 * 1024 * 1024


def _cparams(sem, vmem=VMEM_LIMIT):
    return pltpu.CompilerParams(dimension_semantics=sem, vmem_limit_bytes=vmem)


def _sigmoid(x):
    return 1.0 / (1.0 + jnp.exp(-x))


def _dot(a, b, prec=None):
    return jnp.dot(a, b, preferred_element_type=F32, precision=prec)


def _dot_nt(a, b, prec=None):
    return lax.dot_general(a, b, (((1,), (1,)), ((), ())), preferred_element_type=F32, precision=prec)


def _dot_tn(a, b, prec=None):
    return lax.dot_general(a, b, (((0,), (0,)), ((), ())), preferred_element_type=F32, precision=prec)


def _rms(x, g):
    ms = jnp.mean(x * x, axis=-1, keepdims=True)
    return x * lax.rsqrt(ms + EPS) * g


def _split_bf16(x):
    hi = x.astype(BF16)
    lo = (x - hi.astype(F32)).astype(BF16)
    return hi, lo


def _head_block_ones():
    ri = lax.broadcasted_iota(jnp.int32, (LANES, LANES), 0)
    ci = lax.broadcasted_iota(jnp.int32, (LANES, LANES), 1)
    return ((ri >> HEAD_SHIFT) == (ci >> HEAD_SHIFT)).astype(BF16)


def _head_sums(x, bd2):
    rows = x.shape[0]
    hi, lo = _split_bf16(x)
    outs = []
    for gi in range(x.shape[1] // LANES):
        sl = slice(gi * LANES, (gi + 1) * LANES)
        both = _dot(jnp.concatenate([hi[:, sl], lo[:, sl]], axis=0), bd2)
        outs.append(both[0:rows] + both[rows:])
    return outs[0] if len(outs) == 1 else jnp.concatenate(outs, axis=1)


def _layer_block(shape, layer, index=None):
    index = (0,) * len(shape) if index is None else index
    return pl.BlockSpec((None,) + tuple(shape), lambda *_: (layer,) + tuple(index), pipeline_mode=pl.Buffered(1))


def _norm_mm_kernel(x_ref, g_ref, w_ref, o_ref, xn_ref):
    @pl.when(pl.program_id(1) == 0)
    def _():
        xn_ref[...] = _rms(x_ref[...], g_ref[...]).astype(BF16)

    o_ref[...] = _dot(xn_ref[...], w_ref[...]).astype(o_ref.dtype)


def norm_matmul(x, g, w, tm, tn, out_dtype=F32):
    n, d = x.shape
    depth, _, nc = w.shape
    return pl.pallas_call(
        _norm_mm_kernel,
        grid=(depth * (n // tm), nc // tn),
        in_specs=[
            pl.BlockSpec((tm, d), lambda i, j: (i % (n // tm), 0)),
            pl.BlockSpec((None, 1, d), lambda i, j: (i // (n // tm), 0, 0)),
            pl.BlockSpec((None, d, tn), lambda i, j: (i // (n // tm), 0, j)),
        ],
        out_specs=pl.BlockSpec((None, tm, tn), lambda i, j: (i // (n // tm), i % (n // tm), j)),
        out_shape=jax.ShapeDtypeStruct((depth, n, nc), out_dtype),
        scratch_shapes=[pltpu.VMEM((tm, d), BF16)],
        compiler_params=_cparams(("parallel", "arbitrary")),
        name="norm_matmul",
    )(x, g, w)


def _in_proj_kernel(x_ref, g_ref, w_ref, u_ref, qkv_ref):
    xn = _rms(x_ref[...], g_ref[...]).astype(BF16)
    nu = u_ref.shape[1]
    u_ref[...] = _dot(xn, w_ref[:, 0:nu])
    qkv_ref[...] = _dot(xn, w_ref[:, nu:])


def in_proj(x, g, w, layer, n_rw, tm=512):
    n, d = x.shape
    nc = w.shape[2]
    return pl.pallas_call(
        _in_proj_kernel,
        grid=(n // tm,),
        in_specs=[
            pl.BlockSpec((tm, d), lambda i: (i, 0)),
            _layer_block((1, d), layer),
            _layer_block((d, nc), layer),
        ],
        out_specs=[pl.BlockSpec((tm, n_rw), lambda i: (i, 0)), pl.BlockSpec((tm, nc - n_rw), lambda i: (i, 0))],
        out_shape=[jax.ShapeDtypeStruct((n, n_rw), F32), jax.ShapeDtypeStruct((n, nc - n_rw), F32)],
        compiler_params=_cparams(("parallel",)),
        name="in_proj",
    )(x, g, w)


def _prep_kernel(seq_tiles, has_vres, *refs):
    if has_vres:
        (u_ref, hp_ref, hn_ref, mup_ref, mun_ref, w0_ref, w2_ref, a0_ref, a2_ref, gw_ref, kk_ref, ka_ref, rk_ref,
         vf_ref, v0_ref, v1_ref, v2_ref,
         r_o, v_o, kk_o, b0_o, b1_o, lw0_o, lw1_o, kd0_o, kd1_o, g_o, bg_o) = refs
    else:
        (u_ref, hp_ref, hn_ref, mup_ref, mun_ref, w0_ref, w2_ref, a0_ref, a2_ref, gw_ref, kk_ref, ka_ref, rk_ref,
         r_o, v_o, kk_o, b0_o, b1_o, lw0_o, lw1_o, kd0_o, kd1_o, g_o, bg_o) = refs
    tm = u_ref.shape[0]
    i = pl.program_id(0)
    it = i % seq_tiles
    has_prev = jnp.where(it != 0, 1.0, 0.0)
    has_next = jnp.where(it != seq_tiles - 1, 1.0, 0.0)
    row8 = lax.broadcasted_iota(jnp.int32, (8, 1), 0)

    def shifted(c0, c1):
        u = u_ref[:, c0:c1]
        pr = hp_ref[7:8, c0:c1] * has_prev
        nx = hn_ref[0:1, c0:c1] * has_next
        prev = pltpu.roll(u, 1, 0)
        nxt = pltpu.roll(u, tm - 1, 0)
        prev = jnp.concatenate([jnp.where(row8 == 0, pr, prev[0:8]), prev[8:]], axis=0)
        nxt = jnp.concatenate([nxt[:tm - 8], jnp.where(row8 == 7, nx, nxt[tm - 8:])], axis=0)
        mp = mup_ref[:, c0:c1]
        mn = mun_ref[:, c0:c1]
        return u * (1.0 - mp - mn) + mp * prev + mn * nxt

    bd2 = _head_block_ones()

    r = shifted(R0, R0 + C_RWKV)
    k = shifted(K0, K0 + C_RWKV)
    v = shifted(V0, V0 + C_RWKV)
    r_o[...] = r.astype(r_o.dtype)
    if has_vres:
        low = _dot(v.astype(BF16), v1_ref[...])
        mix = _sigmoid(v0_ref[...] + _dot(low.astype(BF16), v2_ref[...]))
        v = v + (vf_ref[...].astype(F32) - v) * mix
    v_o[...] = v.astype(v_o.dtype)
    kkraw = k * kk_ref[...]
    kk = kkraw / jnp.maximum(jnp.sqrt(_head_sums(kkraw * kkraw, bd2)), 1e-12)
    kk_o[...] = kk.astype(kk_o.dtype)
    ka = ka_ref[...]
    lw_scale = -math.exp(-0.5)
    kd_sum = None
    for z, (wc, ac, lw_o, b_o, kd_o) in enumerate(
            ((WD0, AD0, lw0_o, b0_o, kd0_o), (WD1, AD1, lw1_o, b1_o, kd1_o))):
        wd = shifted(wc, wc + RANK_PAD)
        w_pre = w0_ref[z:z + 1, :] + _dot(jnp.tanh(wd).astype(BF16), w2_ref[z])
        lw_o[...] = lw_scale * _sigmoid(w_pre)
        ad = shifted(ac, ac + RANK_PAD)
        a = _sigmoid(a0_ref[z:z + 1, :] + _dot(ad.astype(BF16), a2_ref[z]))
        b_o[...] = (kk * a).astype(b_o.dtype)
        kd = k * (1.0 + (a - 1.0) * ka)
        kd_o[...] = kd.astype(kd_o.dtype)
        kd_sum = kd if kd_sum is None else kd_sum + kd
    gd = shifted(GD0, GD0 + GATE_RANK)
    g = _dot(_sigmoid(gd).astype(BF16), gw_ref[...])
    g_o[...] = g.astype(g_o.dtype)
    bonus = _head_sums(r * kd_sum * rk_ref[...], bd2) * v
    bg_o[...] = (bonus * g).astype(bg_o.dtype)


def rwkv_prep(u, seq_len, layer, params, vres, v_first, tm=256):
    n = u.shape[0]
    seq_tiles = seq_len // tm
    hb = tm // 8
    nblk8 = n // 8
    has_vres = vres is not None
    c = C_RWKV
    in_specs = [
        pl.BlockSpec((tm, RW_COLS), lambda i: (i, 0)),
        pl.BlockSpec((8, RW_COLS), lambda i: (jnp.maximum(i * hb - 1, 0), 0)),
        pl.BlockSpec((8, RW_COLS), lambda i: (jnp.minimum((i + 1) * hb, nblk8 - 1), 0)),
    ] + [_layer_block(p.shape[1:], layer) for p in params]
    args = [u, u, u] + list(params)
    if has_vres:
        in_specs += [pl.BlockSpec((tm, c), lambda i: (i, 0))] + [_layer_block(p.shape[1:], layer - 1) for p in vres]
        args += [v_first] + list(vres)
    out_spec = pl.BlockSpec((tm, c), lambda i: (i, 0))
    dtypes = [BF16] * 5 + [F32] * 2 + [BF16] * 4
    return pl.pallas_call(
        functools.partial(_prep_kernel, seq_tiles, has_vres),
        grid=(n // tm,),
        in_specs=in_specs,
        out_specs=[out_spec] * len(dtypes),
        out_shape=[jax.ShapeDtypeStruct((n, c), dt) for dt in dtypes],
        compiler_params=_cparams(("parallel",)),
        name="rwkv_prep",
    )(*args)


def _chunk_kernel(rf, vf, kkf, af, lwf, kdf, rb, vb, kkb, ab, lwb, kdb, yf_o, yb_o, s_ref):
    nbatch, L, width = rf.shape
    L2 = 2 * L
    npairs = width // LANES

    @pl.when(pl.program_id(0) == 0)
    def _():
        s_ref[...] = jnp.zeros_like(s_ref)

    lane = lax.broadcasted_iota(jnp.int32, (1, LANES), 1)
    m1 = lane < HEAD_DIM
    t_i = lax.broadcasted_iota(jnp.int32, (L, L2), 0)
    s_i = lax.broadcasted_iota(jnp.int32, (L, L2), 1) & (L - 1)
    ri = lax.broadcasted_iota(jnp.int32, (LANES, LANES), 0)
    ci = lax.broadcasted_iota(jnp.int32, (LANES, LANES), 1)
    same_head = (ri >> HEAD_SHIFT) == (ci >> HEAD_SHIFT)

    def stack(x):
        z = jnp.zeros_like(x)
        return jnp.concatenate([jnp.where(m1, x, z), jnp.where(m1, z, x)], axis=0)

    chains = []
    for rev, ins in ((False, (rf, vf, kkf, af, lwf, kdf)), (True, (rb, vb, kkb, ab, lwb, kdb))):
        strict = (s_i > t_i) if rev else (s_i < t_i)
        incl = (s_i >= t_i) if rev else (s_i <= t_i)
        incl4 = jnp.concatenate([incl, incl], axis=1)
        tri2 = incl.astype(BF16)
        for bi in range(nbatch):
            r, vb16, kk, bn, lw, kd = (x[bi] for x in ins)
            r, kk, bn, kd = (x.astype(F32) for x in (r, kk, bn, kd))
            c = _dot(tri2, jnp.concatenate(_split_bf16(lw), axis=0))
            c_last = c[0:1, :] if rev else c[L - 1:L, :]
            decay_last = jnp.exp(c_last)
            e_pos = jnp.exp(c)
            e_neg = jnp.exp(-c)
            e_rem = jnp.exp(c_last - c)
            a_t = (-kk * jnp.exp(c - lw)).astype(BF16)
            b_t = (bn * e_neg).astype(BF16)
            k_t = (kd * e_neg).astype(BF16)
            r_t = (r * e_pos).astype(BF16)
            b_h = (bn * e_rem).astype(BF16)
            k_h = (kd * e_rem).astype(BF16)
            for pr in range(npairs):
                sl = slice(pr * LANES, (pr + 1) * LANES)
                chains.append(dict(
                    strict=strict, incl4=incl4, s_ref=s_ref.at[1 if rev else 0, bi, pr],
                    decay_last=decay_last[:, sl],
                    ar=jnp.concatenate([a_t[:, sl], r_t[:, sl]], axis=0),
                    bk=jnp.concatenate([stack(b_t[:, sl]), stack(k_t[:, sl])], axis=0),
                    bkh=jnp.concatenate([b_h[:, sl], k_h[:, sl]], axis=0),
                    v=vb16[:, sl], v_s=stack(vb16[:, sl])))
    for ch in chains:
        ch["s0"] = ch["s_ref"][...]
        ch["sc"] = _dot_nt(ch["ar"], ch["bk"])
    for ch in chains:
        ch["from_state"] = _dot_nt(ch["ar"], ch["s0"].astype(BF16))
    for ch in chains:
        sc = ch["sc"]
        m_ak = jnp.where(ch["strict"], sc[0:L, L2:], 0.0).astype(BF16)
        ch["m_r"] = jnp.where(ch["incl4"], sc[L:, :], 0.0).astype(BF16)
        ch["p"] = jnp.where(ch["strict"], sc[0:L, 0:L2], 0.0)
        ch["x"] = ch["from_state"][0:L] + _dot(m_ak, ch["v_s"])
    n_steps = int(math.log2(L))
    for step in range(n_steps):
        for ch in chains:
            pb = ch["p"].astype(BF16)
            xs = stack(ch["x"].astype(BF16))
            if step < n_steps - 1:
                res = _dot(pb, jnp.concatenate([xs, stack(pb)], axis=1))
                ch["x"] = ch["x"] + res[:, 0:LANES]
                ch["p"] = res[:, LANES:]
            else:
                ch["x"] = ch["x"] + _dot(pb, xs)
    for ch in chains:
        ub = ch["x"].astype(BF16)
        ch["ub"] = ub
        ch["y"] = ch["from_state"][L:] + _dot(ch["m_r"], jnp.concatenate([stack(ub), ch["v_s"]], axis=0))
    for ch in chains:
        upd = _dot_tn(jnp.concatenate([ch["ub"], ch["v"]], axis=0), ch["bkh"])
        ch["s_ref"][...] = ch["s0"] * ch["decay_last"] + jnp.where(same_head, upd, 0.0)
    for di, y_o in enumerate((yf_o, yb_o)):
        for bi in range(nbatch):
            group = chains[(di * nbatch + bi) * npairs:(di * nbatch + bi + 1) * npairs]
            y_o[bi] = jnp.concatenate([ch["y"] for ch in group], axis=1).astype(y_o.dtype)


def rwkv_chunk(r, v, kk, b0, b1, lw0, lw1, kd0, kd1, batch, seq_len):
    c = C_RWKV
    L = CHUNK
    nc = seq_len // L

    def r3(x):
        return x.reshape(batch, seq_len, c)

    fwd = pl.BlockSpec((batch, L, c), lambda t: (0, t, 0))
    bwd = pl.BlockSpec((batch, L, c), lambda t: (0, nc - 1 - t, 0))
    sds = jax.ShapeDtypeStruct((batch, seq_len, c), BF16)
    yf, yb = pl.pallas_call(
        _chunk_kernel,
        grid=(nc,),
        in_specs=[fwd] * 6 + [bwd] * 6,
        out_specs=[fwd, bwd],
        out_shape=[sds, sds],
        scratch_shapes=[pltpu.VMEM((2, batch, c // LANES, LANES, LANES), F32)],
        compiler_params=_cparams(("arbitrary",)),
        name="rwkv_chunk",
    )(r3(r), r3(v), r3(kk), r3(b0), r3(lw0), r3(kd0), r3(r), r3(v), r3(kk), r3(b1), r3(lw1), r3(kd1))
    return yf.reshape(-1, c), yb.reshape(-1, c)


def _post_kernel(yf, yb, g, bg, lng, lnb, o_ref):
    bd2 = _head_block_ones()
    y = yf[...].astype(F32) + yb[...].astype(F32)
    mu = _head_sums(y, bd2) * (1.0 / HEAD_DIM)
    d = y - mu
    var = _head_sums(d * d, bd2) * (1.0 / HEAD_DIM)
    yn = d * lax.rsqrt(var + GN_EPS) * lng[...] + lnb[...]
    o_ref[...] = (yn * g[...].astype(F32) + bg[...].astype(F32)).astype(o_ref.dtype)


def rwkv_post(yf, yb, g, bg, ln_g, ln_b, layer, tm=512):
    n, c = yf.shape
    blk = pl.BlockSpec((tm, c), lambda i: (i, 0))
    vec = _layer_block((1, c), layer)
    return pl.pallas_call(
        _post_kernel,
        grid=(n // tm,),
        in_specs=[blk] * 4 + [vec] * 2,
        out_specs=blk,
        out_shape=jax.ShapeDtypeStruct((n, c), BF16),
        compiler_params=_cparams(("parallel",)),
        name="rwkv_post",
    )(yf, yb, g, bg, ln_g, ln_b)


def _attn_kernel(layer, q_ref, kp, kc, kn, vp, vc, vn, bias_ref, sink_ref, gain_ref, o_ref, bias_s):
    n = pl.program_id(1)
    nb = pl.num_programs(1)
    j = lax.broadcasted_iota(jnp.int32, (1, 1, 3 * BLOCK), 2)
    off_seq = ((j < BLOCK) & (n == 0)) | ((j >= 2 * BLOCK) & (n == nb - 1))

    @pl.when((n == 0) | (n == 1) | (n == nb - 1))
    def _():
        bias_s[...] = bias_ref[...] + jnp.where(off_seq, NEG_INF, 0.0)

    kb = jnp.concatenate([kp[...], kc[...], kn[...]], axis=0).astype(BF16)
    vb = jnp.concatenate([vp[...], vc[...], vn[...]], axis=0).astype(BF16)
    q = (q_ref[...] * (HEAD_DIM ** -0.5 * LOG2E)).astype(BF16)
    group = ATT_Q_HEADS // ATT_KV_HEADS
    row_head = lax.broadcasted_iota(jnp.int32, (group * BLOCK, 1), 0) >> BLOCK_SHIFT

    def scores(kh):
        qg = jnp.concatenate([q[:, (kh * group + gi) * HEAD_DIM:(kh * group + gi + 1) * HEAD_DIM]
                              for gi in range(group)], axis=0)
        return _dot_nt(qg, kb[:, kh * HEAD_DIM:(kh + 1) * HEAD_DIM])

    def softmax(kh, s):
        s = s + bias_s[kh * group:(kh + 1) * group].reshape(group * BLOCK, 3 * BLOCK)
        sk = jnp.zeros((group * BLOCK, 1), F32)
        for gi in range(group):
            sk = jnp.where(row_head == gi, sink_ref[layer, kh * group + gi] * LOG2E, sk)
        m = jnp.maximum(jnp.max(s, axis=-1, keepdims=True), sk)
        p = jnp.exp2(s - m)
        denom = jnp.sum(p, axis=-1, keepdims=True) + jnp.exp2(sk - m)
        return p.astype(BF16), denom

    all_scores = [scores(kh) for kh in range(ATT_KV_HEADS)]
    probs = [softmax(kh, s) for kh, s in enumerate(all_scores)]
    outs = []
    for kh, (p, denom) in enumerate(probs):
        og = _dot(p, vb[:, kh * HEAD_DIM:(kh + 1) * HEAD_DIM]) / denom
        outs += [og[gi * BLOCK:(gi + 1) * BLOCK] for gi in range(group)]
    o = jnp.concatenate(outs, axis=-1)
    o_ref[...] = _rms(o, gain_ref[...]).astype(o_ref.dtype)


def window_attention(qkv, bias, sink, gain, layer, batch, seq_len):
    n = qkv.shape[0]
    nb = seq_len // BLOCK
    kcol = ATT_WIDTH // ATT_KV_WIDTH
    vcol = kcol + 1

    def rows(off):
        def f(b, t):
            return jnp.clip(t + off, 0, nb - 1) + b * nb
        return f

    def kv_spec(col, off):
        f = rows(off)
        return pl.BlockSpec((BLOCK, ATT_KV_WIDTH), lambda b, t: (f(b, t), col))

    return pl.pallas_call(
        functools.partial(_attn_kernel, layer),
        grid=(batch, nb),
        in_specs=[
            pl.BlockSpec((BLOCK, ATT_WIDTH), lambda b, t: (b * nb + t, 0)),
            kv_spec(kcol, -1), kv_spec(kcol, 0), kv_spec(kcol, 1),
            kv_spec(vcol, -1), kv_spec(vcol, 0), kv_spec(vcol, 1),
            pl.BlockSpec((ATT_Q_HEADS, BLOCK, 3 * BLOCK), lambda b, t: (0, 0, 0)),
            pl.BlockSpec(memory_space=pltpu.SMEM),
            _layer_block((1, ATT_WIDTH), layer),
        ],
        out_specs=pl.BlockSpec((BLOCK, ATT_WIDTH), lambda b, t: (b * nb + t, 0)),
        out_shape=jax.ShapeDtypeStruct((n, ATT_WIDTH), BF16),
        scratch_shapes=[pltpu.VMEM((ATT_Q_HEADS, BLOCK, 3 * BLOCK), F32)],
        compiler_params=_cparams(("arbitrary", "arbitrary")),
        name="window_attention",
    )(qkv, qkv, qkv, qkv, qkv, qkv, qkv, bias, sink, gain)


def _t5_bias(rel_bias_table):
    i = jnp.arange(BLOCK)[:, None]
    j = jnp.arange(3 * BLOCK)[None, :]
    rel = j - BLOCK - i
    nbk = N_BUCKETS // 2
    max_exact = nbk // 2
    ret = jnp.where(rel > 0, nbk, 0)
    nabs = jnp.abs(rel)
    large = max_exact + (jnp.log(jnp.maximum(nabs, max_exact).astype(jnp.float32) / max_exact)
                         / math.log(MAX_DISTANCE / max_exact) * (nbk - max_exact)).astype(jnp.int32)
    large = jnp.minimum(large, nbk - 1)
    bucket = ret + jnp.where(nabs < max_exact, nabs, large)
    onehot = (bucket[None, :, :] == jnp.arange(N_BUCKETS)[:, None, None]).astype(F32)
    bias = jnp.einsum("bh,bij->hij", rel_bias_table.astype(F32), onehot, precision=HI)
    return jnp.where((nabs <= WINDOW)[None], bias * LOG2E, NEG_INF)


def _outproj_kernel(yr, ya, w1, w2, h_ref, o_ref):
    o_ref[...] = h_ref[...] + _dot(yr[...], w1[...]) + _dot(ya[...], w2[...])


def out_proj(y_r, y_a, w, layer, h, tm=512):
    n, d = h.shape
    c = y_r.shape[1]
    return pl.pallas_call(
        _outproj_kernel,
        grid=(n // tm,),
        in_specs=[
            pl.BlockSpec((tm, c), lambda i: (i, 0)),
            pl.BlockSpec((tm, c), lambda i: (i, 0)),
            _layer_block((c, d), layer, (0, 0)),
            _layer_block((c, d), layer, (1, 0)),
            pl.BlockSpec((tm, d), lambda i: (i, 0)),
        ],
        out_specs=pl.BlockSpec((tm, d), lambda i: (i, 0)),
        out_shape=jax.ShapeDtypeStruct((n, d), F32),
        compiler_params=_cparams(("parallel",)),
        name="out_proj",
    )(y_r, y_a, w, w, h)


def _xatt_kernel(h_ref, g_ref, wq, kv_ref, wo, gm_ref, wr_ref, br_ref, o_ref, slab_ref):
    h = h_ref[...]
    hn = _rms(h, g_ref[...]).astype(BF16)
    q = (_dot(hn, wq[...]) * (XATT_HEAD_DIM ** -0.5 * LOG2E)).astype(BF16)
    kv = kv_ref[...]
    outs = []
    for hd in range(XATT_HEADS):
        sl = slice(hd * XATT_HEAD_DIM, (hd + 1) * XATT_HEAD_DIM)
        k_h = kv[:, sl].astype(BF16)
        v_h = kv[:, XATT_WIDTH + hd * XATT_HEAD_DIM: XATT_WIDTH + (hd + 1) * XATT_HEAD_DIM].astype(BF16)
        s = _dot_nt(q[:, sl], k_h)
        m = jnp.max(s, axis=-1, keepdims=True)
        p = jnp.exp2(s - m)
        p = p / jnp.sum(p, axis=-1, keepdims=True)
        outs.append(_dot(p.astype(BF16), v_h))
    o = jnp.concatenate(outs, axis=-1).astype(BF16)
    h_new = h + _dot(o, wo[...])
    o_ref[...] = h_new
    slab_ref[...] = _route(h_new, gm_ref[...], wr_ref[...], br_ref[...])


def cross_attention_router(h, g, wq, kv, wo, g_moe, w_r, b_r, layer, batch, seq_len, tm=512):
    n, d = h.shape
    m = kv.shape[1] // batch
    tiles = seq_len // tm
    return pl.pallas_call(
        _xatt_kernel,
        grid=(n // tm,),
        in_specs=[
            pl.BlockSpec((tm, d), lambda i: (i, 0)),
            _layer_block((1, d), layer),
            _layer_block((d, XATT_WIDTH), layer),
            pl.BlockSpec((None, m, 2 * XATT_WIDTH), lambda i: (layer, i // tiles, 0)),
            _layer_block((XATT_WIDTH, d), layer),
            _layer_block((1, d), layer),
            _layer_block((d, 2 * LANES), layer),
            _layer_block((1, LANES), layer),
        ],
        out_specs=[pl.BlockSpec((tm, d), lambda i: (i, 0)), pl.BlockSpec((tm, LANES), lambda i: (i, 0))],
        out_shape=[jax.ShapeDtypeStruct((n, d), F32), jax.ShapeDtypeStruct((n, LANES), F32)],
        compiler_params=_cparams(("parallel",)),
        name="cross_attention_router",
    )(h, g, wq, kv, wo, g_moe, w_r, b_r)


def _route(h, g, w2, b):
    hn_hi, hn_lo = _split_bf16(_rms(h, g))
    both = _dot(hn_hi, w2) + _dot(hn_lo, w2)
    logits = both[:, 0:LANES] + both[:, LANES:] + b
    tm = logits.shape[0]
    lane = lax.broadcasted_iota(jnp.int32, (tm, LANES), 1)
    is_coarse = (lane >= N_EXPERTS) & (lane < N_EXPERTS + N_GROUPS)
    cl = jnp.where(is_coarse, logits, NEG_INF)
    cmax = jnp.max(cl, axis=-1, keepdims=True)
    csum = jnp.sum(jnp.where(is_coarse, jnp.exp(cl - cmax), 0.0), axis=-1, keepdims=True)
    p_g = 1.0 / csum
    lane_f = lane.astype(F32)
    grp_f = (lane >> GROUP_SHIFT).astype(F32)
    big = float(LANES)
    g_lane = jnp.min(jnp.where(is_coarse & (cl == cmax), lane_f, big), axis=-1, keepdims=True)
    g_idx = g_lane - float(N_EXPERTS)
    in_grp = (lane < N_EXPERTS) & (grp_f == g_idx)
    fl = jnp.where(in_grp, logits, NEG_INF)
    m1 = jnp.max(fl, axis=-1, keepdims=True)
    i1 = jnp.min(jnp.where(in_grp & (fl == m1), lane_f, big), axis=-1, keepdims=True)
    fl2 = jnp.where(lane_f == i1, NEG_INF, fl)
    m2 = jnp.max(fl2, axis=-1, keepdims=True)
    i2 = jnp.min(jnp.where(in_grp & (lane_f != i1) & (fl2 == m2), lane_f, big), axis=-1, keepdims=True)
    e2 = jnp.exp(m2 - m1)
    w1 = p_g / (1.0 + e2)
    w2 = p_g * e2 / (1.0 + e2)
    return jnp.where(lane == 0, i1,
                     jnp.where(lane == 1, i2,
                               jnp.where(lane == 2, w1, jnp.where(lane == 3, w2, 0.0))))


GATHER_UNROLL = 8


def _start_row_gather(src_hbm, dst, sem, idx_ref, first, stride, n_rows):
    def body(r, carry):
        row = idx_ref[first + r * stride]
        pltpu.make_async_copy(src_hbm.at[pl.ds(row, 1), :], dst.at[pl.ds(r, 1), :], sem).start()
        return carry

    lax.fori_loop(0, n_rows, body, 0, unroll=GATHER_UNROLL)


def _wait_row_gather(dst, sem):
    pltpu.make_async_copy(dst, dst, sem).wait()


def _start_row_groups(src_hbm, dst, sem, idx_ref, first, n_groups):
    def body(grp, carry):
        for j in range(GATHER_UNROLL):
            r = grp * GATHER_UNROLL + j
            row = idx_ref[first + r]
            pltpu.make_async_copy(src_hbm.at[pl.ds(row, 1), :], dst.at[pl.ds(r, 1), :], sem).start()
        return carry

    lax.fori_loop(0, n_groups, body, 0)


def _wait_row_groups(dst, sem, n_groups):
    def body(grp, carry):
        blk = dst.at[pl.ds(0, GATHER_UNROLL), :]
        pltpu.make_async_copy(blk, blk, sem).wait()
        return carry

    lax.fori_loop(0, n_groups, body, 0)


def _expert_kernel(e0, te_ref, nxt_ref, par_ref, grp_ref, nused_ref, tok_ref, h_hbm, g_ref, wg_hbm, wu_hbm, wd_hbm,
                   o_ref, xbuf, wgf, wuf, wdf, wgb, wub, wdb, xsem, wsem):
    t = pl.program_id(0)
    nslots, tm = xbuf.shape[0], xbuf.shape[1]
    ahead = nslots - 1
    n_used = nused_ref[0]
    slot = lax.rem(t, nslots)

    def weight_copies(expert, ws):
        return [pltpu.make_async_copy(src.at[e0 + expert], dst.at[ws], wsem.at[ws, j])
                for j, (src, dst) in enumerate(((wg_hbm, wgf), (wu_hbm, wuf), (wd_hbm, wdf)))]

    @pl.when(t == 0)
    def _():
        for cp in weight_copies(te_ref[0], 0):
            cp.start()
        xbuf[...] = jnp.zeros_like(xbuf)
        for j in range(ahead):
            @pl.when(j < n_used)
            def _():
                _start_row_groups(h_hbm, xbuf.at[j], xsem.at[j], tok_ref, j * tm, grp_ref[j])

    @pl.when(t + ahead < n_used)
    def _():
        nslot = lax.rem(t + ahead, nslots)
        _start_row_groups(h_hbm, xbuf.at[nslot], xsem.at[nslot], tok_ref, (t + ahead) * tm, grp_ref[t + ahead])

    @pl.when((t == 0) | (te_ref[t] != te_ref[jnp.maximum(t - 1, 0)]))
    def _():
        ws = par_ref[t]
        for cp in weight_copies(te_ref[t], ws):
            cp.wait()

        @pl.when(nxt_ref[t] >= 0)
        def _():
            for cp in weight_copies(nxt_ref[t], 1 - ws):
                cp.start()

        wgb[...] = wgf[ws].astype(BF16)
        wub[...] = wuf[ws].astype(BF16)
        wdb[...] = wdf[ws].astype(BF16)

    @pl.when(t < n_used)
    def _():
        _wait_row_groups(xbuf.at[slot], xsem.at[slot], grp_ref[t])
        xn = _rms(xbuf[slot], g_ref[...]).astype(BF16)
        hg = _dot(xn, wgb[...])
        hu = _dot(xn, wub[...])
        act = (hg * _sigmoid(hg) * hu).astype(BF16)
        o_ref[...] = _dot(act, wdb[...])

    @pl.when(t >= n_used)
    def _():
        o_ref[...] = jnp.zeros_like(o_ref)


def moe_experts(h, g, w_gate, w_up, w_down, layer, tables, n_tiles):
    n, d = h.shape
    tm = MOE_TILE
    tile_expert, next_expert, parity, tile_groups, n_used, row_tok = tables
    grid_spec = pltpu.PrefetchScalarGridSpec(
        num_scalar_prefetch=6,
        grid=(n_tiles,),
        in_specs=[
            pl.BlockSpec(memory_space=pl.ANY),
            _layer_block((1, d), layer),
            pl.BlockSpec(memory_space=pl.ANY),
            pl.BlockSpec(memory_space=pl.ANY),
            pl.BlockSpec(memory_space=pl.ANY),
        ],
        out_specs=pl.BlockSpec((tm, d), lambda t, *_: (t, 0)),
        scratch_shapes=[pltpu.VMEM((MOE_ROW_SLOTS, tm, d), F32),
                        pltpu.VMEM((2, d, D_EXPERT), F32), pltpu.VMEM((2, d, D_EXPERT), F32),
                        pltpu.VMEM((2, D_EXPERT, d), F32),
                        pltpu.VMEM((d, D_EXPERT), BF16), pltpu.VMEM((d, D_EXPERT), BF16),
                        pltpu.VMEM((D_EXPERT, d), BF16),
                        pltpu.SemaphoreType.DMA((MOE_ROW_SLOTS,)), pltpu.SemaphoreType.DMA((2, 3))],
    )
    return pl.pallas_call(
        functools.partial(_expert_kernel, layer * N_EXPERTS),
        grid_spec=grid_spec,
        out_shape=jax.ShapeDtypeStruct((n_tiles * tm, d), F32),
        compiler_params=_cparams(("arbitrary",)),
        name="moe_experts",
    )(tile_expert, next_expert, parity, tile_groups, n_used, row_tok, h, g, w_gate, w_up, w_down)


def _combine_kernel(final, pos_ref, y_hbm, h_ref, slab_ref, fg_ref, o_ref, buf, sem):
    i = pl.program_id(0)
    n_steps = pl.num_programs(0)
    tm = h_ref.shape[0]
    slot = i % 2

    def start(step, sl):
        for j in range(2):
            _start_row_gather(y_hbm, buf.at[sl, j], sem.at[sl, j], pos_ref, 2 * step * tm + j, 2, tm)

    @pl.when(i == 0)
    def _():
        start(0, 0)

    @pl.when(i + 1 < n_steps)
    def _():
        start(i + 1, 1 - slot)

    for j in range(2):
        _wait_row_gather(buf.at[slot, j], sem.at[slot, j])
    slab = slab_ref[...]
    out = h_ref[...] + slab[:, 2:3] * buf[slot, 0] + slab[:, 3:4] * buf[slot, 1]
    if final:
        out = _rms(out, fg_ref[...])
    o_ref[...] = out


def moe_combine(y_sorted, pos, h, slab, final_gain, final, tm=256):
    n, d = h.shape
    grid_spec = pltpu.PrefetchScalarGridSpec(
        num_scalar_prefetch=1,
        grid=(n // tm,),
        in_specs=[
            pl.BlockSpec(memory_space=pl.ANY),
            pl.BlockSpec((tm, d), lambda i, p: (i, 0)),
            pl.BlockSpec((tm, LANES), lambda i, p: (i, 0)),
            pl.BlockSpec((1, d), lambda i, p: (0, 0)),
        ],
        out_specs=pl.BlockSpec((tm, d), lambda i, p: (i, 0)),
        scratch_shapes=[pltpu.VMEM((2, 2, tm, d), F32), pltpu.SemaphoreType.DMA((2, 2))],
    )
    return pl.pallas_call(
        functools.partial(_combine_kernel, final),
        grid_spec=grid_spec,
        out_shape=jax.ShapeDtypeStruct((n, d), F32),
        compiler_params=_cparams(("arbitrary",)),
        name="moe_combine",
    )(pos, y_sorted, h, slab, final_gain.reshape(1, d))


def _invert_kernel(pos_ref, zeros_hbm, out_ref, sem):
    fill = pltpu.make_async_copy(zeros_hbm, out_ref, sem)
    fill.start()
    fill.wait()
    n_pairs = pos_ref.shape[0]

    def place(grp, carry):
        for j in range(GATHER_UNROLL):
            i = grp * GATHER_UNROLL + j
            out_ref[pos_ref[i]] = i >> 1
        return carry

    lax.fori_loop(0, n_pairs // GATHER_UNROLL, place, 0)


def invert_rows(pos, n_rows):
    return pl.pallas_call(
        _invert_kernel,
        in_specs=[pl.BlockSpec(memory_space=pltpu.SMEM), pl.BlockSpec(memory_space=pl.ANY)],
        out_specs=pl.BlockSpec(memory_space=pltpu.SMEM),
        out_shape=jax.ShapeDtypeStruct((n_rows,), jnp.int32),
        scratch_shapes=[pltpu.SemaphoreType.DMA(())],
        name="invert_rows",
    )(pos, jnp.zeros((n_rows,), jnp.int32))


def _routing_tables(slab, n_tiles):
    n = slab.shape[0]
    tm = MOE_TILE
    e = slab[:, 0:2].astype(jnp.int32).reshape(-1)
    onehot = (e[:, None] == jnp.arange(N_EXPERTS, dtype=jnp.int32)[None, :]).astype(jnp.int32)
    csum = jnp.cumsum(onehot, axis=0)
    counts = csum[-1]
    rank = jnp.take_along_axis(csum, e[:, None], axis=1)[:, 0] - 1
    tiles_per = (counts + tm - 1) // tm
    tile_end = jnp.cumsum(tiles_per)
    tile_start = tile_end - tiles_per
    pos = tile_start[e] * tm + rank
    row_tok = invert_rows(pos.astype(jnp.int32), n_tiles * tm)
    n_used = tile_end[-1]
    t_idx = jnp.arange(n_tiles, dtype=jnp.int32)
    tile_expert = jnp.searchsorted(tile_end, jnp.minimum(t_idx, n_used - 1), side="right").astype(jnp.int32)
    tile_expert = jnp.minimum(tile_expert, N_EXPERTS - 1)
    after = tile_end[tile_expert]
    next_expert = jnp.where(after < n_used, tile_expert[jnp.minimum(after, n_tiles - 1)], -1).astype(jnp.int32)
    change = jnp.concatenate([jnp.zeros((1,), jnp.int32), (tile_expert[1:] != tile_expert[:-1]).astype(jnp.int32)])
    parity = (jnp.cumsum(change) % 2).astype(jnp.int32)
    rows_left = counts[tile_expert] - (t_idx - tile_start[tile_expert]) * tm
    tile_groups = jnp.where(t_idx < n_used, (jnp.clip(rows_left, 0, tm) + GATHER_UNROLL - 1) // GATHER_UNROLL, 0)
    tables = (tile_expert, next_expert, parity, tile_groups.astype(jnp.int32),
              n_used.reshape(1).astype(jnp.int32), row_tok)
    return tables, pos.astype(jnp.int32)


def _split_in_weights(w):
    c = C_RWKV
    o = 3 * c
    wd0 = w[..., o:o + DECAY_RANK]
    wd1 = w[..., o + DECAY_RANK:o + 2 * DECAY_RANK]
    o += 2 * DECAY_RANK
    ad0 = w[..., o:o + ICLR_RANK]
    ad1 = w[..., o + ICLR_RANK:o + 2 * ICLR_RANK]
    o += 2 * ICLR_RANK
    gd = w[..., o:o + GATE_RANK]
    o += GATE_RANK
    att = w[..., o:]

    def padl(x):
        pad = [(0, 0)] * (x.ndim - 1) + [(0, RANK_PAD - x.shape[-1])]
        return jnp.pad(x, pad)

    rw = jnp.concatenate([w[..., :3 * c], padl(wd0), padl(wd1), padl(ad0), padl(ad1), gd], axis=-1)
    return rw, att


def kernel(x, mem, w_in, shift_prev, shift_next, decay_w0, decay_w2, iclr_a0, iclr_a2, gate_w2, vres_v0, vres_w1, vres_w2, k_k, k_a, r_k, ln_x_gain, ln_x_bias, att_sink, att_out_gain, rel_bias_table, w_out, norm_mix, norm_xatt, mem_norm, xatt_wq, xatt_wk, xatt_wv, xatt_wo, norm_moe, router_coarse_w, router_coarse_b, router_fine_w, router_fine_b, expert_w_gate, expert_w_up, expert_w_down, final_norm):
    batch, seq_len, d = x.shape
    depth = w_in.shape[0]
    n = batch * seq_len
    mem_len = mem.shape[1]
    h = x.reshape(n, d)
    memf = mem.reshape(batch * mem_len, d)
    bias = _t5_bias(rel_bias_table)
    n_tiles = (2 * n) // MOE_TILE + N_EXPERTS

    def row(p):
        return p[:, None, :]

    def pad_axis(p, axis, size):
        pad = [(0, 0)] * p.ndim
        pad[axis] = (0, size - p.shape[axis])
        return jnp.pad(p, pad)

    w_rw, w_at = _split_in_weights(w_in.astype(BF16))
    w_in_all = jnp.concatenate([w_rw, w_at], axis=-1)
    prep_params = (
        row(_split_in_weights(shift_prev)[0]), row(_split_in_weights(shift_next)[0]),
        decay_w0, pad_axis(decay_w2, 2, RANK_PAD).astype(BF16),
        iclr_a0, pad_axis(iclr_a2, 2, RANK_PAD).astype(BF16),
        gate_w2.astype(BF16), row(k_k), row(k_a), row(r_k))
    vres_params = (row(vres_v0), pad_axis(vres_w1, 2, RANK_PAD).astype(BF16),
                   pad_axis(vres_w2, 1, RANK_PAD).astype(BF16))
    ln_g, ln_b = row(ln_x_gain), row(ln_x_bias)
    att_gain = row(att_out_gain)
    w_out_all = w_out.astype(BF16)
    g_mix, g_xatt, g_mem, g_moe = row(norm_mix), row(norm_xatt), row(mem_norm), row(norm_moe)
    wq_all = xatt_wq.astype(BF16)
    wkv_all = jnp.concatenate([xatt_wk, xatt_wv], axis=-1).astype(BF16)
    wo_all = xatt_wo.astype(BF16)
    w_r = pad_axis(jnp.concatenate([router_fine_w, router_coarse_w], axis=-1), 2, LANES)
    w_r2_all = jnp.concatenate(_split_bf16(w_r), axis=-1)
    b_r_all = row(pad_axis(jnp.concatenate([router_fine_b.reshape(depth, -1), router_coarse_b], axis=-1), 1, LANES))
    wg_all = expert_w_gate.reshape(depth * N_EXPERTS, d, D_EXPERT)
    wu_all = expert_w_up.reshape(depth * N_EXPERTS, d, D_EXPERT)
    wd_all = expert_w_down.reshape(depth * N_EXPERTS, D_EXPERT, d)

    kv = norm_matmul(memf, g_mem, wkv_all, memf.shape[0], 2 * XATT_WIDTH)
    v_first = None
    for l in range(depth):
        u, qkv = in_proj(h, g_mix, w_in_all, l, RW_COLS)
        r, v, kk, b0, b1, lw0, lw1, kd0, kd1, g, bg = rwkv_prep(
            u, seq_len, l, prep_params, vres_params if l > 0 else None, v_first)
        if l == 0:
            v_first = v
        yf, yb = rwkv_chunk(r, v, kk, b0, b1, lw0, lw1, kd0, kd1, batch, seq_len)
        y_r = rwkv_post(yf, yb, g, bg, ln_g, ln_b, l)
        y_a = window_attention(qkv, bias, att_sink, att_gain, l, batch, seq_len)
        h = out_proj(y_r, y_a, w_out_all, l, h)
        h, slab = cross_attention_router(h, g_xatt, wq_all, kv, wo_all, g_moe, w_r2_all, b_r_all, l, batch, seq_len)
        tables, pos = _routing_tables(slab, n_tiles)
        y_sorted = moe_experts(h, g_moe, wg_all, wu_all, wd_all, l, tables, n_tiles)
        h = moe_combine(y_sorted, pos, h, slab, final_norm, l == depth - 1)
    return h.reshape(batch, seq_len, d)
```
